```python
import functools
import jax
import jax.numpy as jnp
from jax import lax
import numpy as np

D_MODEL = 2048
BATCH = 2
SEQ = 4096
DEPTH = 2

GRID_W = 64
CTX_LEN = 256
N_MOD = 6
NORM_EPS = 1e-6

V_HEAD = 128
MLA_HEADS = D_MODEL // (2 * V_HEAD)
Q_LORA = D_MODEL // 4
KV_LORA = D_MODEL // 4
QK_NOPE = 128
QK_ROPE = 64
ROPE_FREQS = QK_ROPE // 4
ROPE_THETA = 10000.0
Q_BLOCK = 128
MLA_IN = Q_LORA + KV_LORA + QK_ROPE

RWKV_HEAD = 64
RWKV_HEADS = D_MODEL // (2 * RWKV_HEAD)
RWKV_DIM = RWKV_HEADS * RWKV_HEAD
DECAY_LORA = 64
ICLR_LORA = 64
GATE_LORA = 160
LNX_EPS = 64e-5
RWKV_IN = 3 * RWKV_DIM + 2 * DECAY_LORA + 2 * ICLR_LORA + GATE_LORA
EVEN_SPLITS = (Q_LORA, Q_LORA + KV_LORA, MLA_IN)
RWKV_SPLITS = (RWKV_DIM, 2 * RWKV_DIM, 3 * RWKV_DIM, 3 * RWKV_DIM + 2 * DECAY_LORA,
               3 * RWKV_DIM + 2 * DECAY_LORA + 2 * ICLR_LORA)
SCAN_REVERSE = (False, True)
MIX_DIM = MLA_HEADS * V_HEAD + RWKV_DIM

CHUNK = 128
SG_DIM = D_MODEL
SG_GROUPS = 16
SG_CH = SG_DIM // SG_GROUPS

D_FF = 7 * D_MODEL // 2
N_EXPERTS = 8
TOP_K = 2
MOE_BLOCK = 128

kernel_name = 'hybrid_mla_rwkv7_gmlp_moe_dit'


def rms_norm(x, g):
    xf = x.astype(jnp.float32)
    y = xf * lax.rsqrt(jnp.mean(xf * xf, axis=-1, keepdims=True) + NORM_EPS)
    return (y * g.astype(jnp.float32)).astype(x.dtype)


def layer_norm(x, g, b, eps=1e-5):
    xf = x.astype(jnp.float32)
    mu = jnp.mean(xf, axis=-1, keepdims=True)
    var = jnp.mean(jnp.square(xf - mu), axis=-1, keepdims=True)
    return ((xf - mu) * lax.rsqrt(var + eps) * g + b).astype(x.dtype)


def modulate(x, shift, scale):
    return x * (1 + scale) + shift


def ada_mod(cond, w, b):
    return jnp.split(jax.nn.silu(cond) @ w + b, N_MOD, axis=-1)


def axial_rope_tables(n_tokens):
    rows = n_tokens // GRID_W
    row = jnp.repeat(jnp.arange(rows), GRID_W).astype(jnp.float32)
    col = jnp.tile(jnp.arange(GRID_W), rows).astype(jnp.float32)
    inv = ROPE_THETA ** (-jnp.arange(ROPE_FREQS, dtype=jnp.float32) / ROPE_FREQS)
    ang = jnp.stack([row[:, None] * inv, col[:, None] * inv], axis=1)
    return jnp.cos(ang), jnp.sin(ang)


def axial_rope(x, cos, sin):
    xs = x.reshape(x.shape[:-1] + (2, 2, ROPE_FREQS))
    x1, x2 = xs[..., 0, :], xs[..., 1, :]
    out = jnp.stack([x1 * cos - x2 * sin, x2 * cos + x1 * sin], axis=-2)
    return out.reshape(x.shape).astype(x.dtype)


def mla_queries(c_q, q_norm, w_uq):
    B, L, _ = c_q.shape
    q = (rms_norm(c_q, q_norm) @ w_uq).reshape(B, L, MLA_HEADS, QK_NOPE + QK_ROPE)
    return q[..., :QK_NOPE], q[..., QK_NOPE:]


def mla_keys_values(c_kv, kv_norm, w_ukv):
    B, L, _ = c_kv.shape
    kv = (rms_norm(c_kv, kv_norm) @ w_ukv).reshape(B, L, MLA_HEADS, QK_NOPE + V_HEAD)
    return kv[..., :QK_NOPE], kv[..., QK_NOPE:]


def block_attention(q_nope, q_rope, k_nope, k_rope, v):
    B, L, H, _ = q_nope.shape
    nb = L // Q_BLOCK
    scale = (QK_NOPE + QK_ROPE) ** -0.5

    def to_blocks(t):
        return jnp.moveaxis(t.reshape((B, nb, Q_BLOCK) + t.shape[2:]), 1, 0)

    def one_block(qb):
        qn, qr = qb
        s = (jnp.einsum('bqhd,bkhd->bhqk', qn, k_nope)
             + jnp.einsum('bqhr,bkr->bhqk', qr, k_rope))
        prob = jax.nn.softmax(s.astype(jnp.float32) * scale, axis=-1).astype(v.dtype)
        return jnp.einsum('bhqk,bkhd->bqhd', prob, v)

    o = lax.map(one_block, (to_blocks(q_nope), to_blocks(q_rope)))
    return jnp.moveaxis(o, 0, 1).reshape(B, L, H * V_HEAD)


def centred_shift(z, mu_prev, mu_next):
    z_prev = jnp.pad(z, ((0, 0), (1, 0), (0, 0)))[:, :-1]
    z_next = jnp.pad(z, ((0, 0), (0, 1), (0, 0)))[:, 1:]
    return z + mu_prev * (z_prev - z) + mu_next * (z_next - z)


def rwkv_prep(z, p, with_output):
    B, L, _ = z.shape
    z = centred_shift(z, p['mu_prev'], p['mu_next']).astype(jnp.float32)
    r, k, v, wd, ad, gd = jnp.split(z, RWKV_SPLITS, axis=-1)

    def heads(t):
        return t.reshape(B, L, RWKV_HEADS, RWKV_HEAD)

    kk = heads(k * p['k_k'])
    kk = kk / jnp.maximum(jnp.linalg.norm(kk, axis=-1, keepdims=True), 1e-12)
    dirs = []
    for d in range(2):
        w_log = -jax.nn.softplus(-(p['w0'][d] + jnp.tanh(wd[..., d * DECAY_LORA:(d + 1) * DECAY_LORA]) @ p['w_up'][d])) - 0.5
        decay = jnp.exp(-jnp.exp(w_log))
        iclr = jax.nn.sigmoid(p['a0'][d] + ad[..., d * ICLR_LORA:(d + 1) * ICLR_LORA] @ p['a_up'][d])
        k_d = k * (1 + (iclr - 1) * p['k_a'])
        dirs.append((heads(decay), heads(k_d), heads(iclr) * kk))
    extra = None
    if with_output:
        extra = (heads(r), jax.nn.sigmoid(gd) @ p['g_up'])
    return heads(v), -kk, dirs, extra


def wkv_scan(state, decay, k, b, v, a, r, reverse):
    with_out = r is not None
    seqs = (decay, k, b, v, a) + ((r,) if with_out else ())
    xs = tuple(jnp.moveaxis(t, 1, 0) for t in seqs)

    def step(s, inp):
        w_t, k_t, b_t, v_t, a_t = inp[:5]
        sa = jnp.einsum('bhvk,bhk->bhv', s, a_t)
        s = s * w_t[:, :, None, :] + sa[..., None] * b_t[:, :, None, :] + v_t[..., None] * k_t[:, :, None, :]
        return s, (jnp.einsum('bhvk,bhk->bhv', s, inp[5]) if with_out else None)

    s, ys = lax.scan(step, state, xs, reverse=reverse)
    return s, (jnp.moveaxis(ys, 0, 1) if with_out else None)


def rwkv_readout(y, r, k_sum, v, g, p):
    B, L = y.shape[:2]
    mu = jnp.mean(y, axis=-1, keepdims=True)
    var = jnp.mean(jnp.square(y - mu), axis=-1, keepdims=True)
    yn = ((y - mu) * lax.rsqrt(var + LNX_EPS)).reshape(B, L, RWKV_DIM) * p['lnx_w'] + p['lnx_b']
    bonus = jnp.sum(r * k_sum * p['r_k'], axis=-1, keepdims=True) * v
    return (yn + bonus.reshape(B, L, RWKV_DIM)) * g


def mla_rwkv_mixer(xl, xc, p, cos, sin, ctx_out):
    B = xl.shape[0]
    cq_l, ckv_l, kr_l, zr_l = jnp.split(xl @ p['w_in'], EVEN_SPLITS, axis=-1)
    cq_c, ckv_c, kr_c, zr_c = jnp.split(xc @ p['w_in'], EVEN_SPLITS, axis=-1)
    kn_l, va_l = mla_keys_values(ckv_l, p['kv_norm'], p['w_ukv'])
    kn_c, va_c = mla_keys_values(ckv_c, p['kv_norm'], p['w_ukv'])
    qn_l, qr_l = mla_queries(cq_l, p['q_norm'], p['w_uq'])
    qr_l = axial_rope(qr_l, cos[:, None], sin[:, None])
    kr_l = axial_rope(kr_l, cos, sin)
    att_l = block_attention(qn_l, qr_l,
                            jnp.concatenate([kn_c, kn_l], axis=1),
                            jnp.concatenate([kr_c, kr_l], axis=1),
                            jnp.concatenate([va_c, va_l], axis=1))
    vr_l, a_l, dirs_l, (r_l, g_l) = rwkv_prep(zr_l, p, True)
    vr_c, a_c, dirs_c, extra_c = rwkv_prep(zr_c, p, ctx_out)
    r_c = extra_c[0] if ctx_out else None
    s0 = jnp.zeros((B, RWKV_HEADS, RWKV_HEAD, RWKV_HEAD), jnp.float32)
    y_l, y_c = 0.0, 0.0
    for d, rev in enumerate(SCAN_REVERSE):
        s_ctx, yc_d = wkv_scan(s0, *dirs_c[d], vr_c, a_c, r_c, rev)
        _, yl_d = wkv_scan(s_ctx, *dirs_l[d], vr_l, a_l, r_l, rev)
        y_l = y_l + yl_d
        if ctx_out:
            y_c = y_c + yc_d
    rw_l = rwkv_readout(y_l, r_l, dirs_l[0][1] + dirs_l[1][1], vr_l, g_l, p)
    out_l = jnp.concatenate([att_l, rw_l.astype(att_l.dtype)], axis=-1) @ p['w_o']
    out_c = None
    if ctx_out:
        qn_c, qr_c = mla_queries(cq_c, p['q_norm'], p['w_uq'])
        att_c = block_attention(qn_c, qr_c, kn_c, kr_c, va_c)
        rw_c = rwkv_readout(y_c, r_c, dirs_c[0][1] + dirs_c[1][1], vr_c, extra_c[1], p)
        out_c = jnp.concatenate([att_c, rw_c.astype(att_c.dtype)], axis=-1) @ p['w_o']
    return out_l, out_c


def chunk_gmlp(x, p):
    B, L, _ = x.shape
    u, v = jnp.split(jax.nn.gelu(x @ p['w_in'], approximate=False), 2, axis=-1)
    v = layer_norm(v, p['v_ln_w'], p['v_ln_b'])
    vc = v.reshape(B, L // CHUNK, CHUNK, SG_GROUPS, SG_CH)
    vm = jnp.einsum('gpq,bnqgc->bnpgc', p['w_s'], vc) + p['b_s'].T[:, :, None]
    return (u * vm.reshape(B, L, SG_DIM)) @ p['w_o']


def swiglu(x, w1, w3, w2):
    return (jax.nn.silu(x @ w1) * (x @ w3)) @ w2


def moe_swiglu(x, router, w1, w3, w2):
    shape = x.shape
    h = x.reshape(-1, shape[-1])
    n_tok = h.shape[0]
    logits = (h @ router).astype(jnp.float32)
    top_val, top_idx = lax.top_k(logits, TOP_K)
    gate = jax.nn.softmax(top_val, axis=-1)
    n_assign = n_tok * TOP_K
    e_flat = top_idx.reshape(-1)
    tok_flat = jnp.repeat(jnp.arange(n_tok), TOP_K)
    g_flat = gate.reshape(-1)
    order = jnp.argsort(e_flat)
    e_sorted = e_flat[order]
    counts = jnp.bincount(e_flat, length=N_EXPERTS)
    padded = (counts + MOE_BLOCK - 1) // MOE_BLOCK * MOE_BLOCK
    pad_end = jnp.cumsum(padded)
    pad_start = pad_end - padded
    grp_start = jnp.cumsum(counts) - counts
    dest = pad_start[e_sorted] + jnp.arange(n_assign) - grp_start[e_sorted]
    n_blocks = -(-(n_assign + N_EXPERTS * (MOE_BLOCK - 1)) // MOE_BLOCK)
    n_rows = n_blocks * MOE_BLOCK
    row_tok = jnp.zeros((n_rows,), jnp.int32).at[dest].set(tok_flat[order])
    row_gate = jnp.zeros((n_rows,), gate.dtype).at[dest].set(g_flat[order])
    block_exp = jnp.minimum(jnp.searchsorted(pad_end, jnp.arange(n_blocks) * MOE_BLOCK, side='right'), N_EXPERTS - 1)
    xb = h[row_tok].reshape(n_blocks, MOE_BLOCK, shape[-1])

    def expert_block(args):
        xblk, e = args
        return (jax.nn.silu(xblk @ w1[e]) * (xblk @ w3[e])) @ w2[e]

    yb = lax.map(expert_block, (xb, block_exp)).reshape(n_rows, -1)
    y = jax.ops.segment_sum(yb * row_gate[:, None].astype(yb.dtype), row_tok, num_segments=n_tok)
    return y.reshape(shape).astype(x.dtype)


def setup_inputs(seed: int = 0) -> dict:
    key = jax.random.key(seed)
    ks = iter(jax.random.split(key, 64))
    f32 = jnp.float32

    def nrm(shape, s):
        return jax.random.normal(next(ks), shape, f32) * s

    def unif(shape, lo, hi):
        return jax.random.uniform(next(ks), shape, f32, lo, hi)

    def gain(n):
        return 1.0 + nrm((n,), 0.02)

    D = D_MODEL
    return {
        'x': nrm((BATCH, SEQ, D), 1.0),
        'c': nrm((BATCH, D), 1.0),
        'ctx': nrm((BATCH, CTX_LEN, D), 1.0),
        'c_ctx': nrm((D,), 1.0),
        'l0_ada_w': nrm((D, N_MOD * D), 0.5 * D ** -0.5),
        'l0_ada_b': nrm((N_MOD * D,), 0.02),
        'l0_norm1': gain(D),
        'l0_norm2': gain(D),
        'l0_w_in': nrm((D, MLA_IN + RWKV_IN), D ** -0.5),
        'l0_q_norm': gain(Q_LORA),
        'l0_w_uq': nrm((Q_LORA, MLA_HEADS * (QK_NOPE + QK_ROPE)), Q_LORA ** -0.5),
        'l0_kv_norm': gain(KV_LORA),
        'l0_w_ukv': nrm((KV_LORA, MLA_HEADS * (QK_NOPE + V_HEAD)), KV_LORA ** -0.5),
        'l0_mu_prev': unif((RWKV_IN,), 0.0, 0.4),
        'l0_mu_next': unif((RWKV_IN,), 0.0, 0.4),
        'l0_w0': unif((2, RWKV_DIM), -6.0, 1.0),
        'l0_w_up': nrm((2, DECAY_LORA, RWKV_DIM), 0.5 * DECAY_LORA ** -0.5),
        'l0_a0': nrm((2, RWKV_DIM), 0.5),
        'l0_a_up': nrm((2, ICLR_LORA, RWKV_DIM), 0.5 * ICLR_LORA ** -0.5),
        'l0_g_up': nrm((GATE_LORA, RWKV_DIM), GATE_LORA ** -0.5),
        'l0_k_k': 0.85 + nrm((RWKV_DIM,), 0.05),
        'l0_k_a': 1.0 + nrm((RWKV_DIM,), 0.05),
        'l0_r_k': nrm((RWKV_HEADS, RWKV_HEAD), 0.1),
        'l0_lnx_w': gain(RWKV_DIM),
        'l0_lnx_b': nrm((RWKV_DIM,), 0.02),
        'l0_w_o': nrm((MIX_DIM, D), MIX_DIM ** -0.5),
        'l0_ffn_w1': nrm((D, D_FF), D ** -0.5),
        'l0_ffn_w3': nrm((D, D_FF), D ** -0.5),
        'l0_ffn_w2': nrm((D_FF, D), D_FF ** -0.5),
        'l1_ada_w': nrm((D, N_MOD * D), 0.5 * D ** -0.5),
        'l1_ada_b': nrm((N_MOD * D,), 0.02),
        'l1_norm1': gain(D),
        'l1_norm2': gain(D),
        'l1_w_in': nrm((D, 2 * SG_DIM), D ** -0.5),
        'l1_v_ln_w': gain(SG_DIM),
        'l1_v_ln_b': nrm((SG_DIM,), 0.02),
        'l1_w_s': nrm((SG_GROUPS, CHUNK, CHUNK), CHUNK ** -0.5),
        'l1_b_s': 1.0 + nrm((SG_GROUPS, CHUNK), 0.02),
        'l1_w_o': nrm((SG_DIM, D), SG_DIM ** -0.5),
        'l1_router': nrm((D, N_EXPERTS), D ** -0.5),
        'l1_moe_w1': nrm((N_EXPERTS, D, D_FF), D ** -0.5),
        'l1_moe_w3': nrm((N_EXPERTS, D, D_FF), D ** -0.5),
        'l1_moe_w2': nrm((N_EXPERTS, D_FF, D), D_FF ** -0.5),
        'final_norm': gain(D),
    }


def reference(x, c, ctx, c_ctx,
              l0_ada_w, l0_ada_b, l0_norm1, l0_norm2, l0_w_in, l0_q_norm, l0_w_uq, l0_kv_norm, l0_w_ukv,
              l0_mu_prev, l0_mu_next, l0_w0, l0_w_up, l0_a0, l0_a_up, l0_g_up, l0_k_k, l0_k_a, l0_r_k,
              l0_lnx_w, l0_lnx_b, l0_w_o, l0_ffn_w1, l0_ffn_w3, l0_ffn_w2,
              l1_ada_w, l1_ada_b, l1_norm1, l1_norm2, l1_w_in, l1_v_ln_w, l1_v_ln_b, l1_w_s, l1_b_s, l1_w_o,
              l1_router, l1_moe_w1, l1_moe_w3, l1_moe_w2,
              final_norm):
    layers = (
        dict(ada_w=l0_ada_w, ada_b=l0_ada_b, norm1=l0_norm1, norm2=l0_norm2, w_in=l0_w_in,
             q_norm=l0_q_norm, w_uq=l0_w_uq, kv_norm=l0_kv_norm, w_ukv=l0_w_ukv,
             mu_prev=l0_mu_prev, mu_next=l0_mu_next, w0=l0_w0, w_up=l0_w_up, a0=l0_a0, a_up=l0_a_up,
             g_up=l0_g_up, k_k=l0_k_k, k_a=l0_k_a, r_k=l0_r_k, lnx_w=l0_lnx_w, lnx_b=l0_lnx_b,
             w_o=l0_w_o, ffn_w1=l0_ffn_w1, ffn_w3=l0_ffn_w3, ffn_w2=l0_ffn_w2),
        dict(ada_w=l1_ada_w, ada_b=l1_ada_b, norm1=l1_norm1, norm2=l1_norm2, w_in=l1_w_in,
             v_ln_w=l1_v_ln_w, v_ln_b=l1_v_ln_b, w_s=l1_w_s, b_s=l1_b_s, w_o=l1_w_o,
             router=l1_router, moe_w1=l1_moe_w1, moe_w3=l1_moe_w3, moe_w2=l1_moe_w2),
    )
    cos, sin = axial_rope_tables(x.shape[1])
    h_lat, h_ctx = x, ctx
    for i in range(DEPTH):
        p = layers[i]
        even = i % 2 == 0
        ctx_out = any(j % 2 == 0 for j in range(i + 1, DEPTH))
        sh1, sc1, g1, sh2, sc2, g2 = [m[:, None, :] for m in ada_mod(c, p['ada_w'], p['ada_b'])]
        xl = modulate(rms_norm(h_lat, p['norm1']), sh1, sc1)
        xc = None
        if even or ctx_out:
            csh1, csc1, cg1, csh2, csc2, cg2 = ada_mod(c_ctx, p['ada_w'], p['ada_b'])
            xc = modulate(rms_norm(h_ctx, p['norm1']), csh1, csc1)
        if even:
            mix_l, mix_c = mla_rwkv_mixer(xl, xc, p, cos, sin, ctx_out)
            ffn = functools.partial(swiglu, w1=p['ffn_w1'], w3=p['ffn_w3'], w2=p['ffn_w2'])
        else:
            mix_l = chunk_gmlp(xl, p)
            mix_c = chunk_gmlp(xc, p) if ctx_out else None
            ffn = functools.partial(moe_swiglu, router=p['router'], w1=p['moe_w1'], w3=p['moe_w3'], w2=p['moe_w2'])
        h_lat = h_lat + g1 * mix_l.astype(h_lat.dtype)
        h_lat = h_lat + g2 * ffn(modulate(rms_norm(h_lat, p['norm2']), sh2, sc2)).astype(h_lat.dtype)
        if ctx_out:
            h_ctx = h_ctx + cg1 * mix_c.astype(h_ctx.dtype)
            h_ctx = h_ctx + cg2 * ffn(modulate(rms_norm(h_ctx, p['norm2']), csh2, csc2)).astype(h_ctx.dtype)
    return rms_norm(h_lat, final_norm)
```

```python
import functools

import jax
import jax.numpy as jnp
from jax import lax
from jax.experimental import pallas as pl
from jax.experimental.pallas import tpu as pltpu

F32 = jnp.float32
BF16 = jnp.bfloat16

D_MODEL = 2048
GRID_W = 64
N_MOD = 6
NORM_EPS = 1e-6
V_HEAD = 128
MLA_HEADS = 8
Q_LORA = 512
KV_LORA = 512
QK_NOPE = 128
QK_ROPE = 64
ROPE_FREQS = 16
ROPE_THETA = 10000.0
RWKV_HEAD = 64
RWKV_HEADS = 16
RWKV_DIM = 1024
DECAY_LORA = 64
ICLR_LORA = 64
GATE_LORA = 160
LNX_EPS = 64e-5
RWKV_IN = 3 * RWKV_DIM + 2 * DECAY_LORA + 2 * ICLR_LORA + GATE_LORA
RWKV_PAD = 3712
CHUNK = 128
SG_GROUPS = 16
D_FF = 7168
N_EXPERTS = 8
TOP_K = 2

VMEM_LIMIT_BYTES = 56 * 1024 * 1024
LANES = 128

MOE_TM = 512


def _cparams(n_axes):
    return pltpu.CompilerParams(
        dimension_semantics=("arbitrary",) * n_axes,
        vmem_limit_bytes=VMEM_LIMIT_BYTES)


def _mm_body(*refs, n_w, prologue, epilogue, grouped):
    refs = list(refs)
    if grouped:
        te_ref, tv_ref, _ = refs[:3]
        refs = refs[3:]
    x_ref = refs.pop(0)
    w_refs = [refs.pop(0) for _ in range(n_w)]
    gain_ref = refs.pop(0) if prologue == "rms" else None
    bias_ref = refs.pop(0) if epilogue == "bias" else None
    if epilogue == "resid":
        resid_ref = refs.pop(0)
        gate_ref = refs.pop(0)
    o_ref, wbf_ref = refs

    i = pl.program_id(1)
    if grouped:
        new_w = jnp.logical_or(i == 0, te_ref[i] != te_ref[jnp.maximum(i - 1, 0)])
    else:
        new_w = i == 0

    @pl.when(new_w)
    def _():
        for n in range(n_w):
            wbf_ref[n] = w_refs[n][...].astype(BF16)

    def compute():
        x = x_ref[...]
        if prologue == "rms":
            xf = x.astype(F32)
            ms = jnp.mean(xf * xf, axis=-1, keepdims=True)
            x = xf * lax.rsqrt(ms + NORM_EPS) * gain_ref[...]
        elif prologue == "silu":
            xf = x.astype(F32)
            x = xf * jax.nn.sigmoid(xf)
        x = x.astype(BF16)
        acc = [jnp.dot(x, wbf_ref[n], preferred_element_type=F32) for n in range(n_w)]
        if epilogue == "swiglu":
            a = acc[0]
            out = a * jax.nn.sigmoid(a) * acc[1]
        elif epilogue == "gelu":
            a = acc[0]
            out = 0.5 * a * (1.0 + lax.erf(a * (2.0 ** -0.5)))
        elif epilogue == "bias":
            out = acc[0] + bias_ref[...]
        elif epilogue == "resid":
            out = resid_ref[...] + gate_ref[0] * acc[0]
        else:
            out = acc[0]
        o_ref[...] = out.astype(o_ref.dtype)

    if grouped:
        @pl.when(tv_ref[i] == 1)
        def _():
            compute()

        @pl.when(tv_ref[i] == 0)
        def _():
            o_ref[...] = jnp.zeros(o_ref.shape, o_ref.dtype)
    else:
        compute()


def _matmul(x, ws, *, tm, tn, out_dtype, k=None, x_col_block=0, x_row_map=None, m_out=None,
            prologue=None, gain=None, epilogue=None, bias=None, resid=None, gate=None,
            rows_per_gate=None, group=None):
    grouped = group is not None
    kdim = k if k is not None else x.shape[1]
    n = ws[0].shape[-1]
    m = m_out if m_out is not None else x.shape[0]
    assert m % tm == 0 and n % tn == 0, (m, tm, n, tn)
    n_w = len(ws)
    grid = (n // tn, m // tm)

    if grouped:
        def xmap(j, i, te, tv, ts):
            return (ts[i], x_col_block)

        def wmap(j, i, te, tv, ts):
            return (te[i], 0, j)

        def omap(j, i, te, tv, ts):
            return (i, j)
        w_spec = pl.BlockSpec((None, kdim, tn), wmap)
    else:
        def xmap(j, i):
            return ((x_row_map(i) if x_row_map is not None else i), x_col_block)

        def wmap(j, i):
            return (0, j)

        def omap(j, i):
            return (i, j)
        w_spec = pl.BlockSpec((kdim, tn), wmap)

    in_specs = [pl.BlockSpec((tm, kdim), xmap)] + [w_spec] * n_w
    args = [x] + list(ws)
    if prologue == "rms":
        in_specs.append(pl.BlockSpec((1, kdim), lambda j, i, *_: (0, 0)))
        args.append(gain.reshape(1, kdim))
    if epilogue == "bias":
        in_specs.append(pl.BlockSpec((1, tn), lambda j, i, *_: (0, j)))
        args.append(bias.reshape(1, n))
    if epilogue == "resid":
        in_specs.append(pl.BlockSpec((tm, tn), omap))
        args.append(resid)
        assert rows_per_gate % tm == 0
        tiles_per_gate = rows_per_gate // tm
        in_specs.append(pl.BlockSpec((1, 1, tn), lambda j, i, *_: (i // tiles_per_gate, 0, j)))
        args.append(gate)

    body = functools.partial(_mm_body, n_w=n_w, prologue=prologue, epilogue=epilogue,
                             grouped=grouped)
    gs = pltpu.PrefetchScalarGridSpec(
        num_scalar_prefetch=3 if grouped else 0,
        grid=grid,
        in_specs=in_specs,
        out_specs=pl.BlockSpec((tm, tn), omap),
        scratch_shapes=[pltpu.VMEM((n_w, kdim, tn), BF16)])
    call = pl.pallas_call(
        body, grid_spec=gs,
        out_shape=jax.ShapeDtypeStruct((m, n), out_dtype),
        compiler_params=_cparams(2))
    if grouped:
        return call(*group, *args)
    return call(*args)


def _norm_mod_math(v, g, sh, sc):
    ms = jnp.mean(v * v, axis=-1, keepdims=True)
    y = v * lax.rsqrt(ms + NORM_EPS) * g
    return y * (1.0 + sc) + sh


def _nm_merge_body(x_ref, c_ref, g_ref, sh_ref, sc_ref, csh_ref, csc_ref, o_ref):
    r = pl.program_id(1)

    @pl.when(r == 0)
    def _():
        o_ref[0] = _norm_mod_math(c_ref[0], g_ref[...], csh_ref[...], csc_ref[...]).astype(BF16)

    @pl.when(r > 0)
    def _():
        o_ref[0] = _norm_mod_math(x_ref[0], g_ref[...], sh_ref[0], sc_ref[0]).astype(BF16)


def _norm_mod_merge(x, ctx, gain, sh, sc, csh, csc):
    b, l, d = x.shape
    lc = ctx.shape[1]
    tm = lc
    assert l % tm == 0
    nt = l // tm + 1
    return pl.pallas_call(
        _nm_merge_body,
        grid=(b, nt),
        in_specs=[
            pl.BlockSpec((1, tm, d), lambda bi, r: (bi, jnp.maximum(r - 1, 0), 0)),
            pl.BlockSpec((1, lc, d), lambda bi, r: (bi, 0, 0)),
            pl.BlockSpec((1, d), lambda bi, r: (0, 0)),
            pl.BlockSpec((1, 1, d), lambda bi, r: (bi, 0, 0)),
            pl.BlockSpec((1, 1, d), lambda bi, r: (bi, 0, 0)),
            pl.BlockSpec((1, d), lambda bi, r: (0, 0)),
            pl.BlockSpec((1, d), lambda bi, r: (0, 0)),
        ],
        out_specs=pl.BlockSpec((1, tm, d), lambda bi, r: (bi, r, 0)),
        out_shape=jax.ShapeDtypeStruct((b, lc + l, d), BF16),
        compiler_params=_cparams(2),
    )(x, ctx, gain.reshape(1, d), sh, sc, csh.reshape(1, d), csc.reshape(1, d))


def _nm_body(x_ref, g_ref, sh_ref, sc_ref, o_ref):
    o_ref[0] = _norm_mod_math(x_ref[0], g_ref[...], sh_ref[0], sc_ref[0]).astype(BF16)


def _nm_router_body(x_ref, g_ref, sh_ref, sc_ref, r_ref, o_ref, lg_ref):
    y = _norm_mod_math(x_ref[0], g_ref[...], sh_ref[0], sc_ref[0])
    o_ref[0] = y.astype(BF16)
    lg_ref[0] = jnp.dot(y, r_ref[...], precision=lax.Precision.HIGHEST,
                        preferred_element_type=F32)


def _norm_mod(h, gain, sh, sc, router=None, tm=256):
    b, l, d = h.shape
    assert l % tm == 0
    in_specs = [
        pl.BlockSpec((1, tm, d), lambda bi, r: (bi, r, 0)),
        pl.BlockSpec((1, d), lambda bi, r: (0, 0)),
        pl.BlockSpec((1, 1, d), lambda bi, r: (bi, 0, 0)),
        pl.BlockSpec((1, 1, d), lambda bi, r: (bi, 0, 0)),
    ]
    o_spec = pl.BlockSpec((1, tm, d), lambda bi, r: (bi, r, 0))
    o_shape = jax.ShapeDtypeStruct((b, l, d), BF16)
    if router is None:
        return pl.pallas_call(
            _nm_body, grid=(b, l // tm), in_specs=in_specs, out_specs=o_spec,
            out_shape=o_shape, compiler_params=_cparams(2),
        )(h, gain.reshape(1, d), sh, sc)
    ne = router.shape[1]
    router_pad = jnp.pad(router, ((0, 0), (0, LANES - ne)))
    return pl.pallas_call(
        _nm_router_body, grid=(b, l // tm),
        in_specs=in_specs + [pl.BlockSpec((d, LANES), lambda bi, r: (0, 0))],
        out_specs=[o_spec, pl.BlockSpec((1, tm, LANES), lambda bi, r: (bi, r, 0))],
        out_shape=[o_shape, jax.ShapeDtypeStruct((b, l, LANES), F32)],
        compiler_params=_cparams(2),
    )(h, gain.reshape(1, d), sh, sc, router_pad)


def _final_norm_body(x_ref, g_ref, o_ref):
    v = x_ref[...]
    ms = jnp.mean(v * v, axis=-1, keepdims=True)
    o_ref[...] = v * lax.rsqrt(ms + NORM_EPS) * g_ref[...]


def _final_norm(h, gain, tm=256):
    m, d = h.shape
    return pl.pallas_call(
        _final_norm_body, grid=(m // tm,),
        in_specs=[pl.BlockSpec((tm, d), lambda i: (i, 0)),
                  pl.BlockSpec((1, d), lambda i: (0, 0))],
        out_specs=pl.BlockSpec((tm, d), lambda i: (i, 0)),
        out_shape=jax.ShapeDtypeStruct((m, d), F32),
        compiler_params=_cparams(1),
    )(h, gain.reshape(1, d))


def _krope_body(z_ref, cos_ref, sin_ref, o_ref):
    z = z_ref[...]
    rot = z[:, :QK_ROPE] * cos_ref[...] + z[:, QK_ROPE:] * sin_ref[...]
    o_ref[...] = jnp.concatenate([rot, jnp.zeros_like(rot)], axis=1).astype(BF16)


def _krope(z_all, cos_t, sin_t, col_block, tm=256):
    m = z_all.shape[0]
    t = cos_t.shape[0]
    tiles_per_batch = t // tm
    return pl.pallas_call(
        _krope_body, grid=(m // tm,),
        in_specs=[pl.BlockSpec((tm, LANES), lambda i: (i, col_block)),
                  pl.BlockSpec((tm, QK_ROPE), lambda i: (i % tiles_per_batch, 0)),
                  pl.BlockSpec((tm, QK_ROPE), lambda i: (i % tiles_per_batch, 0))],
        out_specs=pl.BlockSpec((tm, LANES), lambda i: (i, 0)),
        out_shape=jax.ShapeDtypeStruct((m, LANES), BF16),
        compiler_params=_cparams(1),
    )(z_all, cos_t, sin_t)


def _attn_body(q_ref, kn_ref, v_ref, kr_ref, cos_ref, sin_ref, o_ref, kfull_ref, *, scale):
    qi = pl.program_id(2)

    @pl.when(qi == 0)
    def _():
        kfull_ref[:, :QK_NOPE] = kn_ref[0]
        kfull_ref[:, QK_NOPE:] = kr_ref[0]

    q = q_ref[...].astype(F32)
    qn = q[:, :QK_NOPE]
    qr = (q[:, QK_NOPE:QK_NOPE + QK_ROPE] * cos_ref[...]
          + q[:, QK_NOPE + QK_ROPE:] * sin_ref[...])
    qf = (jnp.concatenate([qn, qr, jnp.zeros_like(qr)], axis=1) * scale).astype(BF16)
    s = lax.dot_general(qf, kfull_ref[...], (((1,), (1,)), ((), ())),
                        preferred_element_type=F32)
    m = jnp.max(s, axis=-1, keepdims=True)
    p = jnp.exp(s - m)
    l = jnp.sum(p, axis=-1, keepdims=True)
    o = jnp.dot(p.astype(BF16), v_ref[0], preferred_element_type=F32)
    o_ref[...] = (o / l).astype(BF16)


def _attention(q, kv, kr, cos_q, sin_q, batch, tq=256):
    m = q.shape[0]
    l = m // batch
    t = kv.shape[1]
    nq = l // tq
    scale = (QK_NOPE + QK_ROPE) ** -0.5
    hw = QK_NOPE + 2 * QK_ROPE
    return pl.pallas_call(
        functools.partial(_attn_body, scale=scale),
        grid=(batch, MLA_HEADS, nq),
        in_specs=[
            pl.BlockSpec((tq, hw), lambda b, h, i: (b * nq + i, h)),
            pl.BlockSpec((1, t, QK_NOPE), lambda b, h, i: (b, 0, 2 * h)),
            pl.BlockSpec((1, t, V_HEAD), lambda b, h, i: (b, 0, 2 * h + 1)),
            pl.BlockSpec((1, t, LANES), lambda b, h, i: (b, 0, 0)),
            pl.BlockSpec((tq, QK_ROPE), lambda b, h, i: (i, 0)),
            pl.BlockSpec((tq, QK_ROPE), lambda b, h, i: (i, 0)),
        ],
        out_specs=pl.BlockSpec((tq, V_HEAD), lambda b, h, i: (b * nq + i, h)),
        out_shape=jax.ShapeDtypeStruct((m, MLA_HEADS * V_HEAD), BF16),
        scratch_shapes=[pltpu.VMEM((t, QK_NOPE + LANES), BF16)],
        compiler_params=_cparams(3),
    )(q, kv, kv, kr, cos_q, sin_q)


def _sgu_body(u_ref, v_ref, lw_ref, lb_ref, ws_ref, bs_ref, o_ref, vn_ref):
    v = v_ref[...].astype(F32)
    mu = jnp.mean(v, axis=-1, keepdims=True)
    vc = v - mu
    var = jnp.mean(vc * vc, axis=-1, keepdims=True)
    vn_ref[...] = (vc * lax.rsqrt(var + 1e-5) * lw_ref[...] + lb_ref[...]).astype(BF16)
    n_chunks = v_ref.shape[0] // CHUNK
    for n in range(n_chunks):
        rows = slice(n * CHUNK, (n + 1) * CHUNK)
        for g in range(SG_GROUPS):
            cols = slice(g * LANES, (g + 1) * LANES)
            vm = jnp.dot(ws_ref[g], vn_ref[rows, cols], preferred_element_type=F32)
            vm = vm + bs_ref[:, cols]
            o_ref[rows, cols] = (u_ref[rows, cols].astype(F32) * vm).astype(BF16)


def _spatial_gate(hg, ln_w, ln_b, w_s, b_s, tm=256):
    m = hg.shape[0]
    d = hg.shape[1] // 2
    bs_full = jnp.repeat(b_s.T, d // SG_GROUPS, axis=1)
    return pl.pallas_call(
        _sgu_body, grid=(m // tm,),
        in_specs=[
            pl.BlockSpec((tm, d), lambda i: (i, 0)),
            pl.BlockSpec((tm, d), lambda i: (i, 1)),
            pl.BlockSpec((1, d), lambda i: (0, 0)),
            pl.BlockSpec((1, d), lambda i: (0, 0)),
            pl.BlockSpec((SG_GROUPS, CHUNK, CHUNK), lambda i: (0, 0, 0)),
            pl.BlockSpec((CHUNK, d), lambda i: (0, 0)),
        ],
        out_specs=pl.BlockSpec((tm, d), lambda i: (i, 0)),
        out_shape=jax.ShapeDtypeStruct((m, d), BF16),
        scratch_shapes=[pltpu.VMEM((tm, d), BF16)],
        compiler_params=_cparams(1),
    )(hg, hg, ln_w.reshape(1, d), ln_b.reshape(1, d), w_s.astype(BF16), bs_full)


def _centred_shift(z, mu_prev, mu_next):
    z_prev = jnp.pad(z, ((0, 0), (1, 0), (0, 0)))[:, :-1]
    z_next = jnp.pad(z, ((0, 0), (0, 1), (0, 0)))[:, 1:]
    return z + mu_prev * (z_prev - z) + mu_next * (z_next - z)


def _rwkv_prep(z, p, with_output):
    b, l, _ = z.shape
    z = _centred_shift(z, p['mu_prev'], p['mu_next']).astype(F32)
    splits = (RWKV_DIM, 2 * RWKV_DIM, 3 * RWKV_DIM, 3 * RWKV_DIM + 2 * DECAY_LORA,
              3 * RWKV_DIM + 2 * DECAY_LORA + 2 * ICLR_LORA)
    r, k, v, wd, ad, gd = jnp.split(z, splits, axis=-1)

    def heads(t):
        return t.reshape(b, l, RWKV_HEADS, RWKV_HEAD)

    kk = heads(k * p['k_k'])
    kk = kk / jnp.maximum(jnp.linalg.norm(kk, axis=-1, keepdims=True), 1e-12)
    dirs = []
    for d in range(2):
        w_log = -jax.nn.softplus(-(p['w0'][d] + jnp.tanh(wd[..., d * DECAY_LORA:(d + 1) * DECAY_LORA]) @ p['w_up'][d])) - 0.5
        decay = jnp.exp(-jnp.exp(w_log))
        iclr = jax.nn.sigmoid(p['a0'][d] + ad[..., d * ICLR_LORA:(d + 1) * ICLR_LORA] @ p['a_up'][d])
        k_d = k * (1 + (iclr - 1) * p['k_a'])
        dirs.append((heads(decay), heads(k_d), heads(iclr) * kk))
    extra = None
    if with_output:
        extra = (heads(r), jax.nn.sigmoid(gd) @ p['g_up'])
    return heads(v), -kk, dirs, extra


def _wkv_scan(state, decay, k, b, v, a, r, reverse):
    with_out = r is not None
    seqs = (decay, k, b, v, a) + ((r,) if with_out else ())
    xs = tuple(jnp.moveaxis(t, 1, 0) for t in seqs)

    def step(s, inp):
        w_t, k_t, b_t, v_t, a_t = inp[:5]
        sa = jnp.einsum('bhvk,bhk->bhv', s, a_t)
        s = s * w_t[:, :, None, :] + sa[..., None] * b_t[:, :, None, :] + v_t[..., None] * k_t[:, :, None, :]
        return s, (jnp.einsum('bhvk,bhk->bhv', s, inp[5]) if with_out else None)

    s, ys = lax.scan(step, state, xs, reverse=reverse)
    return s, (jnp.moveaxis(ys, 0, 1) if with_out else None)


def _rwkv_readout(y, r, k_sum, v, g, p):
    b, l = y.shape[:2]
    mu = jnp.mean(y, axis=-1, keepdims=True)
    var = jnp.mean(jnp.square(y - mu), axis=-1, keepdims=True)
    yn = ((y - mu) * lax.rsqrt(var + LNX_EPS)).reshape(b, l, RWKV_DIM) * p['lnx_w'] + p['lnx_b']
    bonus = jnp.sum(r * k_sum * p['r_k'], axis=-1, keepdims=True) * v
    return (yn + bonus.reshape(b, l, RWKV_DIM)) * g


def _rwkv_mixer(zr, lc, p):
    b = zr.shape[0]
    zr_c, zr_l = zr[:, :lc], zr[:, lc:]
    vr_l, a_l, dirs_l, (r_l, g_l) = _rwkv_prep(zr_l, p, True)
    vr_c, a_c, dirs_c, _ = _rwkv_prep(zr_c, p, False)
    s0 = jnp.zeros((b, RWKV_HEADS, RWKV_HEAD, RWKV_HEAD), F32)
    y_l = 0.0
    for d, rev in enumerate((False, True)):
        s_ctx, _ = _wkv_scan(s0, *dirs_c[d], vr_c, a_c, None, rev)
        _, yl_d = _wkv_scan(s_ctx, *dirs_l[d], vr_l, a_l, r_l, rev)
        y_l = y_l + yl_d
    rw_l = _rwkv_readout(y_l, r_l, dirs_l[0][1] + dirs_l[1][1], vr_l, g_l, p)
    return rw_l.reshape(-1, RWKV_DIM).astype(BF16)


def _rope_swap_cols(w):
    f = ROPE_FREQS
    parts = []
    for a in range(2):
        x1 = w[..., (2 * a) * f:(2 * a + 1) * f]
        x2 = w[..., (2 * a + 1) * f:(2 * a + 2) * f]
        parts += [-x2, x1]
    return jnp.concatenate(parts, axis=-1)


def _rope_tables(l):
    rows = l // GRID_W
    row = jnp.repeat(jnp.arange(rows), GRID_W).astype(F32)
    col = jnp.tile(jnp.arange(GRID_W), rows).astype(F32)
    inv = ROPE_THETA ** (-jnp.arange(ROPE_FREQS, dtype=F32) / ROPE_FREQS)
    ar = row[:, None] * inv
    ac = col[:, None] * inv
    cos = jnp.concatenate([jnp.cos(ar), jnp.cos(ar), jnp.cos(ac), jnp.cos(ac)], axis=1)
    sin = jnp.concatenate([jnp.sin(ar), jnp.sin(ar), jnp.sin(ac), jnp.sin(ac)], axis=1)
    return cos, sin


def _moe(xn, logits, w1, w3, w2):
    n_tok, d = xn.shape
    tm = MOE_TM
    top_val, top_idx = lax.top_k(logits, TOP_K)
    gate = jax.nn.softmax(top_val, axis=-1)
    n_assign = n_tok * TOP_K
    e_flat = top_idx.reshape(-1).astype(jnp.int32)
    tok_flat = jnp.repeat(jnp.arange(n_tok, dtype=jnp.int32), TOP_K)
    order = jnp.argsort(e_flat)
    e_sorted = e_flat[order]
    counts = jnp.bincount(e_flat, length=N_EXPERTS).astype(jnp.int32)
    padded = (counts + tm - 1) // tm * tm
    pad_end = jnp.cumsum(padded)
    pad_start = pad_end - padded
    grp_start = jnp.cumsum(counts) - counts
    dest = pad_start[e_sorted] + jnp.arange(n_assign, dtype=jnp.int32) - grp_start[e_sorted]
    n_tiles = n_assign // tm + N_EXPERTS
    n_rows = n_tiles * tm
    row_tok = jnp.zeros((n_rows,), jnp.int32).at[dest].set(tok_flat[order])
    pos = jnp.zeros((n_assign,), jnp.int32).at[order].set(dest).reshape(n_tok, TOP_K)
    tile_start = jnp.arange(n_tiles, dtype=jnp.int32) * tm
    n_valid = pad_end[-1] // tm
    tile_src = jnp.minimum(jnp.arange(n_tiles, dtype=jnp.int32), n_valid - 1).astype(jnp.int32)
    tile_exp_all = jnp.minimum(jnp.searchsorted(pad_end, tile_start, side='right'),
                               N_EXPERTS - 1).astype(jnp.int32)
    tile_exp = tile_exp_all[tile_src]
    tile_valid = (tile_start < pad_end[-1]).astype(jnp.int32)
    group = (tile_exp, tile_valid, tile_src)

    xs = jnp.take(xn, row_tok, axis=0)
    hmid = _matmul(xs, [w1, w3], tm=tm, tn=512, out_dtype=BF16, epilogue="swiglu", group=group)
    ys = _matmul(hmid, [w2], tm=tm, tn=256, out_dtype=F32, group=group)
    y = (jnp.take(ys, pos[:, 0], axis=0) * gate[:, 0:1]
         + jnp.take(ys, pos[:, 1], axis=0) * gate[:, 1:2])
    return y


def kernel(x, c, ctx, c_ctx, l0_ada_w, l0_ada_b, l0_norm1, l0_norm2, l0_w_in, l0_q_norm, l0_w_uq, l0_kv_norm, l0_w_ukv, l0_mu_prev, l0_mu_next, l0_w0, l0_w_up, l0_a0, l0_a_up, l0_g_up, l0_k_k, l0_k_a, l0_r_k, l0_lnx_w, l0_lnx_b, l0_w_o, l0_ffn_w1, l0_ffn_w3, l0_ffn_w2, l1_ada_w, l1_ada_b, l1_norm1, l1_norm2, l1_w_in, l1_v_ln_w, l1_v_ln_b, l1_w_s, l1_b_s, l1_w_o, l1_router, l1_moe_w1, l1_moe_w3, l1_moe_w2, final_norm):
    b, l, d = x.shape
    lc = ctx.shape[1]
    t = lc + l
    n_tok = b * l

    cond = jnp.zeros((8, d), F32).at[:b].set(c).at[b].set(c_ctx)
    mod0 = _matmul(cond, [l0_ada_w], tm=8, tn=1536, out_dtype=F32, prologue="silu",
                   epilogue="bias", bias=l0_ada_b)
    mod1 = _matmul(cond, [l1_ada_w], tm=8, tn=1536, out_dtype=F32, prologue="silu",
                   epilogue="bias", bias=l1_ada_b)

    def mods(mod, row0, nrows):
        return [mod[row0:row0 + nrows, i * d:(i + 1) * d].reshape(nrows, 1, d) for i in range(N_MOD)]

    sh1, sc1, g1, sh2, sc2, g2 = mods(mod0, 0, b)
    csh1, csc1 = mods(mod0, b, 1)[:2]

    xall = _norm_mod_merge(x, ctx, l0_norm1, sh1, sc1, csh1, csc1)
    mla_in = Q_LORA + KV_LORA + QK_ROPE
    w_kr = l0_w_in[:, Q_LORA + KV_LORA:mla_in]
    w_in_ext = jnp.concatenate(
        [l0_w_in[:, :mla_in], _rope_swap_cols(w_kr), l0_w_in[:, mla_in:],
         jnp.zeros((d, RWKV_PAD - RWKV_IN), F32)], axis=1)
    z_all = _matmul(xall.reshape(b * t, d), [w_in_ext], tm=512, tn=256, out_dtype=F32)

    wq = l0_w_uq.reshape(Q_LORA, MLA_HEADS, QK_NOPE + QK_ROPE)
    wq_ext = jnp.concatenate([wq, _rope_swap_cols(wq[..., QK_NOPE:])], axis=-1)
    wq_ext = wq_ext.reshape(Q_LORA, MLA_HEADS * (QK_NOPE + 2 * QK_ROPE))
    tq_rows = 256
    per_b = l // tq_rows
    lat0 = lc // tq_rows

    def lat_rows(i):
        return (i // per_b) * (t // tq_rows) + lat0 + i % per_b

    q = _matmul(z_all, [wq_ext], tm=tq_rows, tn=512, out_dtype=BF16, k=Q_LORA, x_col_block=0,
                x_row_map=lat_rows, m_out=n_tok, prologue="rms", gain=l0_q_norm)
    kv = _matmul(z_all, [l0_w_ukv], tm=512, tn=512, out_dtype=BF16, k=KV_LORA, x_col_block=1,
                 prologue="rms", gain=l0_kv_norm)
    cos_l, sin_l = _rope_tables(l)
    cos_t = jnp.concatenate([jnp.ones((lc, QK_ROPE), F32), cos_l], axis=0)
    sin_t = jnp.concatenate([jnp.zeros((lc, QK_ROPE), F32), sin_l], axis=0)
    kr = _krope(z_all, cos_t, sin_t, col_block=(Q_LORA + KV_LORA) // LANES)
    att = _attention(q, kv.reshape(b, t, -1), kr.reshape(b, t, LANES), cos_l, sin_l, b)

    p0 = dict(mu_prev=l0_mu_prev, mu_next=l0_mu_next, w0=l0_w0, w_up=l0_w_up, a0=l0_a0,
              a_up=l0_a_up, g_up=l0_g_up, k_k=l0_k_k, k_a=l0_k_a, r_k=l0_r_k,
              lnx_w=l0_lnx_w, lnx_b=l0_lnx_b)
    rw_col0 = mla_in + QK_ROPE
    zr = z_all[:, rw_col0:rw_col0 + RWKV_IN].reshape(b, t, RWKV_IN)
    rw = _rwkv_mixer(zr, lc, p0)

    mix = jnp.concatenate([att, rw], axis=1)
    h = _matmul(mix, [l0_w_o], tm=512, tn=512, out_dtype=F32, epilogue="resid",
                resid=x.reshape(n_tok, d), gate=g1, rows_per_gate=l)

    xn = _norm_mod(h.reshape(b, l, d), l0_norm2, sh2, sc2).reshape(n_tok, d)
    hmid = _matmul(xn, [l0_ffn_w1, l0_ffn_w3], tm=512, tn=512, out_dtype=BF16, epilogue="swiglu")
    h = _matmul(hmid, [l0_ffn_w2], tm=512, tn=256, out_dtype=F32, epilogue="resid",
                resid=h, gate=g2, rows_per_gate=l)

    sh1, sc1, g1, sh2, sc2, g2 = mods(mod1, 0, b)
    xn = _norm_mod(h.reshape(b, l, d), l1_norm1, sh1, sc1).reshape(n_tok, d)
    hg = _matmul(xn, [l1_w_in], tm=512, tn=512, out_dtype=BF16, epilogue="gelu")
    gated = _spatial_gate(hg, l1_v_ln_w, l1_v_ln_b, l1_w_s, l1_b_s)
    h = _matmul(gated, [l1_w_o], tm=512, tn=512, out_dtype=F32, epilogue="resid",
                resid=h, gate=g1, rows_per_gate=l)

    xn, logits = _norm_mod(h.reshape(b, l, d), l1_norm2, sh2, sc2, router=l1_router)
    y = _moe(xn.reshape(n_tok, d), logits.reshape(n_tok, LANES)[:, :N_EXPERTS],
             l1_moe_w1, l1_moe_w3, l1_moe_w2)
    h = h + (g2 * y.reshape(b, l, d)).reshape(n_tok, d)

    return _final_norm(h, final_norm).reshape(b, l, d)
```

```python
import functools

import jax
import jax.numpy as jnp
from jax import lax
from jax.experimental import pallas as pl
from jax.experimental.pallas import tpu as pltpu

F32 = jnp.float32
BF16 = jnp.bfloat16

D_MODEL = 2048
GRID_W = 64
N_MOD = 6
NORM_EPS = 1e-6
V_HEAD = 128
MLA_HEADS = 8
Q_LORA = 512
KV_LORA = 512
QK_NOPE = 128
QK_ROPE = 64
ROPE_FREQS = 16
ROPE_THETA = 10000.0
RWKV_HEAD = 64
RWKV_HEADS = 16
RWKV_DIM = 1024
DECAY_LORA = 64
ICLR_LORA = 64
GATE_LORA = 160
LNX_EPS = 64e-5
RWKV_IN = 3 * RWKV_DIM + 2 * DECAY_LORA + 2 * ICLR_LORA + GATE_LORA
RWKV_PAD = 3584
RW_CHUNK = 64
PREP_TM = 128
N_PAIRS = RWKV_HEADS // 2
HI = lax.Precision.HIGHEST
CHUNK = 128
SG_GROUPS = 16
D_FF = 7168
N_EXPERTS = 8
TOP_K = 2

VMEM_LIMIT_BYTES = 56 * 1024 * 1024
LANES = 128

MOE_TM = 512


def _cparams(n_axes):
    return pltpu.CompilerParams(
        dimension_semantics=("arbitrary",) * n_axes,
        vmem_limit_bytes=VMEM_LIMIT_BYTES)


def _mm_body(*refs, n_w, prologue, epilogue, grouped):
    refs = list(refs)
    if grouped:
        te_ref, tv_ref, _ = refs[:3]
        refs = refs[3:]
    x_ref = refs.pop(0)
    w_refs = [refs.pop(0) for _ in range(n_w)]
    gain_ref = refs.pop(0) if prologue == "rms" else None
    bias_ref = refs.pop(0) if epilogue == "bias" else None
    if epilogue == "resid":
        resid_ref = refs.pop(0)
        gate_ref = refs.pop(0)
    o_ref, wbf_ref = refs

    i = pl.program_id(1)
    if grouped:
        new_w = jnp.logical_or(i == 0, te_ref[i] != te_ref[jnp.maximum(i - 1, 0)])
    else:
        new_w = i == 0

    @pl.when(new_w)
    def _():
        for n in range(n_w):
            wbf_ref[n] = w_refs[n][...].astype(BF16)

    def compute():
        x = x_ref[...]
        if prologue == "rms":
            xf = x.astype(F32)
            ms = jnp.mean(xf * xf, axis=-1, keepdims=True)
            x = xf * lax.rsqrt(ms + NORM_EPS) * gain_ref[...]
        elif prologue == "silu":
            xf = x.astype(F32)
            x = xf * jax.nn.sigmoid(xf)
        x = x.astype(BF16)
        acc = [jnp.dot(x, wbf_ref[n], preferred_element_type=F32) for n in range(n_w)]
        if epilogue == "swiglu":
            a = acc[0]
            out = a * jax.nn.sigmoid(a) * acc[1]
        elif epilogue == "gelu":
            a = acc[0]
            out = 0.5 * a * (1.0 + lax.erf(a * (2.0 ** -0.5)))
        elif epilogue == "bias":
            out = acc[0] + bias_ref[...]
        elif epilogue == "resid":
            out = resid_ref[...] + gate_ref[0] * acc[0]
        else:
            out = acc[0]
        o_ref[...] = out.astype(o_ref.dtype)

    if grouped:
        @pl.when(tv_ref[i] == 1)
        def _():
            compute()

        @pl.when(tv_ref[i] == 0)
        def _():
            o_ref[...] = jnp.zeros(o_ref.shape, o_ref.dtype)
    else:
        compute()


def _matmul(x, ws, *, name, tm, tn, out_dtype, k=None, x_col_block=0, x_row_map=None, m_out=None,
            prologue=None, gain=None, epilogue=None, bias=None, resid=None, gate=None,
            rows_per_gate=None, group=None):
    grouped = group is not None
    kdim = k if k is not None else x.shape[1]
    n = ws[0].shape[-1]
    m = m_out if m_out is not None else x.shape[0]
    assert m % tm == 0 and n % tn == 0, (m, tm, n, tn)
    n_w = len(ws)
    grid = (n // tn, m // tm)

    if grouped:
        def xmap(j, i, te, tv, ts):
            return (ts[i], x_col_block)

        def wmap(j, i, te, tv, ts):
            return (te[i], 0, j)

        def omap(j, i, te, tv, ts):
            return (i, j)
        w_spec = pl.BlockSpec((None, kdim, tn), wmap)
    else:
        def xmap(j, i):
            return ((x_row_map(i) if x_row_map is not None else i), x_col_block)

        def wmap(j, i):
            return (0, j)

        def omap(j, i):
            return (i, j)
        w_spec = pl.BlockSpec((kdim, tn), wmap)

    in_specs = [pl.BlockSpec((tm, kdim), xmap)] + [w_spec] * n_w
    args = [x] + list(ws)
    if prologue == "rms":
        in_specs.append(pl.BlockSpec((1, kdim), lambda j, i, *_: (0, 0)))
        args.append(gain.reshape(1, kdim))
    if epilogue == "bias":
        in_specs.append(pl.BlockSpec((1, tn), lambda j, i, *_: (0, j)))
        args.append(bias.reshape(1, n))
    if epilogue == "resid":
        in_specs.append(pl.BlockSpec((tm, tn), omap))
        args.append(resid)
        assert rows_per_gate % tm == 0
        tiles_per_gate = rows_per_gate // tm
        in_specs.append(pl.BlockSpec((1, 1, tn), lambda j, i, *_: (i // tiles_per_gate, 0, j)))
        args.append(gate)

    body = functools.partial(_mm_body, n_w=n_w, prologue=prologue, epilogue=epilogue,
                             grouped=grouped)
    gs = pltpu.PrefetchScalarGridSpec(
        num_scalar_prefetch=3 if grouped else 0,
        grid=grid,
        in_specs=in_specs,
        out_specs=pl.BlockSpec((tm, tn), omap),
        scratch_shapes=[pltpu.VMEM((n_w, kdim, tn), BF16)])
    call = pl.pallas_call(
        body, grid_spec=gs, name=name,
        out_shape=jax.ShapeDtypeStruct((m, n), out_dtype),
        compiler_params=_cparams(2))
    if grouped:
        return call(*group, *args)
    return call(*args)


def _norm_mod_math(v, g, sh, sc):
    ms = jnp.mean(v * v, axis=-1, keepdims=True)
    y = v * lax.rsqrt(ms + NORM_EPS) * g
    return y * (1.0 + sc) + sh


def _nm_merge_body(x_ref, c_ref, g_ref, sh_ref, sc_ref, csh_ref, csc_ref, o_ref):
    r = pl.program_id(1)

    @pl.when(r == 0)
    def _():
        o_ref[0] = _norm_mod_math(c_ref[0], g_ref[...], csh_ref[...], csc_ref[...]).astype(BF16)

    @pl.when(r > 0)
    def _():
        o_ref[0] = _norm_mod_math(x_ref[0], g_ref[...], sh_ref[0], sc_ref[0]).astype(BF16)


def _norm_mod_merge(x, ctx, gain, sh, sc, csh, csc):
    b, l, d = x.shape
    lc = ctx.shape[1]
    tm = lc
    assert l % tm == 0
    nt = l // tm + 1
    return pl.pallas_call(
        _nm_merge_body, name="norm_mod_merge",
        grid=(b, nt),
        in_specs=[
            pl.BlockSpec((1, tm, d), lambda bi, r: (bi, jnp.maximum(r - 1, 0), 0)),
            pl.BlockSpec((1, lc, d), lambda bi, r: (bi, 0, 0)),
            pl.BlockSpec((1, d), lambda bi, r: (0, 0)),
            pl.BlockSpec((1, 1, d), lambda bi, r: (bi, 0, 0)),
            pl.BlockSpec((1, 1, d), lambda bi, r: (bi, 0, 0)),
            pl.BlockSpec((1, d), lambda bi, r: (0, 0)),
            pl.BlockSpec((1, d), lambda bi, r: (0, 0)),
        ],
        out_specs=pl.BlockSpec((1, tm, d), lambda bi, r: (bi, r, 0)),
        out_shape=jax.ShapeDtypeStruct((b, lc + l, d), BF16),
        compiler_params=_cparams(2),
    )(x, ctx, gain.reshape(1, d), sh, sc, csh.reshape(1, d), csc.reshape(1, d))


def _nm_body(x_ref, g_ref, sh_ref, sc_ref, o_ref):
    o_ref[0] = _norm_mod_math(x_ref[0], g_ref[...], sh_ref[0], sc_ref[0]).astype(BF16)


def _nm_router_body(x_ref, g_ref, sh_ref, sc_ref, r_ref, o_ref, lg_ref):
    y = _norm_mod_math(x_ref[0], g_ref[...], sh_ref[0], sc_ref[0])
    o_ref[0] = y.astype(BF16)
    lg_ref[0] = jnp.dot(y, r_ref[...], precision=lax.Precision.HIGHEST,
                        preferred_element_type=F32)


def _norm_mod(h, gain, sh, sc, router=None, tm=256):
    b, l, d = h.shape
    assert l % tm == 0
    in_specs = [
        pl.BlockSpec((1, tm, d), lambda bi, r: (bi, r, 0)),
        pl.BlockSpec((1, d), lambda bi, r: (0, 0)),
        pl.BlockSpec((1, 1, d), lambda bi, r: (bi, 0, 0)),
        pl.BlockSpec((1, 1, d), lambda bi, r: (bi, 0, 0)),
    ]
    o_spec = pl.BlockSpec((1, tm, d), lambda bi, r: (bi, r, 0))
    o_shape = jax.ShapeDtypeStruct((b, l, d), BF16)
    if router is None:
        return pl.pallas_call(
            _nm_body, name="norm_mod", grid=(b, l // tm), in_specs=in_specs, out_specs=o_spec,
            out_shape=o_shape, compiler_params=_cparams(2),
        )(h, gain.reshape(1, d), sh, sc)
    ne = router.shape[1]
    router_pad = jnp.pad(router, ((0, 0), (0, LANES - ne)))
    return pl.pallas_call(
        _nm_router_body, name="norm_mod_router", grid=(b, l // tm),
        in_specs=in_specs + [pl.BlockSpec((d, LANES), lambda bi, r: (0, 0))],
        out_specs=[o_spec, pl.BlockSpec((1, tm, LANES), lambda bi, r: (bi, r, 0))],
        out_shape=[o_shape, jax.ShapeDtypeStruct((b, l, LANES), F32)],
        compiler_params=_cparams(2),
    )(h, gain.reshape(1, d), sh, sc, router_pad)


def _final_norm_body(x_ref, g_ref, o_ref):
    v = x_ref[...]
    ms = jnp.mean(v * v, axis=-1, keepdims=True)
    o_ref[...] = v * lax.rsqrt(ms + NORM_EPS) * g_ref[...]


def _final_norm(h, gain, tm=256):
    m, d = h.shape
    return pl.pallas_call(
        _final_norm_body, name="final_norm", grid=(m // tm,),
        in_specs=[pl.BlockSpec((tm, d), lambda i: (i, 0)),
                  pl.BlockSpec((1, d), lambda i: (0, 0))],
        out_specs=pl.BlockSpec((tm, d), lambda i: (i, 0)),
        out_shape=jax.ShapeDtypeStruct((m, d), F32),
        compiler_params=_cparams(1),
    )(h, gain.reshape(1, d))


def _krope_body(z_ref, cos_ref, sin_ref, o_ref):
    z = z_ref[...]
    rot = z[:, :QK_ROPE] * cos_ref[...] + z[:, QK_ROPE:] * sin_ref[...]
    o_ref[...] = jnp.concatenate([rot, jnp.zeros_like(rot)], axis=1).astype(BF16)


def _krope(z_all, cos_t, sin_t, col_block, tm=256):
    m = z_all.shape[0]
    t = cos_t.shape[0]
    tiles_per_batch = t // tm
    return pl.pallas_call(
        _krope_body, name="krope", grid=(m // tm,),
        in_specs=[pl.BlockSpec((tm, LANES), lambda i: (i, col_block)),
                  pl.BlockSpec((tm, QK_ROPE), lambda i: (i % tiles_per_batch, 0)),
                  pl.BlockSpec((tm, QK_ROPE), lambda i: (i % tiles_per_batch, 0))],
        out_specs=pl.BlockSpec((tm, LANES), lambda i: (i, 0)),
        out_shape=jax.ShapeDtypeStruct((m, LANES), BF16),
        compiler_params=_cparams(1),
    )(z_all, cos_t, sin_t)


def _attn_body(q_ref, kn_ref, v_ref, kr_ref, cos_ref, sin_ref, o_ref, kfull_ref, *, scale):
    qi = pl.program_id(2)

    @pl.when(qi == 0)
    def _():
        kfull_ref[:, :QK_NOPE] = kn_ref[0]
        kfull_ref[:, QK_NOPE:] = kr_ref[0]

    q = q_ref[...].astype(F32)
    qn = q[:, :QK_NOPE]
    qr = (q[:, QK_NOPE:QK_NOPE + QK_ROPE] * cos_ref[...]
          + q[:, QK_NOPE + QK_ROPE:] * sin_ref[...])
    qf = (jnp.concatenate([qn, qr, jnp.zeros_like(qr)], axis=1) * scale).astype(BF16)
    s = lax.dot_general(qf, kfull_ref[...], (((1,), (1,)), ((), ())),
                        preferred_element_type=F32)
    m = jnp.max(s, axis=-1, keepdims=True)
    p = jnp.exp(s - m)
    l = jnp.sum(p, axis=-1, keepdims=True)
    o = jnp.dot(p.astype(BF16), v_ref[0], preferred_element_type=F32)
    o_ref[...] = (o / l).astype(BF16)


def _attention(q, kv, kr, cos_q, sin_q, batch, tq=256):
    m = q.shape[0]
    l = m // batch
    t = kv.shape[1]
    nq = l // tq
    scale = (QK_NOPE + QK_ROPE) ** -0.5
    hw = QK_NOPE + 2 * QK_ROPE
    return pl.pallas_call(
        functools.partial(_attn_body, scale=scale), name="attention",
        grid=(batch, MLA_HEADS, nq),
        in_specs=[
            pl.BlockSpec((tq, hw), lambda b, h, i: (b * nq + i, h)),
            pl.BlockSpec((1, t, QK_NOPE), lambda b, h, i: (b, 0, 2 * h)),
            pl.BlockSpec((1, t, V_HEAD), lambda b, h, i: (b, 0, 2 * h + 1)),
            pl.BlockSpec((1, t, LANES), lambda b, h, i: (b, 0, 0)),
            pl.BlockSpec((tq, QK_ROPE), lambda b, h, i: (i, 0)),
            pl.BlockSpec((tq, QK_ROPE), lambda b, h, i: (i, 0)),
        ],
        out_specs=pl.BlockSpec((tq, V_HEAD), lambda b, h, i: (b * nq + i, h)),
        out_shape=jax.ShapeDtypeStruct((m, MLA_HEADS * V_HEAD), BF16),
        scratch_shapes=[pltpu.VMEM((t, QK_NOPE + LANES), BF16)],
        compiler_params=_cparams(3),
    )(q, kv, kv, kr, cos_q, sin_q)


def _sgu_body(u_ref, v_ref, lw_ref, lb_ref, ws_ref, bs_ref, o_ref, vn_ref):
    v = v_ref[...].astype(F32)
    mu = jnp.mean(v, axis=-1, keepdims=True)
    vc = v - mu
    var = jnp.mean(vc * vc, axis=-1, keepdims=True)
    vn_ref[...] = (vc * lax.rsqrt(var + 1e-5) * lw_ref[...] + lb_ref[...]).astype(BF16)
    n_chunks = v_ref.shape[0] // CHUNK
    for n in range(n_chunks):
        rows = slice(n * CHUNK, (n + 1) * CHUNK)
        for g in range(SG_GROUPS):
            cols = slice(g * LANES, (g + 1) * LANES)
            vm = jnp.dot(ws_ref[g], vn_ref[rows, cols], preferred_element_type=F32)
            vm = vm + bs_ref[:, cols]
            o_ref[rows, cols] = (u_ref[rows, cols].astype(F32) * vm).astype(BF16)


def _spatial_gate(hg, ln_w, ln_b, w_s, b_s, tm=256):
    m = hg.shape[0]
    d = hg.shape[1] // 2
    bs_full = jnp.repeat(b_s.T, d // SG_GROUPS, axis=1)
    return pl.pallas_call(
        _sgu_body, name="spatial_gate", grid=(m // tm,),
        in_specs=[
            pl.BlockSpec((tm, d), lambda i: (i, 0)),
            pl.BlockSpec((tm, d), lambda i: (i, 1)),
            pl.BlockSpec((1, d), lambda i: (0, 0)),
            pl.BlockSpec((1, d), lambda i: (0, 0)),
            pl.BlockSpec((SG_GROUPS, CHUNK, CHUNK), lambda i: (0, 0, 0)),
            pl.BlockSpec((CHUNK, d), lambda i: (0, 0)),
        ],
        out_specs=pl.BlockSpec((tm, d), lambda i: (i, 0)),
        out_shape=jax.ShapeDtypeStruct((m, d), BF16),
        scratch_shapes=[pltpu.VMEM((tm, d), BF16)],
        compiler_params=_cparams(1),
    )(hg, hg, ln_w.reshape(1, d), ln_b.reshape(1, d), w_s.astype(BF16), bs_full)


def _softplus(x):
    return jnp.maximum(x, 0.0) + jnp.log1p(jnp.exp(-jnp.abs(x)))


def _prep_body(z_ref, zp_ref, zn_ref, mup_ref, mun_ref, kk_ref, ka_ref, w0_ref, a0_ref,
               wup_ref, aup_ref, gup_ref, rk_ref, e_ref, et_ref,
               lw0_o, lw1_o, k0_o, k1_o, b0_o, b1_o, v_o, kkn_o, r_o, g_o, bonus_o,
               *, tiles_ctx, tiles_total):
    tm = z_ref.shape[0]
    rt = pl.program_id(0) % tiles_total
    in_ctx = rt < tiles_ctx
    has_prev = jnp.where(in_ctx, rt > 0, rt > tiles_ctx)
    has_next = jnp.where(in_ctx, rt < tiles_ctx - 1, rt < tiles_total - 1)

    z = z_ref[...]
    rows = lax.broadcasted_iota(jnp.int32, z.shape, 0)
    prev_row = jnp.where(has_prev, zp_ref[7:8, :], 0.0)
    next_row = jnp.where(has_next, zn_ref[0:1, :], 0.0)
    z_prev = jnp.where(rows == 0, prev_row, pltpu.roll(z, 1, 0))
    z_next = jnp.where(rows == tm - 1, next_row, pltpu.roll(z, tm - 1, 0))
    zs = z + mup_ref[...] * (z_prev - z) + mun_ref[...] * (z_next - z)

    d = RWKV_DIM
    r = zs[:, 0:d]
    k = zs[:, d:2 * d]
    v = zs[:, 2 * d:3 * d]
    wd = zs[:, 3 * d:3 * d + LANES]
    ad = zs[:, 3 * d + LANES:3 * d + 2 * LANES]
    gd = zs[:, 3 * d + 2 * LANES:3 * d + 4 * LANES]

    def head_sum(x):
        s = jnp.dot(x, e_ref[...], precision=HI, preferred_element_type=F32)
        return jnp.dot(s, et_ref[...], precision=HI, preferred_element_type=F32)

    kk = k * kk_ref[...]
    nrm = jnp.sqrt(head_sum(kk * kk))
    kkn = kk / jnp.maximum(nrm, 1e-12)

    wx = w0_ref[...] + jnp.dot(jnp.tanh(wd), wup_ref[...], precision=HI,
                               preferred_element_type=F32)
    w_log = -_softplus(-wx) - 0.5
    lw = -jnp.exp(w_log)
    ax = a0_ref[...] + jnp.dot(ad, aup_ref[...], precision=HI, preferred_element_type=F32)
    iclr = jax.nn.sigmoid(ax)
    g = jnp.dot(jax.nn.sigmoid(gd), gup_ref[...], precision=HI, preferred_element_type=F32)

    ka = ka_ref[...]
    k0 = k * (1.0 + (iclr[:, :d] - 1.0) * ka)
    k1 = k * (1.0 + (iclr[:, d:] - 1.0) * ka)
    lw0_o[...] = lw[:, :d]
    lw1_o[...] = lw[:, d:]
    k0_o[...] = k0
    k1_o[...] = k1
    b0_o[...] = iclr[:, :d] * kkn
    b1_o[...] = iclr[:, d:] * kkn
    v_o[...] = v
    kkn_o[...] = kkn
    r_o[...] = r
    g_o[...] = g
    bonus_o[...] = head_sum(r * (k0 + k1) * rk_ref[...]) * v


def _head_indicator():
    head_of = jnp.arange(RWKV_DIM) // RWKV_HEAD
    return (head_of[:, None] == jnp.arange(LANES)[None, :]).astype(F32)


def _rwkv_prep(z_rw, p, rows_ctx, rows_total):
    m, w = z_rw.shape
    tm = PREP_TM
    d = RWKV_DIM
    assert rows_ctx % tm == 0 and rows_total % tm == 0 and m % rows_total == 0
    pad = w - RWKV_IN
    mup = jnp.pad(p['mu_prev'], (0, pad)).reshape(1, w)
    mun = jnp.pad(p['mu_next'], (0, pad)).reshape(1, w)
    zero = jnp.zeros((DECAY_LORA, d), F32)
    wup = jnp.concatenate([jnp.concatenate([p['w_up'][0], zero], axis=1),
                           jnp.concatenate([zero, p['w_up'][1]], axis=1)], axis=0)
    aup = jnp.concatenate([jnp.concatenate([p['a_up'][0], zero], axis=1),
                           jnp.concatenate([zero, p['a_up'][1]], axis=1)], axis=0)
    gup = jnp.pad(p['g_up'], ((0, 2 * LANES - GATE_LORA), (0, 0)))
    e = _head_indicator()

    def full(shape):
        return pl.BlockSpec(shape, lambda i: (0,) * len(shape))

    n8 = m // 8
    out_spec = pl.BlockSpec((tm, d), lambda i: (i, 0))
    out_sds = jax.ShapeDtypeStruct((m, d), F32)
    return pl.pallas_call(
        functools.partial(_prep_body, tiles_ctx=rows_ctx // tm, tiles_total=rows_total // tm),
        name="rwkv_prep",
        grid=(m // tm,),
        in_specs=[
            pl.BlockSpec((tm, w), lambda i: (i, 0)),
            pl.BlockSpec((8, w), lambda i: (jnp.maximum(i * (tm // 8) - 1, 0), 0)),
            pl.BlockSpec((8, w), lambda i: (jnp.minimum((i + 1) * (tm // 8), n8 - 1), 0)),
            full((1, w)), full((1, w)), full((1, d)), full((1, d)), full((1, 2 * d)),
            full((1, 2 * d)), full((LANES, 2 * d)), full((LANES, 2 * d)), full((2 * LANES, d)),
            full((1, d)), full((d, LANES)), full((LANES, d)),
        ],
        out_specs=[out_spec] * 11,
        out_shape=[out_sds] * 11,
        compiler_params=_cparams(1),
    )(z_rw, z_rw, z_rw, mup, mun, p['k_k'].reshape(1, d), p['k_a'].reshape(1, d),
      p['w0'].reshape(1, 2 * d), p['a0'].reshape(1, 2 * d), wup, aup, gup,
      p['r_k'].reshape(1, d), e, e.T)


def _bdot(x, y):
    return jnp.dot(x.astype(BF16), y.astype(BF16), preferred_element_type=F32)


def _chunk_unit(lw, k, b, kkn, v, r, *, reverse):
    c = RW_CHUNK
    row = lax.broadcasted_iota(jnp.int32, (c, LANES), 0)
    lane = lax.broadcasted_iota(jnp.int32, (c, LANES), 1)
    pos = lane % c
    lo = lane < c
    tr = lax.broadcasted_iota(jnp.int32, (c, c), 0)
    tc = lax.broadcasted_iota(jnp.int32, (c, c), 1)
    if reverse:
        tri = (tc >= tr).astype(F32)
        strict = pos > row
        incl = pos >= row
        last = 0
    else:
        tri = (tc <= tr).astype(F32)
        strict = pos < row
        incl = pos <= row
        last = c - 1

    def sb(x):
        return jnp.concatenate([jnp.where(lo, x, 0.0), jnp.where(lo, 0.0, x)], axis=0)

    def nt(x, y):
        return lax.dot_general(x.astype(BF16), y.astype(BF16), (((1,), (1,)), ((), ())),
                               preferred_element_type=F32)

    def tn(x, y):
        return lax.dot_general(x.astype(BF16), y.astype(BF16), (((0,), (0,)), ((), ())),
                               preferred_element_type=F32)

    lc = jnp.dot(tri, lw, precision=HI, preferred_element_type=F32)
    ltot = lc[last:last + 1, :]
    e_neg = jnp.exp(-lc)
    e_h = jnp.exp(ltot - lc)
    at = -kkn * jnp.exp(lc - lw)
    rt = r * jnp.exp(lc)
    bt = b * e_neg
    kt = k * e_neg
    bh = b * e_h
    kh = k * e_h

    ar = jnp.concatenate([at, rt], axis=0)
    m_b = nt(ar, sb(bt))
    m_k = nt(ar, sb(kt))
    n1 = jnp.where(strict, m_b[:c], 0.0)
    m_ak = jnp.where(strict, m_k[:c], 0.0)
    m_rb = jnp.where(incl, m_b[c:], 0.0)
    m_rk = jnp.where(incl, m_k[c:], 0.0)
    mv = _bdot(jnp.concatenate([m_ak, m_rk], axis=0), sb(v))
    xa = at
    xu = mv[:c]
    n = n1
    n_steps = c.bit_length() - 1
    for step in range(n_steps):
        x_sb = jnp.concatenate([sb(xa), sb(xu)], axis=1)
        upd = _bdot(n, x_sb)
        xa = xa + upd[:, :LANES]
        xu = xu + upd[:, LANES:]
        if step < n_steps - 1:
            n = _bdot(n, sb(n))
    x_sb = jnp.concatenate([sb(xa), sb(xu)], axis=1)
    rx = _bdot(m_rb, x_sb)
    rp = rt + rx[:, :LANES]
    y0 = rx[:, LANES:] + mv[c:]
    gh = tn(bh, jnp.concatenate([xa, xu], axis=1))
    hk = tn(kh, v)

    def fold(x):
        return jnp.where(lo, x[:c], 0.0) + jnp.where(lo, 0.0, x[c:])

    g = fold(gh[:, :LANES]) + jnp.where(pos == row, jnp.exp(ltot), 0.0)
    h = fold(gh[:, LANES:] + hk)
    return g, h, rp, y0


def _chunk_body(lw0_ref, lw1_ref, k0_ref, k1_ref, b0_ref, b1_ref, v_ref, kkn_ref, r_ref,
                g0_o, h0_o, rp0_o, y00_o, g1_o, h1_o, rp1_o, y01_o):
    n_chunks = v_ref.shape[0] // RW_CHUNK

    def body(ci, carry):
        rows = pl.ds(pl.multiple_of(ci * RW_CHUNK, RW_CHUNK), RW_CHUNK)
        v = v_ref[rows, :]
        kkn = kkn_ref[rows, :]
        r = r_ref[rows, :]
        g, h, rp, y0 = _chunk_unit(lw0_ref[rows, :], k0_ref[rows, :], b0_ref[rows, :], kkn, v, r,
                                   reverse=False)
        g0_o[rows, :] = g
        h0_o[rows, :] = h
        rp0_o[rows, :] = rp
        y00_o[rows, :] = y0
        g, h, rp, y0 = _chunk_unit(lw1_ref[rows, :], k1_ref[rows, :], b1_ref[rows, :], kkn, v, r,
                                   reverse=True)
        g1_o[rows, :] = g
        h1_o[rows, :] = h
        rp1_o[rows, :] = rp
        y01_o[rows, :] = y0
        return carry

    lax.fori_loop(0, n_chunks, body, 0)


def _rwkv_chunk(prep, tm=256):
    lw0, lw1, k0, k1, b0, b1, v, kkn, r = prep
    m, d = v.shape
    spec = pl.BlockSpec((tm, LANES), lambda i, pr: (i, pr))
    sds = jax.ShapeDtypeStruct((m, d), F32)
    return pl.pallas_call(
        _chunk_body, name="rwkv_chunk", grid=(m // tm, d // LANES),
        in_specs=[spec] * 9, out_specs=[spec] * 8, out_shape=[sds] * 8,
        compiler_params=_cparams(2),
    )(lw0, lw1, k0, k1, b0, b1, v, kkn, r)


def _scan_body(g_ref, h_ref, rp_ref, y0_ref, y_ref, s_ref):
    c = RW_CHUNK
    step = pl.program_id(1)

    @pl.when(step == 0)
    def _():
        s_ref[...] = jnp.zeros(s_ref.shape, F32)

    lane = lax.broadcasted_iota(jnp.int32, (c, LANES), 1)
    lo = lane < c

    def sb(x):
        return jnp.concatenate([jnp.where(lo, x, 0.0), jnp.where(lo, 0.0, x)], axis=0)

    for pr in range(N_PAIRS):
        cols = slice(pr * LANES, (pr + 1) * LANES)
        lhs = jnp.concatenate([rp_ref[:, cols], sb(g_ref[:, cols])], axis=0)
        out = _bdot(lhs, s_ref[pr])
        y_ref[:, cols] = out[:c] + y0_ref[:, cols]
        s_ref[pr] = out[c:] + sb(h_ref[:, cols])


def _rwkv_scan(mats, batch, chunks_ctx):
    m, d = mats[0].shape
    c = RW_CHUNK
    n_chunks = m // c // batch

    def chunk_of(b, s, reverse):
        if not reverse:
            return b * n_chunks + s
        rev = jnp.where(s < chunks_ctx, chunks_ctx - 1 - s, n_chunks - 1 - (s - chunks_ctx))
        return b * n_chunks + rev

    ys = []
    for dr in range(2):
        spec = pl.BlockSpec((c, d), functools.partial(
            lambda b, s, reverse: (chunk_of(b, s, reverse), 0), reverse=bool(dr)))
        ys.append(pl.pallas_call(
            _scan_body, name="rwkv_scan", grid=(batch, n_chunks),
            in_specs=[spec] * 4, out_specs=spec,
            out_shape=jax.ShapeDtypeStruct((m, d), F32),
            scratch_shapes=[pltpu.VMEM((N_PAIRS, LANES, LANES), F32)],
            compiler_params=_cparams(2),
        )(*mats[4 * dr:4 * dr + 4]))
    return ys


def _readout_body(y0_ref, y1_ref, bonus_ref, g_ref, att_ref, lw_ref, lb_ref, e_ref, et_ref, o_ref):
    def head_mean(x):
        s = jnp.dot(x, e_ref[...], precision=HI, preferred_element_type=F32)
        return jnp.dot(s, et_ref[...], precision=HI, preferred_element_type=F32) * (1.0 / RWKV_HEAD)

    y = y0_ref[...] + y1_ref[...]
    mu = head_mean(y)
    yc = y - mu
    var = head_mean(yc * yc)
    yn = yc * lax.rsqrt(var + LNX_EPS) * lw_ref[...] + lb_ref[...]
    d_att = att_ref.shape[1]
    o_ref[:, :d_att] = att_ref[...]
    o_ref[:, d_att:] = ((yn + bonus_ref[...]) * g_ref[...]).astype(BF16)


def _rwkv_readout(y0, y1, bonus, g, att, p, rows_ctx, rows_total, tm=256):
    d = RWKV_DIM
    n_lat, d_att = att.shape
    assert rows_ctx % tm == 0 and rows_total % tm == 0
    per_b = (rows_total - rows_ctx) // tm
    lat0 = rows_ctx // tm
    tot = rows_total // tm
    e = _head_indicator()
    rw_spec = pl.BlockSpec((tm, d), lambda i: ((i // per_b) * tot + lat0 + i % per_b, 0))

    def const(shape):
        return pl.BlockSpec(shape, lambda i: (0, 0))

    return pl.pallas_call(
        _readout_body, name="rwkv_readout", grid=(n_lat // tm,),
        in_specs=[rw_spec, rw_spec, rw_spec, rw_spec,
                  pl.BlockSpec((tm, d_att), lambda i: (i, 0)),
                  const((1, d)), const((1, d)), const((d, LANES)), const((LANES, d))],
        out_specs=pl.BlockSpec((tm, d_att + d), lambda i: (i, 0)),
        out_shape=jax.ShapeDtypeStruct((n_lat, d_att + d), BF16),
        compiler_params=_cparams(1),
    )(y0, y1, bonus, g, att, p['lnx_w'].reshape(1, d), p['lnx_b'].reshape(1, d), e, e.T)


def _rwkv_mixer(z_rw, att, p, batch, rows_ctx, rows_total):
    outs = _rwkv_prep(z_rw, p, rows_ctx, rows_total)
    g, bonus = outs[9], outs[10]
    mats = _rwkv_chunk(outs[:9])
    y0, y1 = _rwkv_scan(mats, batch, rows_ctx // RW_CHUNK)
    return _rwkv_readout(y0, y1, bonus, g, att, p, rows_ctx, rows_total)


def _rope_swap_cols(w):
    f = ROPE_FREQS
    parts = []
    for a in range(2):
        x1 = w[..., (2 * a) * f:(2 * a + 1) * f]
        x2 = w[..., (2 * a + 1) * f:(2 * a + 2) * f]
        parts += [-x2, x1]
    return jnp.concatenate(parts, axis=-1)


def _rope_tables(l):
    rows = l // GRID_W
    row = jnp.repeat(jnp.arange(rows), GRID_W).astype(F32)
    col = jnp.tile(jnp.arange(GRID_W), rows).astype(F32)
    inv = ROPE_THETA ** (-jnp.arange(ROPE_FREQS, dtype=F32) / ROPE_FREQS)
    ar = row[:, None] * inv
    ac = col[:, None] * inv
    cos = jnp.concatenate([jnp.cos(ar), jnp.cos(ar), jnp.cos(ac), jnp.cos(ac)], axis=1)
    sin = jnp.concatenate([jnp.sin(ar), jnp.sin(ar), jnp.sin(ac), jnp.sin(ac)], axis=1)
    return cos, sin


def _moe(xn, logits, w1, w3, w2):
    n_tok, d = xn.shape
    tm = MOE_TM
    top_val, top_idx = lax.top_k(logits, TOP_K)
    gate = jax.nn.softmax(top_val, axis=-1)
    n_assign = n_tok * TOP_K
    e_flat = top_idx.reshape(-1).astype(jnp.int32)
    tok_flat = jnp.repeat(jnp.arange(n_tok, dtype=jnp.int32), TOP_K)
    order = jnp.argsort(e_flat)
    e_sorted = e_flat[order]
    counts = jnp.bincount(e_flat, length=N_EXPERTS).astype(jnp.int32)
    padded = (counts + tm - 1) // tm * tm
    pad_end = jnp.cumsum(padded)
    pad_start = pad_end - padded
    grp_start = jnp.cumsum(counts) - counts
    dest = pad_start[e_sorted] + jnp.arange(n_assign, dtype=jnp.int32) - grp_start[e_sorted]
    n_tiles = n_assign // tm + N_EXPERTS
    n_rows = n_tiles * tm
    row_tok = jnp.zeros((n_rows,), jnp.int32).at[dest].set(tok_flat[order])
    pos = jnp.zeros((n_assign,), jnp.int32).at[order].set(dest).reshape(n_tok, TOP_K)
    tile_start = jnp.arange(n_tiles, dtype=jnp.int32) * tm
    n_valid = pad_end[-1] // tm
    tile_src = jnp.minimum(jnp.arange(n_tiles, dtype=jnp.int32), n_valid - 1).astype(jnp.int32)
    tile_exp_all = jnp.minimum(jnp.searchsorted(pad_end, tile_start, side='right'),
                               N_EXPERTS - 1).astype(jnp.int32)
    tile_exp = tile_exp_all[tile_src]
    tile_valid = (tile_start < pad_end[-1]).astype(jnp.int32)
    group = (tile_exp, tile_valid, tile_src)

    xn32 = lax.bitcast_convert_type(xn.reshape(n_tok, d // 2, 2), jnp.uint32)
    xs = lax.bitcast_convert_type(jnp.take(xn32, row_tok, axis=0), BF16).reshape(n_rows, d)
    hmid = _matmul(xs, [w1, w3], name="moe_up", tm=tm, tn=512, out_dtype=BF16,
                   epilogue="swiglu", group=group)
    ys = _matmul(hmid, [w2], name="moe_down", tm=tm, tn=256, out_dtype=F32, group=group)
    y = (jnp.take(ys, pos[:, 0], axis=0) * gate[:, 0:1]
         + jnp.take(ys, pos[:, 1], axis=0) * gate[:, 1:2])
    return y


def kernel(x, c, ctx, c_ctx, l0_ada_w, l0_ada_b, l0_norm1, l0_norm2, l0_w_in, l0_q_norm, l0_w_uq, l0_kv_norm, l0_w_ukv, l0_mu_prev, l0_mu_next, l0_w0, l0_w_up, l0_a0, l0_a_up, l0_g_up, l0_k_k, l0_k_a, l0_r_k, l0_lnx_w, l0_lnx_b, l0_w_o, l0_ffn_w1, l0_ffn_w3, l0_ffn_w2, l1_ada_w, l1_ada_b, l1_norm1, l1_norm2, l1_w_in, l1_v_ln_w, l1_v_ln_b, l1_w_s, l1_b_s, l1_w_o, l1_router, l1_moe_w1, l1_moe_w3, l1_moe_w2, final_norm):
    b, l, d = x.shape
    lc = ctx.shape[1]
    t = lc + l
    n_tok = b * l

    cond = jnp.zeros((8, d), F32).at[:b].set(c).at[b].set(c_ctx)
    mod0 = _matmul(cond, [l0_ada_w], name="ada_mod", tm=8, tn=1536, out_dtype=F32, prologue="silu",
                   epilogue="bias", bias=l0_ada_b)
    mod1 = _matmul(cond, [l1_ada_w], name="ada_mod", tm=8, tn=1536, out_dtype=F32, prologue="silu",
                   epilogue="bias", bias=l1_ada_b)

    def mods(mod, row0, nrows):
        return [mod[row0:row0 + nrows, i * d:(i + 1) * d].reshape(nrows, 1, d) for i in range(N_MOD)]

    sh1, sc1, g1, sh2, sc2, g2 = mods(mod0, 0, b)
    csh1, csc1 = mods(mod0, b, 1)[:2]

    xall = _norm_mod_merge(x, ctx, l0_norm1, sh1, sc1, csh1, csc1)
    mla_in = Q_LORA + KV_LORA + QK_ROPE
    w_kr = l0_w_in[:, Q_LORA + KV_LORA:mla_in]
    w_mla = jnp.concatenate([l0_w_in[:, :mla_in], _rope_swap_cols(w_kr)], axis=1)
    w_rw = jnp.pad(l0_w_in[:, mla_in:], ((0, 0), (0, RWKV_PAD - RWKV_IN)))
    xall2 = xall.reshape(b * t, d)
    z_all = _matmul(xall2, [w_mla], name="in_mla", tm=512, tn=384, out_dtype=F32)
    z_rw = _matmul(xall2, [w_rw], name="in_rwkv", tm=512, tn=512, out_dtype=F32)

    wq = l0_w_uq.reshape(Q_LORA, MLA_HEADS, QK_NOPE + QK_ROPE)
    wq_ext = jnp.concatenate([wq, _rope_swap_cols(wq[..., QK_NOPE:])], axis=-1)
    wq_ext = wq_ext.reshape(Q_LORA, MLA_HEADS * (QK_NOPE + 2 * QK_ROPE))
    tq_rows = 256
    per_b = l // tq_rows
    lat0 = lc // tq_rows

    def lat_rows(i):
        return (i // per_b) * (t // tq_rows) + lat0 + i % per_b

    q = _matmul(z_all, [wq_ext], name="q_up", tm=tq_rows, tn=512, out_dtype=BF16, k=Q_LORA, x_col_block=0,
                x_row_map=lat_rows, m_out=n_tok, prologue="rms", gain=l0_q_norm)
    kv = _matmul(z_all, [l0_w_ukv], name="kv_up", tm=512, tn=512, out_dtype=BF16, k=KV_LORA, x_col_block=1,
                 prologue="rms", gain=l0_kv_norm)
    cos_l, sin_l = _rope_tables(l)
    cos_t = jnp.concatenate([jnp.ones((lc, QK_ROPE), F32), cos_l], axis=0)
    sin_t = jnp.concatenate([jnp.zeros((lc, QK_ROPE), F32), sin_l], axis=0)
    kr = _krope(z_all, cos_t, sin_t, col_block=(Q_LORA + KV_LORA) // LANES)
    att = _attention(q, kv.reshape(b, t, -1), kr.reshape(b, t, LANES), cos_l, sin_l, b)

    p0 = dict(mu_prev=l0_mu_prev, mu_next=l0_mu_next, w0=l0_w0, w_up=l0_w_up, a0=l0_a0,
              a_up=l0_a_up, g_up=l0_g_up, k_k=l0_k_k, k_a=l0_k_a, r_k=l0_r_k,
              lnx_w=l0_lnx_w, lnx_b=l0_lnx_b)
    mix = _rwkv_mixer(z_rw, att, p0, b, lc, t)
    h = _matmul(mix, [l0_w_o], name="mix_out", tm=512, tn=512, out_dtype=F32, epilogue="resid",
                resid=x.reshape(n_tok, d), gate=g1, rows_per_gate=l)

    xn = _norm_mod(h.reshape(b, l, d), l0_norm2, sh2, sc2).reshape(n_tok, d)
    hmid = _matmul(xn, [l0_ffn_w1, l0_ffn_w3], name="ffn_up", tm=512, tn=512, out_dtype=BF16,
                   epilogue="swiglu")
    h = _matmul(hmid, [l0_ffn_w2], name="ffn_down", tm=512, tn=256, out_dtype=F32, epilogue="resid",
                resid=h, gate=g2, rows_per_gate=l)

    sh1, sc1, g1, sh2, sc2, g2 = mods(mod1, 0, b)
    xn = _norm_mod(h.reshape(b, l, d), l1_norm1, sh1, sc1).reshape(n_tok, d)
    hg = _matmul(xn, [l1_w_in], name="gmlp_in", tm=512, tn=512, out_dtype=BF16, epilogue="gelu")
    gated = _spatial_gate(hg, l1_v_ln_w, l1_v_ln_b, l1_w_s, l1_b_s)
    h = _matmul(gated, [l1_w_o], name="gmlp_out", tm=512, tn=512, out_dtype=F32, epilogue="resid",
                resid=h, gate=g1, rows_per_gate=l)

    xn, logits = _norm_mod(h.reshape(b, l, d), l1_norm2, sh2, sc2, router=l1_router)
    y = _moe(xn.reshape(n_tok, d), logits.reshape(n_tok, LANES)[:, :N_EXPERTS],
             l1_moe_w1, l1_moe_w3, l1_moe_w2)
    h = h + (g2 * y.reshape(b, l, d)).reshape(n_tok, d)

    return _final_norm(h, final_norm).reshape(b, l, d)
```

```python
import functools

import jax
import jax.numpy as jnp
from jax import lax
from jax.experimental import pallas as pl
from jax.experimental.pallas import tpu as pltpu

F32 = jnp.float32
BF16 = jnp.bfloat16

D_MODEL = 2048
GRID_W = 64
N_MOD = 6
NORM_EPS = 1e-6
V_HEAD = 128
MLA_HEADS = 8
Q_LORA = 512
KV_LORA = 512
QK_NOPE = 128
QK_ROPE = 64
ROPE_FREQS = 16
ROPE_THETA = 10000.0
RWKV_HEAD = 64
RWKV_HEADS = 16
RWKV_DIM = 1024
DECAY_LORA = 64
ICLR_LORA = 64
GATE_LORA = 160
LNX_EPS = 64e-5
RWKV_IN = 3 * RWKV_DIM + 2 * DECAY_LORA + 2 * ICLR_LORA + GATE_LORA
RWKV_PAD = 3584
RW_CHUNK = 64
PREP_TM = 128
N_PAIRS = RWKV_HEADS // 2
HI = lax.Precision.HIGHEST
CHUNK = 128
SG_GROUPS = 16
D_FF = 7168
N_EXPERTS = 8
TOP_K = 2

VMEM_LIMIT_BYTES = 56 * 1024 * 1024
LANES = 128

MOE_TM = 512


def _cparams(n_axes):
    return pltpu.CompilerParams(
        dimension_semantics=("arbitrary",) * n_axes,
        vmem_limit_bytes=VMEM_LIMIT_BYTES)


def _mm_body(*refs, n_w, prologue, epilogue, grouped):
    refs = list(refs)
    if grouped:
        te_ref, tv_ref, _ = refs[:3]
        refs = refs[3:]
    x_ref = refs.pop(0)
    w_refs = [refs.pop(0) for _ in range(n_w)]
    gain_ref = refs.pop(0) if prologue == "rms" else None
    bias_ref = refs.pop(0) if epilogue == "bias" else None
    if epilogue == "resid":
        resid_ref = refs.pop(0)
        gate_ref = refs.pop(0)
    o_ref, wbf_ref = refs

    i = pl.program_id(1)
    if grouped:
        new_w = jnp.logical_or(i == 0, te_ref[i] != te_ref[jnp.maximum(i - 1, 0)])
    else:
        new_w = i == 0

    @pl.when(new_w)
    def _():
        for n in range(n_w):
            wbf_ref[n] = w_refs[n][...].astype(BF16)

    def compute():
        x = x_ref[...]
        if prologue == "rms":
            xf = x.astype(F32)
            ms = jnp.mean(xf * xf, axis=-1, keepdims=True)
            x = xf * lax.rsqrt(ms + NORM_EPS) * gain_ref[...]
        elif prologue == "silu":
            xf = x.astype(F32)
            x = xf * jax.nn.sigmoid(xf)
        elif prologue == "unpack":
            x = _unpack_bf16_pairs(x)
        x = x.astype(BF16)
        acc = [jnp.dot(x, wbf_ref[n], preferred_element_type=F32) for n in range(n_w)]
        if epilogue == "swiglu":
            a = acc[0]
            out = a * jax.nn.sigmoid(a) * acc[1]
        elif epilogue == "gelu":
            a = acc[0]
            out = 0.5 * a * (1.0 + lax.erf(a * (2.0 ** -0.5)))
        elif epilogue == "bias":
            out = acc[0] + bias_ref[...]
        elif epilogue == "resid":
            out = resid_ref[...] + gate_ref[0] * acc[0]
        else:
            out = acc[0]
        o_ref[...] = out.astype(o_ref.dtype)

    if grouped:
        @pl.when(tv_ref[i] == 1)
        def _():
            compute()

        @pl.when(tv_ref[i] == 0)
        def _():
            o_ref[...] = jnp.zeros(o_ref.shape, o_ref.dtype)
    else:
        compute()


def _matmul(x, ws, *, name, tm, tn, out_dtype, k=None, x_col_block=0, x_row_map=None, m_out=None,
            prologue=None, gain=None, epilogue=None, bias=None, resid=None, gate=None,
            rows_per_gate=None, group=None):
    grouped = group is not None
    kdim = k if k is not None else x.shape[1]
    xk = kdim
    if prologue == "unpack":
        xk, kdim = kdim, 2 * kdim
    n = ws[0].shape[-1]
    m = m_out if m_out is not None else x.shape[0]
    assert m % tm == 0 and n % tn == 0, (m, tm, n, tn)
    n_w = len(ws)
    grid = (n // tn, m // tm)

    if grouped:
        def xmap(j, i, te, tv, ts):
            return (ts[i], x_col_block)

        def wmap(j, i, te, tv, ts):
            return (te[i], 0, j)

        def omap(j, i, te, tv, ts):
            return (i, j)
        w_spec = pl.BlockSpec((None, kdim, tn), wmap)
    else:
        def xmap(j, i):
            return ((x_row_map(i) if x_row_map is not None else i), x_col_block)

        def wmap(j, i):
            return (0, j)

        def omap(j, i):
            return (i, j)
        w_spec = pl.BlockSpec((kdim, tn), wmap)

    in_specs = [pl.BlockSpec((tm, xk), xmap)] + [w_spec] * n_w
    args = [x] + list(ws)
    if prologue == "rms":
        in_specs.append(pl.BlockSpec((1, kdim), lambda j, i, *_: (0, 0)))
        args.append(gain.reshape(1, kdim))
    if epilogue == "bias":
        in_specs.append(pl.BlockSpec((1, tn), lambda j, i, *_: (0, j)))
        args.append(bias.reshape(1, n))
    if epilogue == "resid":
        in_specs.append(pl.BlockSpec((tm, tn), omap))
        args.append(resid)
        assert rows_per_gate % tm == 0
        tiles_per_gate = rows_per_gate // tm
        in_specs.append(pl.BlockSpec((1, 1, tn), lambda j, i, *_: (i // tiles_per_gate, 0, j)))
        args.append(gate)

    body = functools.partial(_mm_body, n_w=n_w, prologue=prologue, epilogue=epilogue,
                             grouped=grouped)
    gs = pltpu.PrefetchScalarGridSpec(
        num_scalar_prefetch=3 if grouped else 0,
        grid=grid,
        in_specs=in_specs,
        out_specs=pl.BlockSpec((tm, tn), omap),
        scratch_shapes=[pltpu.VMEM((n_w, kdim, tn), BF16)])
    call = pl.pallas_call(
        body, grid_spec=gs, name=name,
        out_shape=jax.ShapeDtypeStruct((m, n), out_dtype),
        compiler_params=_cparams(2))
    if grouped:
        return call(*group, *args)
    return call(*args)


def _norm_mod_math(v, g, sh, sc):
    ms = jnp.mean(v * v, axis=-1, keepdims=True)
    y = v * lax.rsqrt(ms + NORM_EPS) * g
    return y * (1.0 + sc) + sh


def _nm_merge_body(x_ref, c_ref, g_ref, sh_ref, sc_ref, csh_ref, csc_ref, o_ref):
    r = pl.program_id(1)

    @pl.when(r == 0)
    def _():
        o_ref[0] = _norm_mod_math(c_ref[0], g_ref[...], csh_ref[...], csc_ref[...]).astype(BF16)

    @pl.when(r > 0)
    def _():
        o_ref[0] = _norm_mod_math(x_ref[0], g_ref[...], sh_ref[0], sc_ref[0]).astype(BF16)


def _norm_mod_merge(x, ctx, gain, sh, sc, csh, csc):
    b, l, d = x.shape
    lc = ctx.shape[1]
    tm = lc
    assert l % tm == 0
    nt = l // tm + 1
    return pl.pallas_call(
        _nm_merge_body, name="norm_mod_merge",
        grid=(b, nt),
        in_specs=[
            pl.BlockSpec((1, tm, d), lambda bi, r: (bi, jnp.maximum(r - 1, 0), 0)),
            pl.BlockSpec((1, lc, d), lambda bi, r: (bi, 0, 0)),
            pl.BlockSpec((1, d), lambda bi, r: (0, 0)),
            pl.BlockSpec((1, 1, d), lambda bi, r: (bi, 0, 0)),
            pl.BlockSpec((1, 1, d), lambda bi, r: (bi, 0, 0)),
            pl.BlockSpec((1, d), lambda bi, r: (0, 0)),
            pl.BlockSpec((1, d), lambda bi, r: (0, 0)),
        ],
        out_specs=pl.BlockSpec((1, tm, d), lambda bi, r: (bi, r, 0)),
        out_shape=jax.ShapeDtypeStruct((b, lc + l, d), BF16),
        compiler_params=_cparams(2),
    )(x, ctx, gain.reshape(1, d), sh, sc, csh.reshape(1, d), csc.reshape(1, d))


def _nm_body(x_ref, g_ref, sh_ref, sc_ref, o_ref):
    o_ref[0] = _norm_mod_math(x_ref[0], g_ref[...], sh_ref[0], sc_ref[0]).astype(BF16)


def _pack_bf16_pairs(y):
    half = y.shape[1] // 2
    lo = lax.bitcast_convert_type(y[:, :half].astype(BF16).astype(F32), jnp.uint32)
    hi = lax.bitcast_convert_type(y[:, half:].astype(BF16).astype(F32), jnp.uint32)
    return (hi & jnp.uint32(0xFFFF0000)) | (lo >> 16)


def _unpack_bf16_pairs(p):
    lo = lax.bitcast_convert_type(p << 16, F32)
    hi = lax.bitcast_convert_type(p & jnp.uint32(0xFFFF0000), F32)
    return jnp.concatenate([lo, hi], axis=1).astype(BF16)


def _nm_router_body(x_ref, g_ref, sh_ref, sc_ref, r_ref, o_ref, lg_ref):
    y = _norm_mod_math(x_ref[0], g_ref[...], sh_ref[0], sc_ref[0])
    o_ref[0] = _pack_bf16_pairs(y)
    lg_ref[0] = jnp.dot(y, r_ref[...], precision=lax.Precision.HIGHEST,
                        preferred_element_type=F32)


def _norm_mod(h, gain, sh, sc, router=None, tm=256):
    b, l, d = h.shape
    assert l % tm == 0
    in_specs = [
        pl.BlockSpec((1, tm, d), lambda bi, r: (bi, r, 0)),
        pl.BlockSpec((1, d), lambda bi, r: (0, 0)),
        pl.BlockSpec((1, 1, d), lambda bi, r: (bi, 0, 0)),
        pl.BlockSpec((1, 1, d), lambda bi, r: (bi, 0, 0)),
    ]
    o_spec = pl.BlockSpec((1, tm, d), lambda bi, r: (bi, r, 0))
    o_shape = jax.ShapeDtypeStruct((b, l, d), BF16)
    if router is None:
        return pl.pallas_call(
            _nm_body, name="norm_mod", grid=(b, l // tm), in_specs=in_specs, out_specs=o_spec,
            out_shape=o_shape, compiler_params=_cparams(2),
        )(h, gain.reshape(1, d), sh, sc)
    ne = router.shape[1]
    router_pad = jnp.pad(router, ((0, 0), (0, LANES - ne)))
    return pl.pallas_call(
        _nm_router_body, name="norm_mod_router", grid=(b, l // tm),
        in_specs=in_specs + [pl.BlockSpec((d, LANES), lambda bi, r: (0, 0))],
        out_specs=[pl.BlockSpec((1, tm, d // 2), lambda bi, r: (bi, r, 0)),
                   pl.BlockSpec((1, tm, LANES), lambda bi, r: (bi, r, 0))],
        out_shape=[jax.ShapeDtypeStruct((b, l, d // 2), jnp.uint32),
                   jax.ShapeDtypeStruct((b, l, LANES), F32)],
        compiler_params=_cparams(2),
    )(h, gain.reshape(1, d), sh, sc, router_pad)


def _final_norm_body(x_ref, g_ref, o_ref):
    v = x_ref[...]
    ms = jnp.mean(v * v, axis=-1, keepdims=True)
    o_ref[...] = v * lax.rsqrt(ms + NORM_EPS) * g_ref[...]


def _final_norm(h, gain, tm=256):
    m, d = h.shape
    return pl.pallas_call(
        _final_norm_body, name="final_norm", grid=(m // tm,),
        in_specs=[pl.BlockSpec((tm, d), lambda i: (i, 0)),
                  pl.BlockSpec((1, d), lambda i: (0, 0))],
        out_specs=pl.BlockSpec((tm, d), lambda i: (i, 0)),
        out_shape=jax.ShapeDtypeStruct((m, d), F32),
        compiler_params=_cparams(1),
    )(h, gain.reshape(1, d))


def _krope_body(z_ref, cos_ref, sin_ref, o_ref):
    z = z_ref[...]
    rot = z[:, :QK_ROPE] * cos_ref[...] + z[:, QK_ROPE:] * sin_ref[...]
    o_ref[...] = jnp.concatenate([rot, jnp.zeros_like(rot)], axis=1).astype(BF16)


def _krope(z_all, cos_t, sin_t, col_block, tm=256):
    m = z_all.shape[0]
    t = cos_t.shape[0]
    tiles_per_batch = t // tm
    return pl.pallas_call(
        _krope_body, name="krope", grid=(m // tm,),
        in_specs=[pl.BlockSpec((tm, LANES), lambda i: (i, col_block)),
                  pl.BlockSpec((tm, QK_ROPE), lambda i: (i % tiles_per_batch, 0)),
                  pl.BlockSpec((tm, QK_ROPE), lambda i: (i % tiles_per_batch, 0))],
        out_specs=pl.BlockSpec((tm, LANES), lambda i: (i, 0)),
        out_shape=jax.ShapeDtypeStruct((m, LANES), BF16),
        compiler_params=_cparams(1),
    )(z_all, cos_t, sin_t)


def _attn_body(q_ref, kn_ref, v_ref, kr_ref, cos_ref, sin_ref, o_ref, kfull_ref, *, scale):
    qi = pl.program_id(2)

    @pl.when(qi == 0)
    def _():
        kfull_ref[:, :QK_NOPE] = kn_ref[0]
        kfull_ref[:, QK_NOPE:] = kr_ref[0]

    q = q_ref[...].astype(F32)
    qn = q[:, :QK_NOPE]
    qr = (q[:, QK_NOPE:QK_NOPE + QK_ROPE] * cos_ref[...]
          + q[:, QK_NOPE + QK_ROPE:] * sin_ref[...])
    qf = (jnp.concatenate([qn, qr, jnp.zeros_like(qr)], axis=1) * scale).astype(BF16)
    s = lax.dot_general(qf, kfull_ref[...], (((1,), (1,)), ((), ())),
                        preferred_element_type=F32)
    m = jnp.max(s, axis=-1, keepdims=True)
    p = jnp.exp(s - m)
    l = jnp.sum(p, axis=-1, keepdims=True)
    o = jnp.dot(p.astype(BF16), v_ref[0], preferred_element_type=F32)
    o_ref[...] = (o / l).astype(BF16)


def _attention(q, kv, kr, cos_q, sin_q, batch, tq=256):
    m = q.shape[0]
    l = m // batch
    t = kv.shape[1]
    nq = l // tq
    scale = (QK_NOPE + QK_ROPE) ** -0.5
    hw = QK_NOPE + 2 * QK_ROPE
    return pl.pallas_call(
        functools.partial(_attn_body, scale=scale), name="attention",
        grid=(batch, MLA_HEADS, nq),
        in_specs=[
            pl.BlockSpec((tq, hw), lambda b, h, i: (b * nq + i, h)),
            pl.BlockSpec((1, t, QK_NOPE), lambda b, h, i: (b, 0, 2 * h)),
            pl.BlockSpec((1, t, V_HEAD), lambda b, h, i: (b, 0, 2 * h + 1)),
            pl.BlockSpec((1, t, LANES), lambda b, h, i: (b, 0, 0)),
            pl.BlockSpec((tq, QK_ROPE), lambda b, h, i: (i, 0)),
            pl.BlockSpec((tq, QK_ROPE), lambda b, h, i: (i, 0)),
        ],
        out_specs=pl.BlockSpec((tq, V_HEAD), lambda b, h, i: (b * nq + i, h)),
        out_shape=jax.ShapeDtypeStruct((m, MLA_HEADS * V_HEAD), BF16),
        scratch_shapes=[pltpu.VMEM((t, QK_NOPE + LANES), BF16)],
        compiler_params=_cparams(3),
    )(q, kv, kv, kr, cos_q, sin_q)


def _sgu_body(u_ref, v_ref, lw_ref, lb_ref, ws_ref, bs_ref, o_ref, vn_ref):
    v = v_ref[...].astype(F32)
    mu = jnp.mean(v, axis=-1, keepdims=True)
    vc = v - mu
    var = jnp.mean(vc * vc, axis=-1, keepdims=True)
    vn_ref[...] = (vc * lax.rsqrt(var + 1e-5) * lw_ref[...] + lb_ref[...]).astype(BF16)
    n_chunks = v_ref.shape[0] // CHUNK
    for n in range(n_chunks):
        rows = slice(n * CHUNK, (n + 1) * CHUNK)
        for g in range(SG_GROUPS):
            cols = slice(g * LANES, (g + 1) * LANES)
            vm = jnp.dot(ws_ref[g], vn_ref[rows, cols], preferred_element_type=F32)
            vm = vm + bs_ref[:, cols]
            o_ref[rows, cols] = (u_ref[rows, cols].astype(F32) * vm).astype(BF16)


def _spatial_gate(hg, ln_w, ln_b, w_s, b_s, tm=256):
    m = hg.shape[0]
    d = hg.shape[1] // 2
    bs_full = jnp.repeat(b_s.T, d // SG_GROUPS, axis=1)
    return pl.pallas_call(
        _sgu_body, name="spatial_gate", grid=(m // tm,),
        in_specs=[
            pl.BlockSpec((tm, d), lambda i: (i, 0)),
            pl.BlockSpec((tm, d), lambda i: (i, 1)),
            pl.BlockSpec((1, d), lambda i: (0, 0)),
            pl.BlockSpec((1, d), lambda i: (0, 0)),
            pl.BlockSpec((SG_GROUPS, CHUNK, CHUNK), lambda i: (0, 0, 0)),
            pl.BlockSpec((CHUNK, d), lambda i: (0, 0)),
        ],
        out_specs=pl.BlockSpec((tm, d), lambda i: (i, 0)),
        out_shape=jax.ShapeDtypeStruct((m, d), BF16),
        scratch_shapes=[pltpu.VMEM((tm, d), BF16)],
        compiler_params=_cparams(1),
    )(hg, hg, ln_w.reshape(1, d), ln_b.reshape(1, d), w_s.astype(BF16), bs_full)


def _softplus(x):
    return jnp.maximum(x, 0.0) + jnp.log1p(jnp.exp(-jnp.abs(x)))


def _prep_body(z_ref, zp_ref, zn_ref, mup_ref, mun_ref, kk_ref, ka_ref, w0_ref, a0_ref,
               wup_ref, aup_ref, gup_ref, rk_ref, e_ref, et_ref,
               lw0_o, lw1_o, k0_o, k1_o, b0_o, b1_o, v_o, kkn_o, r_o, g_o, bonus_o,
               *, tiles_ctx, tiles_total):
    tm = z_ref.shape[0]
    rt = pl.program_id(0) % tiles_total
    in_ctx = rt < tiles_ctx
    has_prev = jnp.where(in_ctx, rt > 0, rt > tiles_ctx)
    has_next = jnp.where(in_ctx, rt < tiles_ctx - 1, rt < tiles_total - 1)

    z = z_ref[...]
    rows = lax.broadcasted_iota(jnp.int32, z.shape, 0)
    prev_row = jnp.where(has_prev, zp_ref[7:8, :], 0.0)
    next_row = jnp.where(has_next, zn_ref[0:1, :], 0.0)
    z_prev = jnp.where(rows == 0, prev_row, pltpu.roll(z, 1, 0))
    z_next = jnp.where(rows == tm - 1, next_row, pltpu.roll(z, tm - 1, 0))
    zs = z + mup_ref[...] * (z_prev - z) + mun_ref[...] * (z_next - z)

    d = RWKV_DIM
    r = zs[:, 0:d]
    k = zs[:, d:2 * d]
    v = zs[:, 2 * d:3 * d]
    wd = zs[:, 3 * d:3 * d + LANES]
    ad = zs[:, 3 * d + LANES:3 * d + 2 * LANES]
    gd = zs[:, 3 * d + 2 * LANES:3 * d + 4 * LANES]

    def head_sum(x):
        s = jnp.dot(x, e_ref[...], precision=HI, preferred_element_type=F32)
        return jnp.dot(s, et_ref[...], precision=HI, preferred_element_type=F32)

    kk = k * kk_ref[...]
    nrm = jnp.sqrt(head_sum(kk * kk))
    kkn = kk / jnp.maximum(nrm, 1e-12)

    wx = w0_ref[...] + jnp.dot(jnp.tanh(wd), wup_ref[...], precision=HI,
                               preferred_element_type=F32)
    w_log = -_softplus(-wx) - 0.5
    lw = -jnp.exp(w_log)
    ax = a0_ref[...] + jnp.dot(ad, aup_ref[...], precision=HI, preferred_element_type=F32)
    iclr = jax.nn.sigmoid(ax)
    g = jnp.dot(jax.nn.sigmoid(gd), gup_ref[...], precision=HI, preferred_element_type=F32)

    ka = ka_ref[...]
    k0 = k * (1.0 + (iclr[:, :d] - 1.0) * ka)
    k1 = k * (1.0 + (iclr[:, d:] - 1.0) * ka)
    lw0_o[...] = lw[:, :d]
    lw1_o[...] = lw[:, d:]
    k0_o[...] = k0
    k1_o[...] = k1
    b0_o[...] = iclr[:, :d] * kkn
    b1_o[...] = iclr[:, d:] * kkn
    v_o[...] = v
    kkn_o[...] = kkn
    r_o[...] = r
    g_o[...] = g
    bonus_o[...] = head_sum(r * (k0 + k1) * rk_ref[...]) * v


def _head_indicator():
    head_of = jnp.arange(RWKV_DIM) // RWKV_HEAD
    return (head_of[:, None] == jnp.arange(LANES)[None, :]).astype(F32)


def _rwkv_prep(z_rw, p, rows_ctx, rows_total):
    m, w = z_rw.shape
    tm = PREP_TM
    d = RWKV_DIM
    assert rows_ctx % tm == 0 and rows_total % tm == 0 and m % rows_total == 0
    pad = w - RWKV_IN
    mup = jnp.pad(p['mu_prev'], (0, pad)).reshape(1, w)
    mun = jnp.pad(p['mu_next'], (0, pad)).reshape(1, w)
    zero = jnp.zeros((DECAY_LORA, d), F32)
    wup = jnp.concatenate([jnp.concatenate([p['w_up'][0], zero], axis=1),
                           jnp.concatenate([zero, p['w_up'][1]], axis=1)], axis=0)
    aup = jnp.concatenate([jnp.concatenate([p['a_up'][0], zero], axis=1),
                           jnp.concatenate([zero, p['a_up'][1]], axis=1)], axis=0)
    gup = jnp.pad(p['g_up'], ((0, 2 * LANES - GATE_LORA), (0, 0)))
    e = _head_indicator()

    def full(shape):
        return pl.BlockSpec(shape, lambda i: (0,) * len(shape))

    n8 = m // 8
    out_spec = pl.BlockSpec((tm, d), lambda i: (i, 0))
    out_sds = jax.ShapeDtypeStruct((m, d), F32)
    return pl.pallas_call(
        functools.partial(_prep_body, tiles_ctx=rows_ctx // tm, tiles_total=rows_total // tm),
        name="rwkv_prep",
        grid=(m // tm,),
        in_specs=[
            pl.BlockSpec((tm, w), lambda i: (i, 0)),
            pl.BlockSpec((8, w), lambda i: (jnp.maximum(i * (tm // 8) - 1, 0), 0)),
            pl.BlockSpec((8, w), lambda i: (jnp.minimum((i + 1) * (tm // 8), n8 - 1), 0)),
            full((1, w)), full((1, w)), full((1, d)), full((1, d)), full((1, 2 * d)),
            full((1, 2 * d)), full((LANES, 2 * d)), full((LANES, 2 * d)), full((2 * LANES, d)),
            full((1, d)), full((d, LANES)), full((LANES, d)),
        ],
        out_specs=[out_spec] * 11,
        out_shape=[out_sds] * 11,
        compiler_params=_cparams(1),
    )(z_rw, z_rw, z_rw, mup, mun, p['k_k'].reshape(1, d), p['k_a'].reshape(1, d),
      p['w0'].reshape(1, 2 * d), p['a0'].reshape(1, 2 * d), wup, aup, gup,
      p['r_k'].reshape(1, d), e, e.T)


def _bdot(x, y):
    return jnp.dot(x.astype(BF16), y.astype(BF16), preferred_element_type=F32)


def _chunk_units(units):
    c = RW_CHUNK
    row = lax.broadcasted_iota(jnp.int32, (c, LANES), 0)
    lane = lax.broadcasted_iota(jnp.int32, (c, LANES), 1)
    pos = lane % c
    lo = lane < c
    tr = lax.broadcasted_iota(jnp.int32, (c, c), 0)
    tc = lax.broadcasted_iota(jnp.int32, (c, c), 1)
    tri = {False: (tc <= tr).astype(F32), True: (tc >= tr).astype(F32)}
    strict = {False: pos < row, True: pos > row}
    incl = {False: pos <= row, True: pos >= row}
    last = {False: c - 1, True: 0}
    rev = [u[6] for u in units]
    nu = range(len(units))

    def sb(x):
        return jnp.concatenate([jnp.where(lo, x, 0.0), jnp.where(lo, 0.0, x)], axis=0)

    def nt(x, y):
        return lax.dot_general(x.astype(BF16), y.astype(BF16), (((1,), (1,)), ((), ())),
                               preferred_element_type=F32)

    def tn(x, y):
        return lax.dot_general(x.astype(BF16), y.astype(BF16), (((0,), (0,)), ((), ())),
                               preferred_element_type=F32)

    def fold(x):
        return jnp.where(lo, x[:c], 0.0) + jnp.where(lo, 0.0, x[c:])

    lc = [jnp.dot(tri[rev[i]], units[i][0], precision=HI, preferred_element_type=F32) for i in nu]
    ltot = [lc[i][last[rev[i]]:last[rev[i]] + 1, :] for i in nu]
    e_neg = [jnp.exp(-lc[i]) for i in nu]
    e_h = [jnp.exp(ltot[i] - lc[i]) for i in nu]
    at = [-units[i][3] * jnp.exp(lc[i] - units[i][0]) for i in nu]
    rt = [units[i][5] * jnp.exp(lc[i]) for i in nu]
    bt = [units[i][2] * e_neg[i] for i in nu]
    kt = [units[i][1] * e_neg[i] for i in nu]
    bh = [units[i][2] * e_h[i] for i in nu]
    kh = [units[i][1] * e_h[i] for i in nu]

    ar = [jnp.concatenate([at[i], rt[i]], axis=0) for i in nu]
    m_b = [nt(ar[i], sb(bt[i])) for i in nu]
    m_k = [nt(ar[i], sb(kt[i])) for i in nu]
    n = [jnp.where(strict[rev[i]], m_b[i][:c], 0.0) for i in nu]
    m_ak = [jnp.where(strict[rev[i]], m_k[i][:c], 0.0) for i in nu]
    m_rb = [jnp.where(incl[rev[i]], m_b[i][c:], 0.0) for i in nu]
    m_rk = [jnp.where(incl[rev[i]], m_k[i][c:], 0.0) for i in nu]
    mv = [_bdot(jnp.concatenate([m_ak[i], m_rk[i]], axis=0), sb(units[i][4])) for i in nu]
    xa = at
    xu = [mv[i][:c] for i in nu]
    n_steps = c.bit_length() - 1
    for step in range(n_steps):
        upd = [_bdot(n[i], jnp.concatenate([sb(xa[i]), sb(xu[i])], axis=1)) for i in nu]
        xa = [xa[i] + upd[i][:, :LANES] for i in nu]
        xu = [xu[i] + upd[i][:, LANES:] for i in nu]
        if step < n_steps - 1:
            n = [_bdot(n[i], sb(n[i])) for i in nu]
    rx = [_bdot(m_rb[i], jnp.concatenate([sb(xa[i]), sb(xu[i])], axis=1)) for i in nu]
    gh = [tn(bh[i], jnp.concatenate([xa[i], xu[i]], axis=1)) for i in nu]
    hk = [tn(kh[i], units[i][4]) for i in nu]
    out = []
    for i in nu:
        rp = rt[i] + rx[i][:, :LANES]
        y0 = rx[i][:, LANES:] + mv[i][c:]
        g = fold(gh[i][:, :LANES]) + jnp.where(pos == row, jnp.exp(ltot[i]), 0.0)
        h = fold(gh[i][:, LANES:] + hk[i])
        out.append((g, h, rp, y0))
    return out


def _chunk_body(lw0_ref, lw1_ref, k0_ref, k1_ref, b0_ref, b1_ref, v_ref, kkn_ref, r_ref,
                g0_o, h0_o, rp0_o, y00_o, g1_o, h1_o, rp1_o, y01_o):
    n_chunks = v_ref.shape[0] // RW_CHUNK
    units = []
    for ci in range(n_chunks):
        rows = slice(ci * RW_CHUNK, (ci + 1) * RW_CHUNK)
        v = v_ref[rows, :]
        kkn = kkn_ref[rows, :]
        r = r_ref[rows, :]
        units.append((lw0_ref[rows, :], k0_ref[rows, :], b0_ref[rows, :], kkn, v, r, False))
        units.append((lw1_ref[rows, :], k1_ref[rows, :], b1_ref[rows, :], kkn, v, r, True))
    res = _chunk_units(units)
    for ci in range(n_chunks):
        rows = slice(ci * RW_CHUNK, (ci + 1) * RW_CHUNK)
        for dr, outs in enumerate(((g0_o, h0_o, rp0_o, y00_o), (g1_o, h1_o, rp1_o, y01_o))):
            for o_ref, val in zip(outs, res[2 * ci + dr]):
                o_ref[rows, :] = val


def _rwkv_chunk(prep, tm=256):
    lw0, lw1, k0, k1, b0, b1, v, kkn, r = prep
    m, d = v.shape
    spec = pl.BlockSpec((tm, LANES), lambda i, pr: (i, pr))
    sds = jax.ShapeDtypeStruct((m, d), F32)
    return pl.pallas_call(
        _chunk_body, name="rwkv_chunk", grid=(m // tm, d // LANES),
        in_specs=[spec] * 9, out_specs=[spec] * 8, out_shape=[sds] * 8,
        compiler_params=_cparams(2),
    )(lw0, lw1, k0, k1, b0, b1, v, kkn, r)


def _scan_body(g_ref, h_ref, rp_ref, y0_ref, y_ref, s_ref):
    c = RW_CHUNK
    step = pl.program_id(1)

    @pl.when(step == 0)
    def _():
        s_ref[...] = jnp.zeros(s_ref.shape, F32)

    lane = lax.broadcasted_iota(jnp.int32, (c, LANES), 1)
    lo = lane < c

    def sb(x):
        return jnp.concatenate([jnp.where(lo, x, 0.0), jnp.where(lo, 0.0, x)], axis=0)

    cols = [slice(pr * LANES, (pr + 1) * LANES) for pr in range(N_PAIRS)]
    lhs = [jnp.concatenate([rp_ref[:, cl], sb(g_ref[:, cl])], axis=0) for cl in cols]
    out = [_bdot(lhs[pr], s_ref[pr]) for pr in range(N_PAIRS)]
    for pr, cl in enumerate(cols):
        y_ref[:, cl] = out[pr][:c] + y0_ref[:, cl]
        s_ref[pr] = out[pr][c:] + sb(h_ref[:, cl])


def _rwkv_scan(mats, batch, chunks_ctx):
    m, d = mats[0].shape
    c = RW_CHUNK
    n_chunks = m // c // batch

    def chunk_of(b, s, reverse):
        if not reverse:
            return b * n_chunks + s
        rev = jnp.where(s < chunks_ctx, chunks_ctx - 1 - s, n_chunks - 1 - (s - chunks_ctx))
        return b * n_chunks + rev

    ys = []
    for dr in range(2):
        spec = pl.BlockSpec((c, d), functools.partial(
            lambda b, s, reverse: (chunk_of(b, s, reverse), 0), reverse=bool(dr)))
        ys.append(pl.pallas_call(
            _scan_body, name="rwkv_scan", grid=(batch, n_chunks),
            in_specs=[spec] * 4, out_specs=spec,
            out_shape=jax.ShapeDtypeStruct((m, d), F32),
            scratch_shapes=[pltpu.VMEM((N_PAIRS, LANES, LANES), F32)],
            compiler_params=_cparams(2),
        )(*mats[4 * dr:4 * dr + 4]))
    return ys


def _readout_body(y0_ref, y1_ref, bonus_ref, g_ref, att_ref, lw_ref, lb_ref, e_ref, et_ref, o_ref):
    def head_mean(x):
        s = jnp.dot(x, e_ref[...], precision=HI, preferred_element_type=F32)
        return jnp.dot(s, et_ref[...], precision=HI, preferred_element_type=F32) * (1.0 / RWKV_HEAD)

    y = y0_ref[...] + y1_ref[...]
    mu = head_mean(y)
    yc = y - mu
    var = head_mean(yc * yc)
    yn = yc * lax.rsqrt(var + LNX_EPS) * lw_ref[...] + lb_ref[...]
    d_att = att_ref.shape[1]
    o_ref[:, :d_att] = att_ref[...]
    o_ref[:, d_att:] = ((yn + bonus_ref[...]) * g_ref[...]).astype(BF16)


def _rwkv_readout(y0, y1, bonus, g, att, p, rows_ctx, rows_total, tm=256):
    d = RWKV_DIM
    n_lat, d_att = att.shape
    assert rows_ctx % tm == 0 and rows_total % tm == 0
    per_b = (rows_total - rows_ctx) // tm
    lat0 = rows_ctx // tm
    tot = rows_total // tm
    e = _head_indicator()
    rw_spec = pl.BlockSpec((tm, d), lambda i: ((i // per_b) * tot + lat0 + i % per_b, 0))

    def const(shape):
        return pl.BlockSpec(shape, lambda i: (0, 0))

    return pl.pallas_call(
        _readout_body, name="rwkv_readout", grid=(n_lat // tm,),
        in_specs=[rw_spec, rw_spec, rw_spec, rw_spec,
                  pl.BlockSpec((tm, d_att), lambda i: (i, 0)),
                  const((1, d)), const((1, d)), const((d, LANES)), const((LANES, d))],
        out_specs=pl.BlockSpec((tm, d_att + d), lambda i: (i, 0)),
        out_shape=jax.ShapeDtypeStruct((n_lat, d_att + d), BF16),
        compiler_params=_cparams(1),
    )(y0, y1, bonus, g, att, p['lnx_w'].reshape(1, d), p['lnx_b'].reshape(1, d), e, e.T)


def _rwkv_mixer(z_rw, att, p, batch, rows_ctx, rows_total):
    outs = _rwkv_prep(z_rw, p, rows_ctx, rows_total)
    g, bonus = outs[9], outs[10]
    mats = _rwkv_chunk(outs[:9])
    y0, y1 = _rwkv_scan(mats, batch, rows_ctx // RW_CHUNK)
    return _rwkv_readout(y0, y1, bonus, g, att, p, rows_ctx, rows_total)


def _rope_swap_cols(w):
    f = ROPE_FREQS
    parts = []
    for a in range(2):
        x1 = w[..., (2 * a) * f:(2 * a + 1) * f]
        x2 = w[..., (2 * a + 1) * f:(2 * a + 2) * f]
        parts += [-x2, x1]
    return jnp.concatenate(parts, axis=-1)


def _rope_tables(l):
    rows = l // GRID_W
    row = jnp.repeat(jnp.arange(rows), GRID_W).astype(F32)
    col = jnp.tile(jnp.arange(GRID_W), rows).astype(F32)
    inv = ROPE_THETA ** (-jnp.arange(ROPE_FREQS, dtype=F32) / ROPE_FREQS)
    ar = row[:, None] * inv
    ac = col[:, None] * inv
    cos = jnp.concatenate([jnp.cos(ar), jnp.cos(ar), jnp.cos(ac), jnp.cos(ac)], axis=1)
    sin = jnp.concatenate([jnp.sin(ar), jnp.sin(ar), jnp.sin(ac), jnp.sin(ac)], axis=1)
    return cos, sin


def _moe(xn, logits, w1, w3, w2):
    n_tok = xn.shape[0]
    tm = MOE_TM
    top_val, top_idx = lax.top_k(logits, TOP_K)
    gate = jax.nn.softmax(top_val, axis=-1)
    n_assign = n_tok * TOP_K
    e_flat = top_idx.reshape(-1).astype(jnp.int32)
    tok_flat = jnp.repeat(jnp.arange(n_tok, dtype=jnp.int32), TOP_K)
    order = jnp.argsort(e_flat)
    e_sorted = e_flat[order]
    counts = jnp.bincount(e_flat, length=N_EXPERTS).astype(jnp.int32)
    padded = (counts + tm - 1) // tm * tm
    pad_end = jnp.cumsum(padded)
    pad_start = pad_end - padded
    grp_start = jnp.cumsum(counts) - counts
    dest = pad_start[e_sorted] + jnp.arange(n_assign, dtype=jnp.int32) - grp_start[e_sorted]
    n_tiles = n_assign // tm + N_EXPERTS
    n_rows = n_tiles * tm
    row_tok = jnp.zeros((n_rows,), jnp.int32).at[dest].set(tok_flat[order])
    pos = jnp.zeros((n_assign,), jnp.int32).at[order].set(dest).reshape(n_tok, TOP_K)
    tile_start = jnp.arange(n_tiles, dtype=jnp.int32) * tm
    n_valid = pad_end[-1] // tm
    tile_src = jnp.minimum(jnp.arange(n_tiles, dtype=jnp.int32), n_valid - 1).astype(jnp.int32)
    tile_exp_all = jnp.minimum(jnp.searchsorted(pad_end, tile_start, side='right'),
                               N_EXPERTS - 1).astype(jnp.int32)
    tile_exp = tile_exp_all[tile_src]
    tile_valid = (tile_start < pad_end[-1]).astype(jnp.int32)
    group = (tile_exp, tile_valid, tile_src)

    xs = jnp.take(xn, row_tok, axis=0)
    hmid = _matmul(xs, [w1, w3], name="moe_up", tm=tm, tn=512, out_dtype=BF16,
                   prologue="unpack", epilogue="swiglu", group=group)
    ys = _matmul(hmid, [w2], name="moe_down", tm=tm, tn=256, out_dtype=F32, group=group)
    y = (jnp.take(ys, pos[:, 0], axis=0) * gate[:, 0:1]
         + jnp.take(ys, pos[:, 1], axis=0) * gate[:, 1:2])
    return y


def kernel(x, c, ctx, c_ctx, l0_ada_w, l0_ada_b, l0_norm1, l0_norm2, l0_w_in, l0_q_norm, l0_w_uq, l0_kv_norm, l0_w_ukv, l0_mu_prev, l0_mu_next, l0_w0, l0_w_up, l0_a0, l0_a_up, l0_g_up, l0_k_k, l0_k_a, l0_r_k, l0_lnx_w, l0_lnx_b, l0_w_o, l0_ffn_w1, l0_ffn_w3, l0_ffn_w2, l1_ada_w, l1_ada_b, l1_norm1, l1_norm2, l1_w_in, l1_v_ln_w, l1_v_ln_b, l1_w_s, l1_b_s, l1_w_o, l1_router, l1_moe_w1, l1_moe_w3, l1_moe_w2, final_norm):
    b, l, d = x.shape
    lc = ctx.shape[1]
    t = lc + l
    n_tok = b * l

    cond = jnp.zeros((8, d), F32).at[:b].set(c).at[b].set(c_ctx)
    mod0 = _matmul(cond, [l0_ada_w], name="ada_mod", tm=8, tn=1536, out_dtype=F32, prologue="silu",
                   epilogue="bias", bias=l0_ada_b)
    mod1 = _matmul(cond, [l1_ada_w], name="ada_mod", tm=8, tn=1536, out_dtype=F32, prologue="silu",
                   epilogue="bias", bias=l1_ada_b)

    def mods(mod, row0, nrows):
        return [mod[row0:row0 + nrows, i * d:(i + 1) * d].reshape(nrows, 1, d) for i in range(N_MOD)]

    sh1, sc1, g1, sh2, sc2, g2 = mods(mod0, 0, b)
    csh1, csc1 = mods(mod0, b, 1)[:2]

    xall = _norm_mod_merge(x, ctx, l0_norm1, sh1, sc1, csh1, csc1)
    mla_in = Q_LORA + KV_LORA + QK_ROPE
    w_kr = l0_w_in[:, Q_LORA + KV_LORA:mla_in]
    w_mla = jnp.concatenate([l0_w_in[:, :mla_in], _rope_swap_cols(w_kr)], axis=1)
    w_rw = jnp.pad(l0_w_in[:, mla_in:], ((0, 0), (0, RWKV_PAD - RWKV_IN)))
    xall2 = xall.reshape(b * t, d)
    z_all = _matmul(xall2, [w_mla], name="in_mla", tm=512, tn=384, out_dtype=F32)
    z_rw = _matmul(xall2, [w_rw], name="in_rwkv", tm=1088, tn=512, out_dtype=F32)

    wq = l0_w_uq.reshape(Q_LORA, MLA_HEADS, QK_NOPE + QK_ROPE)
    wq_ext = jnp.concatenate([wq, _rope_swap_cols(wq[..., QK_NOPE:])], axis=-1)
    wq_ext = wq_ext.reshape(Q_LORA, MLA_HEADS * (QK_NOPE + 2 * QK_ROPE))
    tq_rows = 256
    per_b = l // tq_rows
    lat0 = lc // tq_rows

    def lat_rows(i):
        return (i // per_b) * (t // tq_rows) + lat0 + i % per_b

    q = _matmul(z_all, [wq_ext], name="q_up", tm=tq_rows, tn=512, out_dtype=BF16, k=Q_LORA, x_col_block=0,
                x_row_map=lat_rows, m_out=n_tok, prologue="rms", gain=l0_q_norm)
    kv = _matmul(z_all, [l0_w_ukv], name="kv_up", tm=512, tn=512, out_dtype=BF16, k=KV_LORA, x_col_block=1,
                 prologue="rms", gain=l0_kv_norm)
    cos_l, sin_l = _rope_tables(l)
    cos_t = jnp.concatenate([jnp.ones((lc, QK_ROPE), F32), cos_l], axis=0)
    sin_t = jnp.concatenate([jnp.zeros((lc, QK_ROPE), F32), sin_l], axis=0)
    kr = _krope(z_all, cos_t, sin_t, col_block=(Q_LORA + KV_LORA) // LANES)
    att = _attention(q, kv.reshape(b, t, -1), kr.reshape(b, t, LANES), cos_l, sin_l, b)

    p0 = dict(mu_prev=l0_mu_prev, mu_next=l0_mu_next, w0=l0_w0, w_up=l0_w_up, a0=l0_a0,
              a_up=l0_a_up, g_up=l0_g_up, k_k=l0_k_k, k_a=l0_k_a, r_k=l0_r_k,
              lnx_w=l0_lnx_w, lnx_b=l0_lnx_b)
    mix = _rwkv_mixer(z_rw, att, p0, b, lc, t)
    h = _matmul(mix, [l0_w_o], name="mix_out", tm=1024, tn=512, out_dtype=F32, epilogue="resid",
                resid=x.reshape(n_tok, d), gate=g1, rows_per_gate=l)

    xn = _norm_mod(h.reshape(b, l, d), l0_norm2, sh2, sc2).reshape(n_tok, d)
    hmid = _matmul(xn, [l0_ffn_w1, l0_ffn_w3], name="ffn_up", tm=1024, tn=512, out_dtype=BF16,
                   epilogue="swiglu")
    h = _matmul(hmid, [l0_ffn_w2], name="ffn_down", tm=512, tn=256, out_dtype=F32, epilogue="resid",
                resid=h, gate=g2, rows_per_gate=l)

    sh1, sc1, g1, sh2, sc2, g2 = mods(mod1, 0, b)
    xn = _norm_mod(h.reshape(b, l, d), l1_norm1, sh1, sc1).reshape(n_tok, d)
    hg = _matmul(xn, [l1_w_in], name="gmlp_in", tm=1024, tn=512, out_dtype=BF16, epilogue="gelu")
    gated = _spatial_gate(hg, l1_v_ln_w, l1_v_ln_b, l1_w_s, l1_b_s)
    h = _matmul(gated, [l1_w_o], name="gmlp_out", tm=1024, tn=512, out_dtype=F32, epilogue="resid",
                resid=h, gate=g1, rows_per_gate=l)

    xn, logits = _norm_mod(h.reshape(b, l, d), l1_norm2, sh2, sc2, router=l1_router)
    y = _moe(xn.reshape(n_tok, d // 2), logits.reshape(n_tok, LANES)[:, :N_EXPERTS],
             l1_moe_w1, l1_moe_w3, l1_moe_w2)
    h = h + (g2 * y.reshape(b, l, d)).reshape(n_tok, d)

    return _final_norm(h, final_norm).reshape(b, l, d)
```

```python
import functools

import jax
import jax.numpy as jnp
import numpy as np
from jax import lax
from jax.experimental import pallas as pl
from jax.experimental.pallas import tpu as pltpu

F32 = jnp.float32
BF16 = jnp.bfloat16

D_MODEL = 2048
GRID_W = 64
N_MOD = 6
NORM_EPS = 1e-6
V_HEAD = 128
MLA_HEADS = 8
Q_LORA = 512
KV_LORA = 512
QK_NOPE = 128
QK_ROPE = 64
ROPE_FREQS = 16
ROPE_THETA = 10000.0
RWKV_HEAD = 64
RWKV_HEADS = 16
RWKV_DIM = 1024
DECAY_LORA = 64
ICLR_LORA = 64
GATE_LORA = 160
LNX_EPS = 64e-5
RWKV_IN = 3 * RWKV_DIM + 2 * DECAY_LORA + 2 * ICLR_LORA + GATE_LORA
RWKV_PAD = 3584
RW_CHUNK = 64
PREP_TM = 128
N_PAIRS = RWKV_HEADS // 2
HI = lax.Precision.HIGHEST
CHUNK = 128
SG_GROUPS = 16
D_FF = 7168
N_EXPERTS = 8
TOP_K = 2

VMEM_LIMIT_BYTES = 56 * 1024 * 1024
LANES = 128

MOE_TM = 512


def _cparams(n_axes):
    return pltpu.CompilerParams(
        dimension_semantics=("arbitrary",) * n_axes,
        vmem_limit_bytes=VMEM_LIMIT_BYTES)


def _mm_body(*refs, n_w, prologue, epilogue, grouped):
    refs = list(refs)
    if grouped:
        te_ref, tv_ref, _ = refs[:3]
        refs = refs[3:]
    x_ref = refs.pop(0)
    w_refs = [refs.pop(0) for _ in range(n_w)]
    gain_ref = refs.pop(0) if prologue == "rms" else None
    bias_ref = refs.pop(0) if epilogue == "bias" else None
    if epilogue == "resid":
        resid_ref = refs.pop(0)
        gate_ref = refs.pop(0)
    o_ref, wbf_ref = refs

    i = pl.program_id(1)
    if grouped:
        new_w = jnp.logical_or(i == 0, te_ref[i] != te_ref[jnp.maximum(i - 1, 0)])
    else:
        new_w = i == 0

    @pl.when(new_w)
    def _():
        for n in range(n_w):
            wbf_ref[n] = w_refs[n][...].astype(BF16)

    def compute():
        x = x_ref[...]
        if prologue == "rms":
            xf = x.astype(F32)
            ms = jnp.mean(xf * xf, axis=-1, keepdims=True)
            x = xf * lax.rsqrt(ms + NORM_EPS) * gain_ref[...]
        elif prologue == "silu":
            xf = x.astype(F32)
            x = xf * jax.nn.sigmoid(xf)
        x = x.astype(BF16)
        acc = [jnp.dot(x, wbf_ref[n], preferred_element_type=F32) for n in range(n_w)]
        if epilogue == "swiglu":
            a = acc[0]
            out = a * jax.nn.sigmoid(a) * acc[1]
        elif epilogue == "gelu":
            a = acc[0]
            out = 0.5 * a * (1.0 + lax.erf(a * (2.0 ** -0.5)))
        elif epilogue == "bias":
            out = acc[0] + bias_ref[...]
        elif epilogue == "resid":
            out = resid_ref[...] + gate_ref[0] * acc[0]
        else:
            out = acc[0]
        o_ref[...] = out.astype(o_ref.dtype)

    if grouped:
        @pl.when(tv_ref[i] == 1)
        def _():
            compute()

        @pl.when(tv_ref[i] == 0)
        def _():
            o_ref[...] = jnp.zeros(o_ref.shape, o_ref.dtype)
    else:
        compute()


def _matmul(x, ws, *, name, tm, tn, out_dtype, k=None, x_col_block=0, x_row_map=None, m_out=None,
            prologue=None, gain=None, epilogue=None, bias=None, resid=None, gate=None,
            rows_per_gate=None, group=None, single_buffer_w=False):
    grouped = group is not None
    kdim = k if k is not None else x.shape[1]
    n = ws[0].shape[-1]
    m = m_out if m_out is not None else x.shape[0]
    assert m % tm == 0 and n % tn == 0, (m, tm, n, tn)
    n_w = len(ws)
    grid = (n // tn, m // tm)
    w_mode = dict(pipeline_mode=pl.Buffered(1)) if single_buffer_w else {}

    if grouped:
        def xmap(j, i, te, tv, ts):
            return (ts[i], x_col_block)

        def wmap(j, i, te, tv, ts):
            return (te[i], 0, j)

        def omap(j, i, te, tv, ts):
            return (i, j)
        w_spec = pl.BlockSpec((None, kdim, tn), wmap, **w_mode)
    else:
        def xmap(j, i):
            return ((x_row_map(i) if x_row_map is not None else i), x_col_block)

        def wmap(j, i):
            return (0, j)

        def omap(j, i):
            return (i, j)
        w_spec = pl.BlockSpec((kdim, tn), wmap, **w_mode)

    in_specs = [pl.BlockSpec((tm, kdim), xmap)] + [w_spec] * n_w
    args = [x] + list(ws)
    if prologue == "rms":
        in_specs.append(pl.BlockSpec((1, kdim), lambda j, i, *_: (0, 0)))
        args.append(gain.reshape(1, kdim))
    if epilogue == "bias":
        in_specs.append(pl.BlockSpec((1, tn), lambda j, i, *_: (0, j)))
        args.append(bias.reshape(1, n))
    if epilogue == "resid":
        in_specs.append(pl.BlockSpec((tm, tn), omap))
        args.append(resid)
        assert rows_per_gate % tm == 0
        tiles_per_gate = rows_per_gate // tm
        in_specs.append(pl.BlockSpec((1, 1, tn), lambda j, i, *_: (i // tiles_per_gate, 0, j)))
        args.append(gate)

    body = functools.partial(_mm_body, n_w=n_w, prologue=prologue, epilogue=epilogue,
                             grouped=grouped)
    gs = pltpu.PrefetchScalarGridSpec(
        num_scalar_prefetch=3 if grouped else 0,
        grid=grid,
        in_specs=in_specs,
        out_specs=pl.BlockSpec((tm, tn), omap),
        scratch_shapes=[pltpu.VMEM((n_w, kdim, tn), BF16)])
    call = pl.pallas_call(
        body, grid_spec=gs, name=name,
        out_shape=jax.ShapeDtypeStruct((m, n), out_dtype),
        compiler_params=_cparams(2))
    if grouped:
        return call(*group, *args)
    return call(*args)


def _norm_mod_math(v, g, sh, sc):
    ms = jnp.mean(v * v, axis=-1, keepdims=True)
    y = v * lax.rsqrt(ms + NORM_EPS) * g
    return y * (1.0 + sc) + sh


def _nm_merge_body(x_ref, c_ref, g_ref, sh_ref, sc_ref, csh_ref, csc_ref, o_ref):
    r = pl.program_id(1)

    @pl.when(r == 0)
    def _():
        o_ref[0] = _norm_mod_math(c_ref[0], g_ref[...], csh_ref[...], csc_ref[...]).astype(BF16)

    @pl.when(r > 0)
    def _():
        o_ref[0] = _norm_mod_math(x_ref[0], g_ref[...], sh_ref[0], sc_ref[0]).astype(BF16)


def _norm_mod_merge(x, ctx, gain, sh, sc, csh, csc):
    b, l, d = x.shape
    lc = ctx.shape[1]
    tm = lc
    assert l % tm == 0
    nt = l // tm + 1
    return pl.pallas_call(
        _nm_merge_body, name="norm_mod_merge",
        grid=(b, nt),
        in_specs=[
            pl.BlockSpec((1, tm, d), lambda bi, r: (bi, jnp.maximum(r - 1, 0), 0)),
            pl.BlockSpec((1, lc, d), lambda bi, r: (bi, 0, 0)),
            pl.BlockSpec((1, d), lambda bi, r: (0, 0)),
            pl.BlockSpec((1, 1, d), lambda bi, r: (bi, 0, 0)),
            pl.BlockSpec((1, 1, d), lambda bi, r: (bi, 0, 0)),
            pl.BlockSpec((1, d), lambda bi, r: (0, 0)),
            pl.BlockSpec((1, d), lambda bi, r: (0, 0)),
        ],
        out_specs=pl.BlockSpec((1, tm, d), lambda bi, r: (bi, r, 0)),
        out_shape=jax.ShapeDtypeStruct((b, lc + l, d), BF16),
        compiler_params=_cparams(2),
    )(x, ctx, gain.reshape(1, d), sh, sc, csh.reshape(1, d), csc.reshape(1, d))


def _nm_body(x_ref, g_ref, sh_ref, sc_ref, o_ref):
    o_ref[0] = _norm_mod_math(x_ref[0], g_ref[...], sh_ref[0], sc_ref[0]).astype(BF16)


def _nm_router_body(x_ref, g_ref, sh_ref, sc_ref, r_ref, o_ref, lg_ref):
    y = _norm_mod_math(x_ref[0], g_ref[...], sh_ref[0], sc_ref[0])
    o_ref[0] = y
    lg_ref[0] = jnp.dot(y, r_ref[...], precision=lax.Precision.HIGHEST,
                        preferred_element_type=F32)


def _norm_mod(h, gain, sh, sc, router=None, tm=256):
    b, l, d = h.shape
    assert l % tm == 0
    in_specs = [
        pl.BlockSpec((1, tm, d), lambda bi, r: (bi, r, 0)),
        pl.BlockSpec((1, d), lambda bi, r: (0, 0)),
        pl.BlockSpec((1, 1, d), lambda bi, r: (bi, 0, 0)),
        pl.BlockSpec((1, 1, d), lambda bi, r: (bi, 0, 0)),
    ]
    o_spec = pl.BlockSpec((1, tm, d), lambda bi, r: (bi, r, 0))
    o_shape = jax.ShapeDtypeStruct((b, l, d), BF16)
    if router is None:
        return pl.pallas_call(
            _nm_body, name="norm_mod", grid=(b, l // tm), in_specs=in_specs, out_specs=o_spec,
            out_shape=o_shape, compiler_params=_cparams(2),
        )(h, gain.reshape(1, d), sh, sc)
    ne = router.shape[1]
    router_pad = jnp.pad(router, ((0, 0), (0, LANES - ne)))
    return pl.pallas_call(
        _nm_router_body, name="norm_mod_router", grid=(b, l // tm),
        in_specs=in_specs + [pl.BlockSpec((d, LANES), lambda bi, r: (0, 0))],
        out_specs=[o_spec, pl.BlockSpec((1, tm, LANES), lambda bi, r: (bi, r, 0))],
        out_shape=[jax.ShapeDtypeStruct((b, l, d), F32),
                   jax.ShapeDtypeStruct((b, l, LANES), F32)],
        compiler_params=_cparams(2),
    )(h, gain.reshape(1, d), sh, sc, router_pad)


def _combine_norm_body(h_ref, ya_ref, yb_ref, gt_ref, g2_ref, gain_ref, o_ref):
    gt = gt_ref[...]
    y = ya_ref[...] * gt[:, 0:1] + yb_ref[...] * gt[:, 1:2]
    v = h_ref[...] + g2_ref[0] * y
    ms = jnp.mean(v * v, axis=-1, keepdims=True)
    o_ref[...] = v * lax.rsqrt(ms + NORM_EPS) * gain_ref[...]


def _combine_norm(h, ya, yb, gate, g2, gain, rows_per_gate, tm=256):
    m, d = h.shape
    gt = jnp.pad(gate, ((0, 0), (0, LANES - gate.shape[1])))
    tiles_per_gate = rows_per_gate // tm
    row = pl.BlockSpec((tm, d), lambda i: (i, 0))
    return pl.pallas_call(
        _combine_norm_body, name="combine_norm", grid=(m // tm,),
        in_specs=[row, row, row, pl.BlockSpec((tm, LANES), lambda i: (i, 0)),
                  pl.BlockSpec((1, 1, d), lambda i: (i // tiles_per_gate, 0, 0)),
                  pl.BlockSpec((1, d), lambda i: (0, 0))],
        out_specs=row,
        out_shape=jax.ShapeDtypeStruct((m, d), F32),
        compiler_params=_cparams(1),
    )(h, ya, yb, gt, g2, gain.reshape(1, d))


def _krope_body(z_ref, cos_ref, sin_ref, o_ref):
    z = z_ref[...]
    rot = z[:, :QK_ROPE] * cos_ref[...] + z[:, QK_ROPE:] * sin_ref[...]
    o_ref[...] = jnp.concatenate([rot, jnp.zeros_like(rot)], axis=1).astype(BF16)


def _krope(z_all, cos_t, sin_t, col_block, tm=256):
    m = z_all.shape[0]
    t = cos_t.shape[0]
    tiles_per_batch = t // tm
    return pl.pallas_call(
        _krope_body, name="krope", grid=(m // tm,),
        in_specs=[pl.BlockSpec((tm, LANES), lambda i: (i, col_block)),
                  pl.BlockSpec((tm, QK_ROPE), lambda i: (i % tiles_per_batch, 0)),
                  pl.BlockSpec((tm, QK_ROPE), lambda i: (i % tiles_per_batch, 0))],
        out_specs=pl.BlockSpec((tm, LANES), lambda i: (i, 0)),
        out_shape=jax.ShapeDtypeStruct((m, LANES), BF16),
        compiler_params=_cparams(1),
    )(z_all, cos_t, sin_t)


def _attn_body(q_ref, kn_ref, v_ref, kr_ref, cos_ref, sin_ref, o_ref, kfull_ref, *, scale):
    qi = pl.program_id(2)

    @pl.when(qi == 0)
    def _():
        kfull_ref[:, :QK_NOPE] = kn_ref[0]
        kfull_ref[:, QK_NOPE:] = kr_ref[0]

    q = q_ref[...].astype(F32)
    qn = q[:, :QK_NOPE]
    qr = (q[:, QK_NOPE:QK_NOPE + QK_ROPE] * cos_ref[...]
          + q[:, QK_NOPE + QK_ROPE:] * sin_ref[...])
    qf = (jnp.concatenate([qn, qr, jnp.zeros_like(qr)], axis=1) * scale).astype(BF16)
    s = lax.dot_general(qf, kfull_ref[...], (((1,), (1,)), ((), ())),
                        preferred_element_type=F32)
    m = jnp.max(s, axis=-1, keepdims=True)
    p = jnp.exp2(s - m)
    l = jnp.sum(p, axis=-1, keepdims=True)
    o = jnp.dot(p.astype(BF16), v_ref[0], preferred_element_type=F32)
    o_ref[...] = (o / l).astype(BF16)


def _attention(q, kv, kr, cos_q, sin_q, batch, tq=256):
    m = q.shape[0]
    l = m // batch
    t = kv.shape[1]
    nq = l // tq
    scale = float((QK_NOPE + QK_ROPE) ** -0.5 * np.log2(np.e))
    hw = QK_NOPE + 2 * QK_ROPE
    return pl.pallas_call(
        functools.partial(_attn_body, scale=scale), name="attention",
        grid=(batch, MLA_HEADS, nq),
        in_specs=[
            pl.BlockSpec((tq, hw), lambda b, h, i: (b * nq + i, h)),
            pl.BlockSpec((1, t, QK_NOPE), lambda b, h, i: (b, 0, 2 * h)),
            pl.BlockSpec((1, t, V_HEAD), lambda b, h, i: (b, 0, 2 * h + 1)),
            pl.BlockSpec((1, t, LANES), lambda b, h, i: (b, 0, 0)),
            pl.BlockSpec((tq, QK_ROPE), lambda b, h, i: (i, 0)),
            pl.BlockSpec((tq, QK_ROPE), lambda b, h, i: (i, 0)),
        ],
        out_specs=pl.BlockSpec((tq, V_HEAD), lambda b, h, i: (b * nq + i, h)),
        out_shape=jax.ShapeDtypeStruct((m, MLA_HEADS * V_HEAD), BF16),
        scratch_shapes=[pltpu.VMEM((t, QK_NOPE + LANES), BF16)],
        compiler_params=_cparams(3),
    )(q, kv, kv, kr, cos_q, sin_q)


def _sgu_body(u_ref, v_ref, lw_ref, lb_ref, ws_ref, bs_ref, o_ref, vn_ref):
    v = v_ref[...].astype(F32)
    mu = jnp.mean(v, axis=-1, keepdims=True)
    vc = v - mu
    var = jnp.mean(vc * vc, axis=-1, keepdims=True)
    vn_ref[...] = (vc * lax.rsqrt(var + 1e-5) * lw_ref[...] + lb_ref[...]).astype(BF16)
    n_chunks = v_ref.shape[0] // CHUNK
    for n in range(n_chunks):
        rows = slice(n * CHUNK, (n + 1) * CHUNK)
        for g in range(SG_GROUPS):
            cols = slice(g * LANES, (g + 1) * LANES)
            vm = jnp.dot(ws_ref[g], vn_ref[rows, cols], preferred_element_type=F32)
            vm = vm + bs_ref[:, cols]
            o_ref[rows, cols] = (u_ref[rows, cols].astype(F32) * vm).astype(BF16)


def _spatial_gate(hg, ln_w, ln_b, w_s, b_s, tm=256):
    m = hg.shape[0]
    d = hg.shape[1] // 2
    bs_full = jnp.repeat(b_s.T, d // SG_GROUPS, axis=1)
    return pl.pallas_call(
        _sgu_body, name="spatial_gate", grid=(m // tm,),
        in_specs=[
            pl.BlockSpec((tm, d), lambda i: (i, 0)),
            pl.BlockSpec((tm, d), lambda i: (i, 1)),
            pl.BlockSpec((1, d), lambda i: (0, 0)),
            pl.BlockSpec((1, d), lambda i: (0, 0)),
            pl.BlockSpec((SG_GROUPS, CHUNK, CHUNK), lambda i: (0, 0, 0)),
            pl.BlockSpec((CHUNK, d), lambda i: (0, 0)),
        ],
        out_specs=pl.BlockSpec((tm, d), lambda i: (i, 0)),
        out_shape=jax.ShapeDtypeStruct((m, d), BF16),
        scratch_shapes=[pltpu.VMEM((tm, d), BF16)],
        compiler_params=_cparams(1),
    )(hg, hg, ln_w.reshape(1, d), ln_b.reshape(1, d), w_s.astype(BF16), bs_full)


def _softplus(x):
    return jnp.maximum(x, 0.0) + jnp.log1p(jnp.exp(-jnp.abs(x)))


def _split_bf16(x, pieces):
    out = []
    for _ in range(pieces):
        p = x.astype(BF16)
        out.append(p)
        x = x - p.astype(F32)
    return out


def _dot_exact_rhs(x, w, pieces=2):
    return sum(jnp.dot(p, w, preferred_element_type=F32) for p in _split_bf16(x, pieces))


def _dot_split(x, w_hi, w_lo):
    x_hi, x_lo = _split_bf16(x, 2)
    return (jnp.dot(x_hi, w_hi, preferred_element_type=F32)
            + (jnp.dot(x_hi, w_lo, preferred_element_type=F32)
               + jnp.dot(x_lo, w_hi, preferred_element_type=F32)))


def _hi_lo(w):
    hi = w.astype(BF16)
    return hi, (w - hi.astype(F32)).astype(BF16)


def _prep_body(z_ref, zp_ref, zn_ref, mup_ref, mun_ref, kk_ref, ka_ref, w0_ref, a0_ref,
               wup_ref, aup_ref, gup_ref, rk_ref, e_ref, et_ref,
               lw0_o, lw1_o, k0_o, k1_o, b0_o, b1_o, v_o, kkn_o, r_o, g_o, bonus_o,
               *, tiles_ctx, tiles_total):
    tm = z_ref.shape[0]
    rt = pl.program_id(0) % tiles_total
    in_ctx = rt < tiles_ctx
    has_prev = jnp.where(in_ctx, rt > 0, rt > tiles_ctx)
    has_next = jnp.where(in_ctx, rt < tiles_ctx - 1, rt < tiles_total - 1)

    z = z_ref[...]
    rows = lax.broadcasted_iota(jnp.int32, z.shape, 0)
    prev_row = jnp.where(has_prev, zp_ref[7:8, :], 0.0)
    next_row = jnp.where(has_next, zn_ref[0:1, :], 0.0)
    z_prev = jnp.where(rows == 0, prev_row, pltpu.roll(z, 1, 0))
    z_next = jnp.where(rows == tm - 1, next_row, pltpu.roll(z, tm - 1, 0))
    zs = z + mup_ref[...] * (z_prev - z) + mun_ref[...] * (z_next - z)

    d = RWKV_DIM
    r = zs[:, 0:d]
    k = zs[:, d:2 * d]
    v = zs[:, 2 * d:3 * d]
    wd = zs[:, 3 * d:3 * d + LANES]
    ad = zs[:, 3 * d + LANES:3 * d + 2 * LANES]
    gd = zs[:, 3 * d + 2 * LANES:3 * d + 4 * LANES]

    def head_sum(x):
        return _dot_exact_rhs(_dot_exact_rhs(x, e_ref[...]), et_ref[...])

    kk = k * kk_ref[...]
    nrm = jnp.sqrt(head_sum(kk * kk))
    kkn = kk / jnp.maximum(nrm, 1e-12)

    wx = w0_ref[...] + _dot_split(jnp.tanh(wd), wup_ref[0], wup_ref[1])
    w_log = -_softplus(-wx) - 0.5
    lw = -jnp.exp(w_log)
    ax = a0_ref[...] + _dot_split(ad, aup_ref[0], aup_ref[1])
    iclr = jax.nn.sigmoid(ax)
    g = _dot_split(jax.nn.sigmoid(gd), gup_ref[0], gup_ref[1])

    ka = ka_ref[...]
    k0 = k * (1.0 + (iclr[:, :d] - 1.0) * ka)
    k1 = k * (1.0 + (iclr[:, d:] - 1.0) * ka)
    lw0_o[...] = lw[:, :d]
    lw1_o[...] = lw[:, d:]
    k0_o[...] = k0
    k1_o[...] = k1
    b0_o[...] = iclr[:, :d] * kkn
    b1_o[...] = iclr[:, d:] * kkn
    v_o[...] = v
    kkn_o[...] = kkn
    r_o[...] = r
    g_o[...] = g
    bonus_o[...] = head_sum(r * (k0 + k1) * rk_ref[...]) * v


def _head_indicator():
    head_of = jnp.arange(RWKV_DIM) // RWKV_HEAD
    return (head_of[:, None] == jnp.arange(LANES)[None, :]).astype(F32)


def _rwkv_prep(z_rw, p, rows_ctx, rows_total):
    m, w = z_rw.shape
    tm = PREP_TM
    d = RWKV_DIM
    assert rows_ctx % tm == 0 and rows_total % tm == 0 and m % rows_total == 0
    pad = w - RWKV_IN
    mup = jnp.pad(p['mu_prev'], (0, pad)).reshape(1, w)
    mun = jnp.pad(p['mu_next'], (0, pad)).reshape(1, w)
    zero = jnp.zeros((DECAY_LORA, d), F32)
    wup = jnp.concatenate([jnp.concatenate([p['w_up'][0], zero], axis=1),
                           jnp.concatenate([zero, p['w_up'][1]], axis=1)], axis=0)
    aup = jnp.concatenate([jnp.concatenate([p['a_up'][0], zero], axis=1),
                           jnp.concatenate([zero, p['a_up'][1]], axis=1)], axis=0)
    gup = jnp.pad(p['g_up'], ((0, 2 * LANES - GATE_LORA), (0, 0)))
    wup, aup, gup = (jnp.stack(_hi_lo(w)) for w in (wup, aup, gup))
    e = _head_indicator().astype(BF16)

    def full(shape):
        return pl.BlockSpec(shape, lambda i: (0,) * len(shape))

    n8 = m // 8
    out_spec = pl.BlockSpec((tm, d), lambda i: (i, 0))
    out_sds = jax.ShapeDtypeStruct((m, d), F32)
    return pl.pallas_call(
        functools.partial(_prep_body, tiles_ctx=rows_ctx // tm, tiles_total=rows_total // tm),
        name="rwkv_prep",
        grid=(m // tm,),
        in_specs=[
            pl.BlockSpec((tm, w), lambda i: (i, 0)),
            pl.BlockSpec((8, w), lambda i: (jnp.maximum(i * (tm // 8) - 1, 0), 0)),
            pl.BlockSpec((8, w), lambda i: (jnp.minimum((i + 1) * (tm // 8), n8 - 1), 0)),
            full((1, w)), full((1, w)), full((1, d)), full((1, d)), full((1, 2 * d)),
            full((1, 2 * d)), full((2, LANES, 2 * d)), full((2, LANES, 2 * d)),
            full((2, 2 * LANES, d)), full((1, d)), full((d, LANES)), full((LANES, d)),
        ],
        out_specs=[out_spec] * 11,
        out_shape=[out_sds] * 11,
        compiler_params=_cparams(1),
    )(z_rw, z_rw, z_rw, mup, mun, p['k_k'].reshape(1, d), p['k_a'].reshape(1, d),
      p['w0'].reshape(1, 2 * d), p['a0'].reshape(1, 2 * d), wup, aup, gup,
      p['r_k'].reshape(1, d), e, e.T)


def _bdot(x, y):
    return jnp.dot(x.astype(BF16), y.astype(BF16), preferred_element_type=F32)


def _chunk_units(units):
    c = RW_CHUNK
    row = lax.broadcasted_iota(jnp.int32, (c, LANES), 0)
    lane = lax.broadcasted_iota(jnp.int32, (c, LANES), 1)
    pos = lane % c
    lo = lane < c
    tr = lax.broadcasted_iota(jnp.int32, (c, c), 0)
    tc = lax.broadcasted_iota(jnp.int32, (c, c), 1)
    tri = {False: (tc <= tr).astype(BF16), True: (tc >= tr).astype(BF16)}
    strict = {False: pos < row, True: pos > row}
    incl = {False: pos <= row, True: pos >= row}
    last = {False: c - 1, True: 0}
    rev = [u[6] for u in units]
    nu = range(len(units))

    def sb(x):
        return jnp.concatenate([jnp.where(lo, x, 0.0), jnp.where(lo, 0.0, x)], axis=0)

    def nt(x, y):
        return lax.dot_general(x.astype(BF16), y.astype(BF16), (((1,), (1,)), ((), ())),
                               preferred_element_type=F32)

    def tn(x, y):
        return lax.dot_general(x.astype(BF16), y.astype(BF16), (((0,), (0,)), ((), ())),
                               preferred_element_type=F32)

    def fold(x):
        return jnp.where(lo, x[:c], 0.0) + jnp.where(lo, 0.0, x[c:])

    lw3 = [_split_bf16(units[i][0], 3) for i in nu]
    lc = [sum(jnp.dot(tri[rev[i]], piece, preferred_element_type=F32) for piece in lw3[i])
          for i in nu]
    ltot = [lc[i][last[rev[i]]:last[rev[i]] + 1, :] for i in nu]
    e_neg = [jnp.exp(-lc[i]) for i in nu]
    e_h = [jnp.exp(ltot[i] - lc[i]) for i in nu]
    at = [-units[i][3] * jnp.exp(lc[i] - units[i][0]) for i in nu]
    rt = [units[i][5] * jnp.exp(lc[i]) for i in nu]
    bt = [units[i][2] * e_neg[i] for i in nu]
    kt = [units[i][1] * e_neg[i] for i in nu]
    bh = [units[i][2] * e_h[i] for i in nu]
    kh = [units[i][1] * e_h[i] for i in nu]

    ar = [jnp.concatenate([at[i], rt[i]], axis=0) for i in nu]
    m_b = [nt(ar[i], sb(bt[i])) for i in nu]
    m_k = [nt(ar[i], sb(kt[i])) for i in nu]
    n = [jnp.where(strict[rev[i]], m_b[i][:c], 0.0) for i in nu]
    m_ak = [jnp.where(strict[rev[i]], m_k[i][:c], 0.0) for i in nu]
    m_rb = [jnp.where(incl[rev[i]], m_b[i][c:], 0.0) for i in nu]
    m_rk = [jnp.where(incl[rev[i]], m_k[i][c:], 0.0) for i in nu]
    mv = [_bdot(jnp.concatenate([m_ak[i], m_rk[i]], axis=0), sb(units[i][4])) for i in nu]
    xa = at
    xu = [mv[i][:c] for i in nu]
    n_steps = c.bit_length() - 1
    for step in range(n_steps):
        upd = [_bdot(n[i], jnp.concatenate([sb(xa[i]), sb(xu[i])], axis=1)) for i in nu]
        xa = [xa[i] + upd[i][:, :LANES] for i in nu]
        xu = [xu[i] + upd[i][:, LANES:] for i in nu]
        if step < n_steps - 1:
            n = [_bdot(n[i], sb(n[i])) for i in nu]
    rx = [_bdot(m_rb[i], jnp.concatenate([sb(xa[i]), sb(xu[i])], axis=1)) for i in nu]
    gh = [tn(bh[i], jnp.concatenate([xa[i], xu[i]], axis=1)) for i in nu]
    hk = [tn(kh[i], units[i][4]) for i in nu]
    out = []
    for i in nu:
        rp = rt[i] + rx[i][:, :LANES]
        y0 = rx[i][:, LANES:] + mv[i][c:]
        g = fold(gh[i][:, :LANES]) + jnp.where(pos == row, jnp.exp(ltot[i]), 0.0)
        h = fold(gh[i][:, LANES:] + hk[i])
        out.append((g, h, rp, y0))
    return out


def _chunk_body(lw0_ref, lw1_ref, k0_ref, k1_ref, b0_ref, b1_ref, v_ref, kkn_ref, r_ref,
                g0_o, h0_o, rp0_o, y00_o, g1_o, h1_o, rp1_o, y01_o):
    n_chunks = v_ref.shape[0] // RW_CHUNK
    units = []
    for ci in range(n_chunks):
        rows = slice(ci * RW_CHUNK, (ci + 1) * RW_CHUNK)
        v = v_ref[rows, :]
        kkn = kkn_ref[rows, :]
        r = r_ref[rows, :]
        units.append((lw0_ref[rows, :], k0_ref[rows, :], b0_ref[rows, :], kkn, v, r, False))
        units.append((lw1_ref[rows, :], k1_ref[rows, :], b1_ref[rows, :], kkn, v, r, True))
    res = _chunk_units(units)
    for ci in range(n_chunks):
        rows = slice(ci * RW_CHUNK, (ci + 1) * RW_CHUNK)
        for dr, outs in enumerate(((g0_o, h0_o, rp0_o, y00_o), (g1_o, h1_o, rp1_o, y01_o))):
            for o_ref, val in zip(outs, res[2 * ci + dr]):
                o_ref[rows, :] = val


def _rwkv_chunk(prep, tm=512):
    lw0, lw1, k0, k1, b0, b1, v, kkn, r = prep
    m, d = v.shape
    spec = pl.BlockSpec((tm, LANES), lambda i, pr: (i, pr))
    sds = jax.ShapeDtypeStruct((m, d), F32)
    return pl.pallas_call(
        _chunk_body, name="rwkv_chunk", grid=(m // tm, d // LANES),
        in_specs=[spec] * 9, out_specs=[spec] * 8, out_shape=[sds] * 8,
        compiler_params=_cparams(2),
    )(lw0, lw1, k0, k1, b0, b1, v, kkn, r)


def _scan_body(gf_ref, hf_ref, rpf_ref, y0f_ref, gb_ref, hb_ref, rpb_ref, y0b_ref,
               yf_ref, yb_ref, s_ref):
    c = RW_CHUNK
    step = pl.program_id(1)

    @pl.when(step == 0)
    def _():
        s_ref[...] = jnp.zeros(s_ref.shape, F32)

    lane = lax.broadcasted_iota(jnp.int32, (c, LANES), 1)
    lo = lane < c

    def sb(x):
        return jnp.concatenate([jnp.where(lo, x, 0.0), jnp.where(lo, 0.0, x)], axis=0)

    dirs = ((gf_ref, hf_ref, rpf_ref, y0f_ref, yf_ref), (gb_ref, hb_ref, rpb_ref, y0b_ref, yb_ref))
    cols = [slice(pr * LANES, (pr + 1) * LANES) for pr in range(N_PAIRS)]
    out = []
    for dr, (g_ref, _, rp_ref, _, _) in enumerate(dirs):
        lhs = [jnp.concatenate([rp_ref[:, cl], sb(g_ref[:, cl])], axis=0) for cl in cols]
        out.append([_bdot(lhs[pr], s_ref[dr, pr]) for pr in range(N_PAIRS)])
    for dr, (_, h_ref, _, y0_ref, y_ref) in enumerate(dirs):
        for pr, cl in enumerate(cols):
            y_ref[:, cl] = out[dr][pr][:c] + y0_ref[:, cl]
            s_ref[dr, pr] = out[dr][pr][c:] + sb(h_ref[:, cl])


def _rwkv_scan(mats, batch, chunks_ctx):
    m, d = mats[0].shape
    c = RW_CHUNK
    n_chunks = m // c // batch

    def chunk_of(b, s, reverse):
        if not reverse:
            return b * n_chunks + s
        rev = jnp.where(s < chunks_ctx, chunks_ctx - 1 - s, n_chunks - 1 - (s - chunks_ctx))
        return b * n_chunks + rev

    fwd = pl.BlockSpec((c, d), lambda b, s: (chunk_of(b, s, False), 0))
    bwd = pl.BlockSpec((c, d), lambda b, s: (chunk_of(b, s, True), 0))
    sds = jax.ShapeDtypeStruct((m, d), F32)
    return pl.pallas_call(
        _scan_body, name="rwkv_scan", grid=(batch, n_chunks),
        in_specs=[fwd] * 4 + [bwd] * 4, out_specs=[fwd, bwd], out_shape=[sds, sds],
        scratch_shapes=[pltpu.VMEM((2, N_PAIRS, LANES, LANES), F32)],
        compiler_params=_cparams(2),
    )(*mats)


def _readout_body(y0_ref, y1_ref, bonus_ref, g_ref, att_ref, lw_ref, lb_ref, e_ref, et_ref, o_ref):
    def head_mean(x):
        return _dot_exact_rhs(_dot_exact_rhs(x, e_ref[...]), et_ref[...]) * (1.0 / RWKV_HEAD)

    y = y0_ref[...] + y1_ref[...]
    mu = head_mean(y)
    yc = y - mu
    var = head_mean(yc * yc)
    yn = yc * lax.rsqrt(var + LNX_EPS) * lw_ref[...] + lb_ref[...]
    d_att = att_ref.shape[1]
    o_ref[:, :d_att] = att_ref[...]
    o_ref[:, d_att:] = ((yn + bonus_ref[...]) * g_ref[...]).astype(BF16)


def _rwkv_readout(y0, y1, bonus, g, att, p, rows_ctx, rows_total, tm=256):
    d = RWKV_DIM
    n_lat, d_att = att.shape
    assert rows_ctx % tm == 0 and rows_total % tm == 0
    per_b = (rows_total - rows_ctx) // tm
    lat0 = rows_ctx // tm
    tot = rows_total // tm
    e = _head_indicator().astype(BF16)
    rw_spec = pl.BlockSpec((tm, d), lambda i: ((i // per_b) * tot + lat0 + i % per_b, 0))

    def const(shape):
        return pl.BlockSpec(shape, lambda i: (0, 0))

    return pl.pallas_call(
        _readout_body, name="rwkv_readout", grid=(n_lat // tm,),
        in_specs=[rw_spec, rw_spec, rw_spec, rw_spec,
                  pl.BlockSpec((tm, d_att), lambda i: (i, 0)),
                  const((1, d)), const((1, d)), const((d, LANES)), const((LANES, d))],
        out_specs=pl.BlockSpec((tm, d_att + d), lambda i: (i, 0)),
        out_shape=jax.ShapeDtypeStruct((n_lat, d_att + d), BF16),
        compiler_params=_cparams(1),
    )(y0, y1, bonus, g, att, p['lnx_w'].reshape(1, d), p['lnx_b'].reshape(1, d), e, e.T)


def _rwkv_mixer(z_rw, att, p, batch, rows_ctx, rows_total):
    outs = _rwkv_prep(z_rw, p, rows_ctx, rows_total)
    g, bonus = outs[9], outs[10]
    mats = _rwkv_chunk(outs[:9])
    y0, y1 = _rwkv_scan(mats, batch, rows_ctx // RW_CHUNK)
    return _rwkv_readout(y0, y1, bonus, g, att, p, rows_ctx, rows_total)


def _rope_swap_cols(w):
    f = ROPE_FREQS
    parts = []
    for a in range(2):
        x1 = w[..., (2 * a) * f:(2 * a + 1) * f]
        x2 = w[..., (2 * a + 1) * f:(2 * a + 2) * f]
        parts += [-x2, x1]
    return jnp.concatenate(parts, axis=-1)


def _rope_tables(l):
    pos = np.arange(l)
    inv = ROPE_THETA ** (-np.arange(ROPE_FREQS, dtype=np.float64) / ROPE_FREQS)
    ar = (pos // GRID_W)[:, None] * inv
    ac = (pos % GRID_W)[:, None] * inv
    cos = np.concatenate([np.cos(ar), np.cos(ar), np.cos(ac), np.cos(ac)], axis=1)
    sin = np.concatenate([np.sin(ar), np.sin(ar), np.sin(ac), np.sin(ac)], axis=1)
    return cos.astype(np.float32), sin.astype(np.float32)


def _to_bf16_body(x_ref, o_ref):
    o_ref[...] = x_ref[...].astype(BF16)


def _to_bf16(x, tm=512):
    m, d = x.shape
    spec = pl.BlockSpec((tm, d), lambda i: (i, 0))
    return pl.pallas_call(
        _to_bf16_body, name="to_bf16", grid=(m // tm,), in_specs=[spec], out_specs=spec,
        out_shape=jax.ShapeDtypeStruct((m, d), BF16), compiler_params=_cparams(1),
    )(x)


def _moe(xn, logits, w1, w3, w2):
    n_tok = xn.shape[0]
    tm = MOE_TM
    top_val, top_idx = lax.top_k(logits, TOP_K)
    gate = jax.nn.softmax(top_val, axis=-1)
    n_assign = n_tok * TOP_K
    e_flat = top_idx.reshape(-1).astype(jnp.int32)
    experts = jnp.arange(N_EXPERTS, dtype=jnp.int32)
    order = jnp.argsort(e_flat).astype(jnp.int32)
    rank = jnp.argsort(order).astype(jnp.int32)
    counts = jnp.sum((e_flat[:, None] == experts[None, :]).astype(jnp.int32), axis=0)
    padded = (counts + tm - 1) // tm * tm
    pad_end = jnp.cumsum(padded)
    pad_start = pad_end - padded
    grp_start = jnp.cumsum(counts) - counts
    pos = (pad_start[e_flat] + rank - grp_start[e_flat]).reshape(n_tok, TOP_K)
    n_tiles = n_assign // tm + N_EXPERTS
    n_rows = n_tiles * tm

    def expert_of(row):
        return jnp.minimum(jnp.sum((row[:, None] >= pad_end[None, :]).astype(jnp.int32), axis=1),
                           N_EXPERTS - 1)

    rows = jnp.arange(n_rows, dtype=jnp.int32)
    e_row = expert_of(rows)
    off = rows - pad_start[e_row]
    src = jnp.clip(grp_start[e_row] + off, 0, n_assign - 1)
    row_tok = jnp.where(off < counts[e_row], order[src] // TOP_K, 0)
    tile_start = jnp.arange(n_tiles, dtype=jnp.int32) * tm
    n_valid = pad_end[-1] // tm
    tile_src = jnp.minimum(jnp.arange(n_tiles, dtype=jnp.int32), n_valid - 1).astype(jnp.int32)
    tile_exp = expert_of(tile_start)[tile_src]
    tile_valid = (tile_start < pad_end[-1]).astype(jnp.int32)
    group = (tile_exp, tile_valid, tile_src)

    def rows_of(a, idx):
        return a.at[idx].get(mode="promise_in_bounds")

    xs = _to_bf16(rows_of(xn, row_tok))
    hmid = _matmul(xs, [w1, w3], name="moe_up", tm=tm, tn=512, out_dtype=BF16,
                   epilogue="swiglu", group=group)
    ys = _matmul(hmid, [w2], name="moe_down", tm=tm, tn=512, out_dtype=F32, group=group,
                 single_buffer_w=True)
    return rows_of(ys, pos[:, 0]), rows_of(ys, pos[:, 1]), gate


def kernel(x, c, ctx, c_ctx, l0_ada_w, l0_ada_b, l0_norm1, l0_norm2, l0_w_in, l0_q_norm, l0_w_uq, l0_kv_norm, l0_w_ukv, l0_mu_prev, l0_mu_next, l0_w0, l0_w_up, l0_a0, l0_a_up, l0_g_up, l0_k_k, l0_k_a, l0_r_k, l0_lnx_w, l0_lnx_b, l0_w_o, l0_ffn_w1, l0_ffn_w3, l0_ffn_w2, l1_ada_w, l1_ada_b, l1_norm1, l1_norm2, l1_w_in, l1_v_ln_w, l1_v_ln_b, l1_w_s, l1_b_s, l1_w_o, l1_router, l1_moe_w1, l1_moe_w3, l1_moe_w2, final_norm):
    b, l, d = x.shape
    lc = ctx.shape[1]
    t = lc + l
    n_tok = b * l

    cond = jnp.zeros((8, d), F32).at[:b].set(c).at[b].set(c_ctx)
    mod0 = _matmul(cond, [l0_ada_w], name="ada_mod", tm=8, tn=1536, out_dtype=F32, prologue="silu",
                   epilogue="bias", bias=l0_ada_b)
    mod1 = _matmul(cond, [l1_ada_w], name="ada_mod", tm=8, tn=1536, out_dtype=F32, prologue="silu",
                   epilogue="bias", bias=l1_ada_b)

    def mods(mod, row0, nrows):
        return [mod[row0:row0 + nrows, i * d:(i + 1) * d].reshape(nrows, 1, d) for i in range(N_MOD)]

    sh1, sc1, g1, sh2, sc2, g2 = mods(mod0, 0, b)
    csh1, csc1 = mods(mod0, b, 1)[:2]

    xall = _norm_mod_merge(x, ctx, l0_norm1, sh1, sc1, csh1, csc1)
    mla_in = Q_LORA + KV_LORA + QK_ROPE
    w_kr = l0_w_in[:, Q_LORA + KV_LORA:mla_in]
    w_mla = jnp.concatenate([l0_w_in[:, :mla_in], _rope_swap_cols(w_kr)], axis=1)
    w_rw = jnp.pad(l0_w_in[:, mla_in:], ((0, 0), (0, RWKV_PAD - RWKV_IN)))
    xall2 = xall.reshape(b * t, d)
    z_all = _matmul(xall2, [w_mla], name="in_mla", tm=1088, tn=384, out_dtype=F32)
    z_rw = _matmul(xall2, [w_rw], name="in_rwkv", tm=1088, tn=512, out_dtype=F32)

    wq = l0_w_uq.reshape(Q_LORA, MLA_HEADS, QK_NOPE + QK_ROPE)
    wq_ext = jnp.concatenate([wq, _rope_swap_cols(wq[..., QK_NOPE:])], axis=-1)
    wq_ext = wq_ext.reshape(Q_LORA, MLA_HEADS * (QK_NOPE + 2 * QK_ROPE))
    tq_rows = 256
    per_b = l // tq_rows
    lat0 = lc // tq_rows

    def lat_rows(i):
        return (i // per_b) * (t // tq_rows) + lat0 + i % per_b

    q = _matmul(z_all, [wq_ext], name="q_up", tm=tq_rows, tn=2048, out_dtype=BF16, k=Q_LORA, x_col_block=0,
                x_row_map=lat_rows, m_out=n_tok, prologue="rms", gain=l0_q_norm)
    kv = _matmul(z_all, [l0_w_ukv], name="kv_up", tm=1088, tn=1024, out_dtype=BF16, k=KV_LORA, x_col_block=1,
                 prologue="rms", gain=l0_kv_norm)
    cos_l, sin_l = _rope_tables(l)
    cos_t = np.concatenate([np.ones((lc, QK_ROPE), np.float32), cos_l], axis=0)
    sin_t = np.concatenate([np.zeros((lc, QK_ROPE), np.float32), sin_l], axis=0)
    kr = _krope(z_all, cos_t, sin_t, col_block=(Q_LORA + KV_LORA) // LANES)
    att = _attention(q, kv.reshape(b, t, -1), kr.reshape(b, t, LANES), cos_l, sin_l, b)

    p0 = dict(mu_prev=l0_mu_prev, mu_next=l0_mu_next, w0=l0_w0, w_up=l0_w_up, a0=l0_a0,
              a_up=l0_a_up, g_up=l0_g_up, k_k=l0_k_k, k_a=l0_k_a, r_k=l0_r_k,
              lnx_w=l0_lnx_w, lnx_b=l0_lnx_b)
    mix = _rwkv_mixer(z_rw, att, p0, b, lc, t)
    h = _matmul(mix, [l0_w_o], name="mix_out", tm=1024, tn=512, out_dtype=F32, epilogue="resid",
                resid=x.reshape(n_tok, d), gate=g1, rows_per_gate=l)

    xn = _norm_mod(h.reshape(b, l, d), l0_norm2, sh2, sc2).reshape(n_tok, d)
    hmid = _matmul(xn, [l0_ffn_w1, l0_ffn_w3], name="ffn_up", tm=1024, tn=512, out_dtype=BF16,
                   epilogue="swiglu")
    h = _matmul(hmid, [l0_ffn_w2], name="ffn_down", tm=512, tn=512, out_dtype=F32, epilogue="resid",
                resid=h, gate=g2, rows_per_gate=l, single_buffer_w=True)

    sh1, sc1, g1, sh2, sc2, g2 = mods(mod1, 0, b)
    xn = _norm_mod(h.reshape(b, l, d), l1_norm1, sh1, sc1).reshape(n_tok, d)
    hg = _matmul(xn, [l1_w_in], name="gmlp_in", tm=1024, tn=512, out_dtype=BF16, epilogue="gelu")
    gated = _spatial_gate(hg, l1_v_ln_w, l1_v_ln_b, l1_w_s, l1_b_s)
    h = _matmul(gated, [l1_w_o], name="gmlp_out", tm=1024, tn=512, out_dtype=F32, epilogue="resid",
                resid=h, gate=g1, rows_per_gate=l)

    xn, logits = _norm_mod(h.reshape(b, l, d), l1_norm2, sh2, sc2, router=l1_router)
    ya, yb, gate = _moe(xn.reshape(n_tok, d), logits.reshape(n_tok, LANES)[:, :N_EXPERTS],
                        l1_moe_w1, l1_moe_w3, l1_moe_w2)
    return _combine_norm(h, ya, yb, gate, g2, final_norm, l).reshape(b, l, d)
```

```python
import functools

import jax
import jax.numpy as jnp
import numpy as np
from jax import lax
from jax.experimental import pallas as pl
from jax.experimental.pallas import tpu as pltpu

F32 = jnp.float32
BF16 = jnp.bfloat16

D_MODEL = 2048
GRID_W = 64
N_MOD = 6
NORM_EPS = 1e-6
V_HEAD = 128
MLA_HEADS = 8
Q_LORA = 512
KV_LORA = 512
QK_NOPE = 128
QK_ROPE = 64
ROPE_FREQS = 16
ROPE_THETA = 10000.0
RWKV_HEAD = 64
RWKV_HEADS = 16
RWKV_DIM = 1024
DECAY_LORA = 64
ICLR_LORA = 64
GATE_LORA = 160
LNX_EPS = 64e-5
RWKV_IN = 3 * RWKV_DIM + 2 * DECAY_LORA + 2 * ICLR_LORA + GATE_LORA
RWKV_PAD = 3584
RW_CHUNK = 64
PREP_TM = 128
N_PAIRS = RWKV_HEADS // 2
HI = lax.Precision.HIGHEST
CHUNK = 128
SG_GROUPS = 16
D_FF = 7168
N_EXPERTS = 8
TOP_K = 2

VMEM_LIMIT_BYTES = 56 * 1024 * 1024
LANES = 128

MOE_TM = 512


def _cparams(n_axes):
    return pltpu.CompilerParams(
        dimension_semantics=("arbitrary",) * n_axes,
        vmem_limit_bytes=VMEM_LIMIT_BYTES)


def _mm_body(*refs, n_w, prologue, epilogue, grouped):
    refs = list(refs)
    if grouped:
        te_ref, tv_ref, _ = refs[:3]
        refs = refs[3:]
    x_ref = refs.pop(0)
    w_refs = [refs.pop(0) for _ in range(n_w)]
    gain_ref = refs.pop(0) if prologue == "rms" else None
    bias_ref = refs.pop(0) if epilogue == "bias" else None
    if epilogue == "resid":
        resid_ref = refs.pop(0)
        gate_ref = refs.pop(0)
    o_ref, wbf_ref = refs

    i = pl.program_id(1)
    if grouped:
        new_w = jnp.logical_or(i == 0, te_ref[i] != te_ref[jnp.maximum(i - 1, 0)])
    else:
        new_w = i == 0

    @pl.when(new_w)
    def _():
        for n in range(n_w):
            wbf_ref[n] = w_refs[n][...].astype(BF16)

    def compute():
        x = x_ref[...]
        if prologue == "rms":
            xf = x.astype(F32)
            ms = jnp.mean(xf * xf, axis=-1, keepdims=True)
            x = xf * lax.rsqrt(ms + NORM_EPS) * gain_ref[...]
        elif prologue == "silu":
            xf = x.astype(F32)
            x = xf * jax.nn.sigmoid(xf)
        x = x.astype(BF16)
        acc = [jnp.dot(x, wbf_ref[n], preferred_element_type=F32) for n in range(n_w)]
        if epilogue == "swiglu":
            a = acc[0]
            out = a * jax.nn.sigmoid(a) * acc[1]
        elif epilogue == "gelu":
            a = acc[0]
            out = 0.5 * a * (1.0 + lax.erf(a * (2.0 ** -0.5)))
        elif epilogue == "bias":
            out = acc[0] + bias_ref[...]
        elif epilogue == "resid":
            out = resid_ref[...] + gate_ref[0] * acc[0]
        else:
            out = acc[0]
        o_ref[...] = out.astype(o_ref.dtype)

    if grouped:
        @pl.when(tv_ref[i] == 1)
        def _():
            compute()

        @pl.when(tv_ref[i] == 0)
        def _():
            o_ref[...] = jnp.zeros(o_ref.shape, o_ref.dtype)
    else:
        compute()


def _matmul(x, ws, *, name, tm, tn, out_dtype, k=None, x_col_block=0, x_row_map=None, m_out=None,
            prologue=None, gain=None, epilogue=None, bias=None, resid=None, gate=None,
            rows_per_gate=None, group=None, single_buffer_w=False):
    grouped = group is not None
    kdim = k if k is not None else x.shape[1]
    n = ws[0].shape[-1]
    m = m_out if m_out is not None else x.shape[0]
    assert m % tm == 0 and n % tn == 0, (m, tm, n, tn)
    n_w = len(ws)
    grid = (n // tn, m // tm)
    w_mode = dict(pipeline_mode=pl.Buffered(1)) if single_buffer_w else {}

    if grouped:
        def xmap(j, i, te, tv, ts):
            return (ts[i], x_col_block)

        def wmap(j, i, te, tv, ts):
            return (te[i], 0, j)

        def omap(j, i, te, tv, ts):
            return (i, j)
        w_spec = pl.BlockSpec((None, kdim, tn), wmap, **w_mode)
    else:
        def xmap(j, i):
            return ((x_row_map(i) if x_row_map is not None else i), x_col_block)

        def wmap(j, i):
            return (0, j)

        def omap(j, i):
            return (i, j)
        w_spec = pl.BlockSpec((kdim, tn), wmap, **w_mode)

    in_specs = [pl.BlockSpec((tm, kdim), xmap)] + [w_spec] * n_w
    args = [x] + list(ws)
    if prologue == "rms":
        in_specs.append(pl.BlockSpec((1, kdim), lambda j, i, *_: (0, 0)))
        args.append(gain.reshape(1, kdim))
    if epilogue == "bias":
        in_specs.append(pl.BlockSpec((1, tn), lambda j, i, *_: (0, j)))
        args.append(bias.reshape(1, n))
    if epilogue == "resid":
        in_specs.append(pl.BlockSpec((tm, tn), omap))
        args.append(resid)
        assert rows_per_gate % tm == 0
        tiles_per_gate = rows_per_gate // tm
        in_specs.append(pl.BlockSpec((1, 1, tn), lambda j, i, *_: (i // tiles_per_gate, 0, j)))
        args.append(gate)

    body = functools.partial(_mm_body, n_w=n_w, prologue=prologue, epilogue=epilogue,
                             grouped=grouped)
    gs = pltpu.PrefetchScalarGridSpec(
        num_scalar_prefetch=3 if grouped else 0,
        grid=grid,
        in_specs=in_specs,
        out_specs=pl.BlockSpec((tm, tn), omap),
        scratch_shapes=[pltpu.VMEM((n_w, kdim, tn), BF16)])
    call = pl.pallas_call(
        body, grid_spec=gs, name=name,
        out_shape=jax.ShapeDtypeStruct((m, n), out_dtype),
        compiler_params=_cparams(2))
    if grouped:
        return call(*group, *args)
    return call(*args)


def _norm_mod_math(v, g, sh, sc):
    ms = jnp.mean(v * v, axis=-1, keepdims=True)
    y = v * lax.rsqrt(ms + NORM_EPS) * g
    return y * (1.0 + sc) + sh


def _nm_merge_body(x_ref, c_ref, g_ref, sh_ref, sc_ref, csh_ref, csc_ref, o_ref):
    r = pl.program_id(1)

    @pl.when(r == 0)
    def _():
        o_ref[0] = _norm_mod_math(c_ref[0], g_ref[...], csh_ref[...], csc_ref[...]).astype(BF16)

    @pl.when(r > 0)
    def _():
        o_ref[0] = _norm_mod_math(x_ref[0], g_ref[...], sh_ref[0], sc_ref[0]).astype(BF16)


def _norm_mod_merge(x, ctx, gain, sh, sc, csh, csc):
    b, l, d = x.shape
    lc = ctx.shape[1]
    tm = lc
    assert l % tm == 0
    nt = l // tm + 1
    return pl.pallas_call(
        _nm_merge_body, name="norm_mod_merge",
        grid=(b, nt),
        in_specs=[
            pl.BlockSpec((1, tm, d), lambda bi, r: (bi, jnp.maximum(r - 1, 0), 0)),
            pl.BlockSpec((1, lc, d), lambda bi, r: (bi, 0, 0)),
            pl.BlockSpec((1, d), lambda bi, r: (0, 0)),
            pl.BlockSpec((1, 1, d), lambda bi, r: (bi, 0, 0)),
            pl.BlockSpec((1, 1, d), lambda bi, r: (bi, 0, 0)),
            pl.BlockSpec((1, d), lambda bi, r: (0, 0)),
            pl.BlockSpec((1, d), lambda bi, r: (0, 0)),
        ],
        out_specs=pl.BlockSpec((1, tm, d), lambda bi, r: (bi, r, 0)),
        out_shape=jax.ShapeDtypeStruct((b, lc + l, d), BF16),
        compiler_params=_cparams(2),
    )(x, ctx, gain.reshape(1, d), sh, sc, csh.reshape(1, d), csc.reshape(1, d))


def _nm_body(x_ref, g_ref, sh_ref, sc_ref, o_ref):
    o_ref[0] = _norm_mod_math(x_ref[0], g_ref[...], sh_ref[0], sc_ref[0]).astype(BF16)


def _nm_router_body(x_ref, g_ref, sh_ref, sc_ref, r_ref, o_ref, lg_ref):
    y = _norm_mod_math(x_ref[0], g_ref[...], sh_ref[0], sc_ref[0])
    o_ref[0] = y
    lg_ref[0] = jnp.dot(y, r_ref[...], precision=lax.Precision.HIGHEST,
                        preferred_element_type=F32)


def _norm_mod(h, gain, sh, sc, router=None, tm=256):
    b, l, d = h.shape
    assert l % tm == 0
    in_specs = [
        pl.BlockSpec((1, tm, d), lambda bi, r: (bi, r, 0)),
        pl.BlockSpec((1, d), lambda bi, r: (0, 0)),
        pl.BlockSpec((1, 1, d), lambda bi, r: (bi, 0, 0)),
        pl.BlockSpec((1, 1, d), lambda bi, r: (bi, 0, 0)),
    ]
    o_spec = pl.BlockSpec((1, tm, d), lambda bi, r: (bi, r, 0))
    o_shape = jax.ShapeDtypeStruct((b, l, d), BF16)
    if router is None:
        return pl.pallas_call(
            _nm_body, name="norm_mod", grid=(b, l // tm), in_specs=in_specs, out_specs=o_spec,
            out_shape=o_shape, compiler_params=_cparams(2),
        )(h, gain.reshape(1, d), sh, sc)
    ne = router.shape[1]
    router_pad = jnp.pad(router, ((0, 0), (0, LANES - ne)))
    return pl.pallas_call(
        _nm_router_body, name="norm_mod_router", grid=(b, l // tm),
        in_specs=in_specs + [pl.BlockSpec((d, LANES), lambda bi, r: (0, 0))],
        out_specs=[o_spec, pl.BlockSpec((1, tm, LANES), lambda bi, r: (bi, r, 0))],
        out_shape=[jax.ShapeDtypeStruct((b, l, d), F32),
                   jax.ShapeDtypeStruct((b, l, LANES), F32)],
        compiler_params=_cparams(2),
    )(h, gain.reshape(1, d), sh, sc, router_pad)


def _combine_norm_body(h_ref, ya_ref, yb_ref, gt_ref, g2_ref, gain_ref, o_ref):
    gt = gt_ref[...]
    y = ya_ref[...] * gt[:, 0:1] + yb_ref[...] * gt[:, 1:2]
    v = h_ref[...] + g2_ref[0] * y
    ms = jnp.mean(v * v, axis=-1, keepdims=True)
    o_ref[...] = v * lax.rsqrt(ms + NORM_EPS) * gain_ref[...]


def _combine_norm(h, ya, yb, gate, g2, gain, rows_per_gate, tm=256):
    m, d = h.shape
    gt = jnp.pad(gate, ((0, 0), (0, LANES - gate.shape[1])))
    tiles_per_gate = rows_per_gate // tm
    row = pl.BlockSpec((tm, d), lambda i: (i, 0))
    return pl.pallas_call(
        _combine_norm_body, name="combine_norm", grid=(m // tm,),
        in_specs=[row, row, row, pl.BlockSpec((tm, LANES), lambda i: (i, 0)),
                  pl.BlockSpec((1, 1, d), lambda i: (i // tiles_per_gate, 0, 0)),
                  pl.BlockSpec((1, d), lambda i: (0, 0))],
        out_specs=row,
        out_shape=jax.ShapeDtypeStruct((m, d), F32),
        compiler_params=_cparams(1),
    )(h, ya, yb, gt, g2, gain.reshape(1, d))


def _krope_body(z_ref, cos_ref, sin_ref, o_ref):
    z = z_ref[...]
    rot = z[:, :QK_ROPE] * cos_ref[...] + z[:, QK_ROPE:] * sin_ref[...]
    o_ref[...] = jnp.concatenate([rot, jnp.zeros_like(rot)], axis=1).astype(BF16)


def _krope(z_all, cos_t, sin_t, col_block, tm=256):
    m = z_all.shape[0]
    t = cos_t.shape[0]
    tiles_per_batch = t // tm
    return pl.pallas_call(
        _krope_body, name="krope", grid=(m // tm,),
        in_specs=[pl.BlockSpec((tm, LANES), lambda i: (i, col_block)),
                  pl.BlockSpec((tm, QK_ROPE), lambda i: (i % tiles_per_batch, 0)),
                  pl.BlockSpec((tm, QK_ROPE), lambda i: (i % tiles_per_batch, 0))],
        out_specs=pl.BlockSpec((tm, LANES), lambda i: (i, 0)),
        out_shape=jax.ShapeDtypeStruct((m, LANES), BF16),
        compiler_params=_cparams(1),
    )(z_all, cos_t, sin_t)


def _attn_body(q_ref, kn_ref, v_ref, kr_ref, cos_ref, sin_ref, o_ref, kfull_ref, *, scale):
    qi = pl.program_id(2)

    @pl.when(qi == 0)
    def _():
        kfull_ref[:, :QK_NOPE] = kn_ref[0]
        kfull_ref[:, QK_NOPE:] = kr_ref[0]

    q = q_ref[...].astype(F32)
    qn = q[:, :QK_NOPE]
    qr = (q[:, QK_NOPE:QK_NOPE + QK_ROPE] * cos_ref[...]
          + q[:, QK_NOPE + QK_ROPE:] * sin_ref[...])
    qf = (jnp.concatenate([qn, qr, jnp.zeros_like(qr)], axis=1) * scale).astype(BF16)
    s = lax.dot_general(qf, kfull_ref[...], (((1,), (1,)), ((), ())),
                        preferred_element_type=F32)
    m = jnp.max(s, axis=-1, keepdims=True)
    p = jnp.exp2(s - m)
    l = jnp.sum(p, axis=-1, keepdims=True)
    o = jnp.dot(p.astype(BF16), v_ref[0], preferred_element_type=F32)
    o_ref[...] = (o / l).astype(BF16)


def _attention(q, kv, kr, cos_q, sin_q, batch, tq=256):
    m = q.shape[0]
    l = m // batch
    t = kv.shape[1]
    nq = l // tq
    scale = float((QK_NOPE + QK_ROPE) ** -0.5 * np.log2(np.e))
    hw = QK_NOPE + 2 * QK_ROPE
    return pl.pallas_call(
        functools.partial(_attn_body, scale=scale), name="attention",
        grid=(batch, MLA_HEADS, nq),
        in_specs=[
            pl.BlockSpec((tq, hw), lambda b, h, i: (b * nq + i, h)),
            pl.BlockSpec((1, t, QK_NOPE), lambda b, h, i: (b, 0, 2 * h)),
            pl.BlockSpec((1, t, V_HEAD), lambda b, h, i: (b, 0, 2 * h + 1)),
            pl.BlockSpec((1, t, LANES), lambda b, h, i: (b, 0, 0)),
            pl.BlockSpec((tq, QK_ROPE), lambda b, h, i: (i, 0)),
            pl.BlockSpec((tq, QK_ROPE), lambda b, h, i: (i, 0)),
        ],
        out_specs=pl.BlockSpec((tq, V_HEAD), lambda b, h, i: (b * nq + i, h)),
        out_shape=jax.ShapeDtypeStruct((m, MLA_HEADS * V_HEAD), BF16),
        scratch_shapes=[pltpu.VMEM((t, QK_NOPE + LANES), BF16)],
        compiler_params=_cparams(3),
    )(q, kv, kv, kr, cos_q, sin_q)


def _sgu_body(u_ref, v_ref, lw_ref, lb_ref, ws_ref, bs_ref, o_ref, vn_ref):
    v = v_ref[...].astype(F32)
    mu = jnp.mean(v, axis=-1, keepdims=True)
    vc = v - mu
    var = jnp.mean(vc * vc, axis=-1, keepdims=True)
    vn_ref[...] = (vc * lax.rsqrt(var + 1e-5) * lw_ref[...] + lb_ref[...]).astype(BF16)
    n_chunks = v_ref.shape[0] // CHUNK
    for n in range(n_chunks):
        rows = slice(n * CHUNK, (n + 1) * CHUNK)
        for g in range(SG_GROUPS):
            cols = slice(g * LANES, (g + 1) * LANES)
            vm = jnp.dot(ws_ref[g], vn_ref[rows, cols], preferred_element_type=F32)
            vm = vm + bs_ref[:, cols]
            o_ref[rows, cols] = (u_ref[rows, cols].astype(F32) * vm).astype(BF16)


def _spatial_gate(hg, ln_w, ln_b, w_s, b_s, tm=256):
    m = hg.shape[0]
    d = hg.shape[1] // 2
    bs_full = jnp.repeat(b_s.T, d // SG_GROUPS, axis=1)
    return pl.pallas_call(
        _sgu_body, name="spatial_gate", grid=(m // tm,),
        in_specs=[
            pl.BlockSpec((tm, d), lambda i: (i, 0)),
            pl.BlockSpec((tm, d), lambda i: (i, 1)),
            pl.BlockSpec((1, d), lambda i: (0, 0)),
            pl.BlockSpec((1, d), lambda i: (0, 0)),
            pl.BlockSpec((SG_GROUPS, CHUNK, CHUNK), lambda i: (0, 0, 0)),
            pl.BlockSpec((CHUNK, d), lambda i: (0, 0)),
        ],
        out_specs=pl.BlockSpec((tm, d), lambda i: (i, 0)),
        out_shape=jax.ShapeDtypeStruct((m, d), BF16),
        scratch_shapes=[pltpu.VMEM((tm, d), BF16)],
        compiler_params=_cparams(1),
    )(hg, hg, ln_w.reshape(1, d), ln_b.reshape(1, d), w_s.astype(BF16), bs_full)


def _softplus(x):
    return jnp.maximum(x, 0.0) + jnp.log1p(jnp.exp(-jnp.abs(x)))


def _split_bf16(x, pieces):
    out = []
    for _ in range(pieces):
        p = x.astype(BF16)
        out.append(p)
        x = x - p.astype(F32)
    return out


def _dot_exact_rhs(x, w, pieces=2):
    return sum(jnp.dot(p, w, preferred_element_type=F32) for p in _split_bf16(x, pieces))


def _dot_split(x, w_hi, w_lo):
    x_hi, x_lo = _split_bf16(x, 2)
    return (jnp.dot(x_hi, w_hi, preferred_element_type=F32)
            + (jnp.dot(x_hi, w_lo, preferred_element_type=F32)
               + jnp.dot(x_lo, w_hi, preferred_element_type=F32)))


def _hi_lo(w):
    hi = w.astype(BF16)
    return hi, (w - hi.astype(F32)).astype(BF16)


def _prep_body(z_ref, zp_ref, zn_ref, mup_ref, mun_ref, kk_ref, ka_ref, w0_ref, a0_ref,
               wup_ref, aup_ref, gup_ref, rk_ref, e_ref, et_ref,
               lw0_o, lw1_o, k0_o, k1_o, b0_o, b1_o, v_o, kkn_o, r_o, g_o, bonus_o,
               *, tiles_ctx, tiles_total):
    tm = z_ref.shape[0]
    rt = pl.program_id(0) % tiles_total
    in_ctx = rt < tiles_ctx
    has_prev = jnp.where(in_ctx, rt > 0, rt > tiles_ctx)
    has_next = jnp.where(in_ctx, rt < tiles_ctx - 1, rt < tiles_total - 1)

    z = z_ref[...]
    rows = lax.broadcasted_iota(jnp.int32, z.shape, 0)
    prev_row = jnp.where(has_prev, zp_ref[7:8, :], 0.0)
    next_row = jnp.where(has_next, zn_ref[0:1, :], 0.0)
    z_prev = jnp.where(rows == 0, prev_row, pltpu.roll(z, 1, 0))
    z_next = jnp.where(rows == tm - 1, next_row, pltpu.roll(z, tm - 1, 0))
    zs = z + mup_ref[...] * (z_prev - z) + mun_ref[...] * (z_next - z)

    d = RWKV_DIM
    r = zs[:, 0:d]
    k = zs[:, d:2 * d]
    v = zs[:, 2 * d:3 * d]
    wd = zs[:, 3 * d:3 * d + LANES]
    ad = zs[:, 3 * d + LANES:3 * d + 2 * LANES]
    gd = zs[:, 3 * d + 2 * LANES:3 * d + 4 * LANES]

    def head_sum(x):
        return _dot_exact_rhs(_dot_exact_rhs(x, e_ref[...]), et_ref[...])

    kk = k * kk_ref[...]
    nrm = jnp.sqrt(head_sum(kk * kk))
    kkn = kk / jnp.maximum(nrm, 1e-12)

    wx = w0_ref[...] + _dot_split(jnp.tanh(wd), wup_ref[0], wup_ref[1])
    w_log = -_softplus(-wx) - 0.5
    lw = -jnp.exp(w_log)
    ax = a0_ref[...] + _dot_split(ad, aup_ref[0], aup_ref[1])
    iclr = jax.nn.sigmoid(ax)
    g = _dot_split(jax.nn.sigmoid(gd), gup_ref[0], gup_ref[1])

    ka = ka_ref[...]
    k0 = k * (1.0 + (iclr[:, :d] - 1.0) * ka)
    k1 = k * (1.0 + (iclr[:, d:] - 1.0) * ka)
    lw0_o[...] = lw[:, :d]
    lw1_o[...] = lw[:, d:]
    k0_o[...] = k0
    k1_o[...] = k1
    b0_o[...] = iclr[:, :d] * kkn
    b1_o[...] = iclr[:, d:] * kkn
    v_o[...] = v
    kkn_o[...] = kkn
    r_o[...] = r
    g_o[...] = g
    bonus_o[...] = head_sum(r * (k0 + k1) * rk_ref[...]) * v


def _head_indicator():
    head_of = jnp.arange(RWKV_DIM) // RWKV_HEAD
    return (head_of[:, None] == jnp.arange(LANES)[None, :]).astype(F32)


def _rwkv_prep(z_rw, p, rows_ctx, rows_total):
    m, w = z_rw.shape
    tm = PREP_TM
    d = RWKV_DIM
    assert rows_ctx % tm == 0 and rows_total % tm == 0 and m % rows_total == 0
    pad = w - RWKV_IN
    mup = jnp.pad(p['mu_prev'], (0, pad)).reshape(1, w)
    mun = jnp.pad(p['mu_next'], (0, pad)).reshape(1, w)
    zero = jnp.zeros((DECAY_LORA, d), F32)
    wup = jnp.concatenate([jnp.concatenate([p['w_up'][0], zero], axis=1),
                           jnp.concatenate([zero, p['w_up'][1]], axis=1)], axis=0)
    aup = jnp.concatenate([jnp.concatenate([p['a_up'][0], zero], axis=1),
                           jnp.concatenate([zero, p['a_up'][1]], axis=1)], axis=0)
    gup = jnp.pad(p['g_up'], ((0, 2 * LANES - GATE_LORA), (0, 0)))
    wup, aup, gup = (jnp.stack(_hi_lo(w)) for w in (wup, aup, gup))
    e = _head_indicator().astype(BF16)

    def full(shape):
        return pl.BlockSpec(shape, lambda i: (0,) * len(shape))

    n8 = m // 8
    out_spec = pl.BlockSpec((tm, d), lambda i: (i, 0))
    out_sds = jax.ShapeDtypeStruct((m, d), F32)
    return pl.pallas_call(
        functools.partial(_prep_body, tiles_ctx=rows_ctx // tm, tiles_total=rows_total // tm),
        name="rwkv_prep",
        grid=(m // tm,),
        in_specs=[
            pl.BlockSpec((tm, w), lambda i: (i, 0)),
            pl.BlockSpec((8, w), lambda i: (jnp.maximum(i * (tm // 8) - 1, 0), 0)),
            pl.BlockSpec((8, w), lambda i: (jnp.minimum((i + 1) * (tm // 8), n8 - 1), 0)),
            full((1, w)), full((1, w)), full((1, d)), full((1, d)), full((1, 2 * d)),
            full((1, 2 * d)), full((2, LANES, 2 * d)), full((2, LANES, 2 * d)),
            full((2, 2 * LANES, d)), full((1, d)), full((d, LANES)), full((LANES, d)),
        ],
        out_specs=[out_spec] * 11,
        out_shape=[out_sds] * 11,
        compiler_params=_cparams(1),
    )(z_rw, z_rw, z_rw, mup, mun, p['k_k'].reshape(1, d), p['k_a'].reshape(1, d),
      p['w0'].reshape(1, 2 * d), p['a0'].reshape(1, 2 * d), wup, aup, gup,
      p['r_k'].reshape(1, d), e, e.T)


def _bdot(x, y):
    return jnp.dot(x.astype(BF16), y.astype(BF16), preferred_element_type=F32)


def _chunk_units(units):
    c = RW_CHUNK
    row = lax.broadcasted_iota(jnp.int32, (c, LANES), 0)
    lane = lax.broadcasted_iota(jnp.int32, (c, LANES), 1)
    pos = lane % c
    lo = lane < c
    tr = lax.broadcasted_iota(jnp.int32, (c, c), 0)
    tc = lax.broadcasted_iota(jnp.int32, (c, c), 1)
    tri = {False: (tc <= tr).astype(BF16), True: (tc >= tr).astype(BF16)}
    strict = {False: pos < row, True: pos > row}
    incl = {False: pos <= row, True: pos >= row}
    last = {False: c - 1, True: 0}
    rev = [u[6] for u in units]
    nu = range(len(units))

    def sb(x):
        return jnp.concatenate([jnp.where(lo, x, 0.0), jnp.where(lo, 0.0, x)], axis=0)

    def nt(x, y):
        return lax.dot_general(x.astype(BF16), y.astype(BF16), (((1,), (1,)), ((), ())),
                               preferred_element_type=F32)

    def tn(x, y):
        return lax.dot_general(x.astype(BF16), y.astype(BF16), (((0,), (0,)), ((), ())),
                               preferred_element_type=F32)

    def fold(x):
        return jnp.where(lo, x[:c], 0.0) + jnp.where(lo, 0.0, x[c:])

    lw3 = [_split_bf16(units[i][0], 3) for i in nu]
    lc = [sum(jnp.dot(tri[rev[i]], piece, preferred_element_type=F32) for piece in lw3[i])
          for i in nu]
    ltot = [lc[i][last[rev[i]]:last[rev[i]] + 1, :] for i in nu]
    e_neg = [jnp.exp(-lc[i]) for i in nu]
    e_h = [jnp.exp(ltot[i] - lc[i]) for i in nu]
    at = [-units[i][3] * jnp.exp(lc[i] - units[i][0]) for i in nu]
    rt = [units[i][5] * jnp.exp(lc[i]) for i in nu]
    bt = [units[i][2] * e_neg[i] for i in nu]
    kt = [units[i][1] * e_neg[i] for i in nu]
    bh = [units[i][2] * e_h[i] for i in nu]
    kh = [units[i][1] * e_h[i] for i in nu]

    ar = [jnp.concatenate([at[i], rt[i]], axis=0) for i in nu]
    m_b = [nt(ar[i], sb(bt[i])) for i in nu]
    m_k = [nt(ar[i], sb(kt[i])) for i in nu]
    n = [jnp.where(strict[rev[i]], m_b[i][:c], 0.0) for i in nu]
    m_ak = [jnp.where(strict[rev[i]], m_k[i][:c], 0.0) for i in nu]
    m_rb = [jnp.where(incl[rev[i]], m_b[i][c:], 0.0) for i in nu]
    m_rk = [jnp.where(incl[rev[i]], m_k[i][c:], 0.0) for i in nu]
    mv = [_bdot(jnp.concatenate([m_ak[i], m_rk[i]], axis=0), sb(units[i][4])) for i in nu]
    xa = at
    xu = [mv[i][:c] for i in nu]
    n_steps = c.bit_length() - 1
    for step in range(n_steps):
        upd = [_bdot(n[i], jnp.concatenate([sb(xa[i]), sb(xu[i])], axis=1)) for i in nu]
        xa = [xa[i] + upd[i][:, :LANES] for i in nu]
        xu = [xu[i] + upd[i][:, LANES:] for i in nu]
        if step < n_steps - 1:
            n = [_bdot(n[i], sb(n[i])) for i in nu]
    rx = [_bdot(m_rb[i], jnp.concatenate([sb(xa[i]), sb(xu[i])], axis=1)) for i in nu]
    gh = [tn(bh[i], jnp.concatenate([xa[i], xu[i]], axis=1)) for i in nu]
    hk = [tn(kh[i], units[i][4]) for i in nu]
    out = []
    for i in nu:
        rp = rt[i] + rx[i][:, :LANES]
        y0 = rx[i][:, LANES:] + mv[i][c:]
        g = fold(gh[i][:, :LANES]) + jnp.where(pos == row, jnp.exp(ltot[i]), 0.0)
        h = fold(gh[i][:, LANES:] + hk[i])
        out.append((g, h, rp, y0))
    return out


def _chunk_body(lw0_ref, lw1_ref, k0_ref, k1_ref, b0_ref, b1_ref, v_ref, kkn_ref, r_ref,
                g0_o, h0_o, rp0_o, y00_o, g1_o, h1_o, rp1_o, y01_o):
    n_chunks = v_ref.shape[0] // RW_CHUNK
    units = []
    for ci in range(n_chunks):
        rows = slice(ci * RW_CHUNK, (ci + 1) * RW_CHUNK)
        v = v_ref[rows, :]
        kkn = kkn_ref[rows, :]
        r = r_ref[rows, :]
        units.append((lw0_ref[rows, :], k0_ref[rows, :], b0_ref[rows, :], kkn, v, r, False))
        units.append((lw1_ref[rows, :], k1_ref[rows, :], b1_ref[rows, :], kkn, v, r, True))
    res = _chunk_units(units)
    for ci in range(n_chunks):
        rows = slice(ci * RW_CHUNK, (ci + 1) * RW_CHUNK)
        for dr, outs in enumerate(((g0_o, h0_o, rp0_o, y00_o), (g1_o, h1_o, rp1_o, y01_o))):
            for o_ref, val in zip(outs, res[2 * ci + dr]):
                o_ref[rows, :] = val


def _rwkv_chunk(prep, tm=512):
    lw0, lw1, k0, k1, b0, b1, v, kkn, r = prep
    m, d = v.shape
    spec = pl.BlockSpec((tm, LANES), lambda i, pr: (i, pr))
    sds = jax.ShapeDtypeStruct((m, d), F32)
    return pl.pallas_call(
        _chunk_body, name="rwkv_chunk", grid=(m // tm, d // LANES),
        in_specs=[spec] * 9, out_specs=[spec] * 8, out_shape=[sds] * 8,
        compiler_params=_cparams(2),
    )(lw0, lw1, k0, k1, b0, b1, v, kkn, r)


def _scan_body(gf_ref, hf_ref, rpf_ref, y0f_ref, gb_ref, hb_ref, rpb_ref, y0b_ref,
               yf_ref, yb_ref, s_ref):
    c = RW_CHUNK
    step = pl.program_id(1)

    @pl.when(step == 0)
    def _():
        s_ref[...] = jnp.zeros(s_ref.shape, F32)

    lane = lax.broadcasted_iota(jnp.int32, (c, LANES), 1)
    lo = lane < c

    def sb(x):
        return jnp.concatenate([jnp.where(lo, x, 0.0), jnp.where(lo, 0.0, x)], axis=0)

    dirs = ((gf_ref, hf_ref, rpf_ref, y0f_ref, yf_ref), (gb_ref, hb_ref, rpb_ref, y0b_ref, yb_ref))
    cols = [slice(pr * LANES, (pr + 1) * LANES) for pr in range(N_PAIRS)]
    out = []
    for dr, (g_ref, _, rp_ref, _, _) in enumerate(dirs):
        lhs = [jnp.concatenate([rp_ref[:, cl], sb(g_ref[:, cl])], axis=0) for cl in cols]
        out.append([_bdot(lhs[pr], s_ref[dr, pr]) for pr in range(N_PAIRS)])
    for dr, (_, h_ref, _, y0_ref, y_ref) in enumerate(dirs):
        for pr, cl in enumerate(cols):
            y_ref[:, cl] = out[dr][pr][:c] + y0_ref[:, cl]
            s_ref[dr, pr] = out[dr][pr][c:] + sb(h_ref[:, cl])


def _rwkv_scan(mats, batch, chunks_ctx):
    m, d = mats[0].shape
    c = RW_CHUNK
    n_chunks = m // c // batch

    def chunk_of(b, s, reverse):
        if not reverse:
            return b * n_chunks + s
        rev = jnp.where(s < chunks_ctx, chunks_ctx - 1 - s, n_chunks - 1 - (s - chunks_ctx))
        return b * n_chunks + rev

    fwd = pl.BlockSpec((c, d), lambda b, s: (chunk_of(b, s, False), 0))
    bwd = pl.BlockSpec((c, d), lambda b, s: (chunk_of(b, s, True), 0))
    sds = jax.ShapeDtypeStruct((m, d), F32)
    return pl.pallas_call(
        _scan_body, name="rwkv_scan", grid=(batch, n_chunks),
        in_specs=[fwd] * 4 + [bwd] * 4, out_specs=[fwd, bwd], out_shape=[sds, sds],
        scratch_shapes=[pltpu.VMEM((2, N_PAIRS, LANES, LANES), F32)],
        compiler_params=_cparams(2),
    )(*mats)


def _readout_body(y0_ref, y1_ref, bonus_ref, g_ref, att_ref, lw_ref, lb_ref, e_ref, et_ref, o_ref):
    def head_mean(x):
        return _dot_exact_rhs(_dot_exact_rhs(x, e_ref[...]), et_ref[...]) * (1.0 / RWKV_HEAD)

    y = y0_ref[...] + y1_ref[...]
    mu = head_mean(y)
    yc = y - mu
    var = head_mean(yc * yc)
    yn = yc * lax.rsqrt(var + LNX_EPS) * lw_ref[...] + lb_ref[...]
    d_att = att_ref.shape[1]
    o_ref[:, :d_att] = att_ref[...]
    o_ref[:, d_att:] = ((yn + bonus_ref[...]) * g_ref[...]).astype(BF16)


def _rwkv_readout(y0, y1, bonus, g, att, p, rows_ctx, rows_total, tm=256):
    d = RWKV_DIM
    n_lat, d_att = att.shape
    assert rows_ctx % tm == 0 and rows_total % tm == 0
    per_b = (rows_total - rows_ctx) // tm
    lat0 = rows_ctx // tm
    tot = rows_total // tm
    e = _head_indicator().astype(BF16)
    rw_spec = pl.BlockSpec((tm, d), lambda i: ((i // per_b) * tot + lat0 + i % per_b, 0))

    def const(shape):
        return pl.BlockSpec(shape, lambda i: (0, 0))

    return pl.pallas_call(
        _readout_body, name="rwkv_readout", grid=(n_lat // tm,),
        in_specs=[rw_spec, rw_spec, rw_spec, rw_spec,
                  pl.BlockSpec((tm, d_att), lambda i: (i, 0)),
                  const((1, d)), const((1, d)), const((d, LANES)), const((LANES, d))],
        out_specs=pl.BlockSpec((tm, d_att + d), lambda i: (i, 0)),
        out_shape=jax.ShapeDtypeStruct((n_lat, d_att + d), BF16),
        compiler_params=_cparams(1),
    )(y0, y1, bonus, g, att, p['lnx_w'].reshape(1, d), p['lnx_b'].reshape(1, d), e, e.T)


def _rwkv_mixer(z_rw, att, p, batch, rows_ctx, rows_total):
    outs = _rwkv_prep(z_rw, p, rows_ctx, rows_total)
    g, bonus = outs[9], outs[10]
    mats = _rwkv_chunk(outs[:9])
    y0, y1 = _rwkv_scan(mats, batch, rows_ctx // RW_CHUNK)
    return _rwkv_readout(y0, y1, bonus, g, att, p, rows_ctx, rows_total)


def _rope_swap_cols(w):
    f = ROPE_FREQS
    parts = []
    for a in range(2):
        x1 = w[..., (2 * a) * f:(2 * a + 1) * f]
        x2 = w[..., (2 * a + 1) * f:(2 * a + 2) * f]
        parts += [-x2, x1]
    return jnp.concatenate(parts, axis=-1)


def _rope_tables(l):
    pos = np.arange(l)
    inv = ROPE_THETA ** (-np.arange(ROPE_FREQS, dtype=np.float64) / ROPE_FREQS)
    ar = (pos // GRID_W)[:, None] * inv
    ac = (pos % GRID_W)[:, None] * inv
    cos = np.concatenate([np.cos(ar), np.cos(ar), np.cos(ac), np.cos(ac)], axis=1)
    sin = np.concatenate([np.sin(ar), np.sin(ar), np.sin(ac), np.sin(ac)], axis=1)
    return cos.astype(np.float32), sin.astype(np.float32)


def _gather_body(tok_ref, tv_ref, x_hbm, o_ref, buf_ref, sem):
    i = pl.program_id(0)
    tm = o_ref.shape[0]
    base = i * tm

    def row_copy(r):
        return pltpu.make_async_copy(x_hbm.at[pl.ds(tok_ref[base + r], 1), :],
                                     buf_ref.at[pl.ds(r, 1), :], sem)

    @pl.when(tv_ref[i] == 1)
    def _():
        def start(r, carry):
            row_copy(r).start()
            return carry

        def wait(r, carry):
            row_copy(r).wait()
            return carry

        lax.fori_loop(0, tm, start, 0, unroll=8)
        lax.fori_loop(0, tm, wait, 0, unroll=8)
        o_ref[...] = buf_ref[...].astype(BF16)

    @pl.when(tv_ref[i] == 0)
    def _():
        o_ref[...] = jnp.zeros(o_ref.shape, o_ref.dtype)


def _gather_rows_bf16(x, row_tok, tile_valid, tm):
    d = x.shape[1]
    n_rows = row_tok.shape[0]
    gs = pltpu.PrefetchScalarGridSpec(
        num_scalar_prefetch=2,
        grid=(n_rows // tm,),
        in_specs=[pl.BlockSpec(memory_space=pl.ANY)],
        out_specs=pl.BlockSpec((tm, d), lambda i, tok, tv: (i, 0)),
        scratch_shapes=[pltpu.VMEM((tm, d), F32), pltpu.SemaphoreType.DMA(())])
    return pl.pallas_call(
        _gather_body, grid_spec=gs, name="moe_gather",
        out_shape=jax.ShapeDtypeStruct((n_rows, d), BF16),
        compiler_params=_cparams(1),
    )(row_tok, tile_valid, x)


def _moe(xn, logits, w1, w3, w2):
    n_tok = xn.shape[0]
    tm = MOE_TM
    top_val, top_idx = lax.top_k(logits, TOP_K)
    gate = jax.nn.softmax(top_val, axis=-1)
    n_assign = n_tok * TOP_K
    e_flat = top_idx.reshape(-1).astype(jnp.int32)
    experts = jnp.arange(N_EXPERTS, dtype=jnp.int32)
    order = jnp.argsort(e_flat).astype(jnp.int32)
    rank = jnp.argsort(order).astype(jnp.int32)
    counts = jnp.sum((e_flat[:, None] == experts[None, :]).astype(jnp.int32), axis=0)
    padded = (counts + tm - 1) // tm * tm
    pad_end = jnp.cumsum(padded)
    pad_start = pad_end - padded
    grp_start = jnp.cumsum(counts) - counts
    pos = (pad_start[e_flat] + rank - grp_start[e_flat]).reshape(n_tok, TOP_K)
    n_tiles = n_assign // tm + N_EXPERTS
    n_rows = n_tiles * tm

    def expert_of(row):
        return jnp.minimum(jnp.sum((row[:, None] >= pad_end[None, :]).astype(jnp.int32), axis=1),
                           N_EXPERTS - 1)

    rows = jnp.arange(n_rows, dtype=jnp.int32)
    e_row = expert_of(rows)
    off = rows - pad_start[e_row]
    src = jnp.clip(grp_start[e_row] + off, 0, n_assign - 1)
    row_tok = jnp.where(off < counts[e_row], order[src] // TOP_K, 0)
    tile_start = jnp.arange(n_tiles, dtype=jnp.int32) * tm
    n_valid = pad_end[-1] // tm
    tile_src = jnp.minimum(jnp.arange(n_tiles, dtype=jnp.int32), n_valid - 1).astype(jnp.int32)
    tile_exp = expert_of(tile_start)[tile_src]
    tile_valid = (tile_start < pad_end[-1]).astype(jnp.int32)
    group = (tile_exp, tile_valid, tile_src)

    def rows_of(a, idx):
        return a.at[idx].get(mode="promise_in_bounds")

    xs = _gather_rows_bf16(xn, row_tok.astype(jnp.int32), tile_valid, tm)
    hmid = _matmul(xs, [w1, w3], name="moe_up", tm=tm, tn=512, out_dtype=BF16,
                   epilogue="swiglu", group=group)
    ys = _matmul(hmid, [w2], name="moe_down", tm=tm, tn=512, out_dtype=F32, group=group,
                 single_buffer_w=True)
    return rows_of(ys, pos[:, 0]), rows_of(ys, pos[:, 1]), gate


def kernel(x, c, ctx, c_ctx, l0_ada_w, l0_ada_b, l0_norm1, l0_norm2, l0_w_in, l0_q_norm, l0_w_uq, l0_kv_norm, l0_w_ukv, l0_mu_prev, l0_mu_next, l0_w0, l0_w_up, l0_a0, l0_a_up, l0_g_up, l0_k_k, l0_k_a, l0_r_k, l0_lnx_w, l0_lnx_b, l0_w_o, l0_ffn_w1, l0_ffn_w3, l0_ffn_w2, l1_ada_w, l1_ada_b, l1_norm1, l1_norm2, l1_w_in, l1_v_ln_w, l1_v_ln_b, l1_w_s, l1_b_s, l1_w_o, l1_router, l1_moe_w1, l1_moe_w3, l1_moe_w2, final_norm):
    b, l, d = x.shape
    lc = ctx.shape[1]
    t = lc + l
    n_tok = b * l

    cond = jnp.zeros((8, d), F32).at[:b].set(c).at[b].set(c_ctx)
    mod0 = _matmul(cond, [l0_ada_w], name="ada_mod", tm=8, tn=1536, out_dtype=F32, prologue="silu",
                   epilogue="bias", bias=l0_ada_b)
    mod1 = _matmul(cond, [l1_ada_w], name="ada_mod", tm=8, tn=1536, out_dtype=F32, prologue="silu",
                   epilogue="bias", bias=l1_ada_b)

    def mods(mod, row0, nrows):
        return [mod[row0:row0 + nrows, i * d:(i + 1) * d].reshape(nrows, 1, d) for i in range(N_MOD)]

    sh1, sc1, g1, sh2, sc2, g2 = mods(mod0, 0, b)
    csh1, csc1 = mods(mod0, b, 1)[:2]

    xall = _norm_mod_merge(x, ctx, l0_norm1, sh1, sc1, csh1, csc1)
    mla_in = Q_LORA + KV_LORA + QK_ROPE
    w_kr = l0_w_in[:, Q_LORA + KV_LORA:mla_in]
    w_mla = jnp.concatenate([l0_w_in[:, :mla_in], _rope_swap_cols(w_kr)], axis=1)
    w_rw = jnp.pad(l0_w_in[:, mla_in:], ((0, 0), (0, RWKV_PAD - RWKV_IN)))
    xall2 = xall.reshape(b * t, d)
    z_all = _matmul(xall2, [w_mla], name="in_mla", tm=1088, tn=384, out_dtype=F32)
    z_rw = _matmul(xall2, [w_rw], name="in_rwkv", tm=1088, tn=512, out_dtype=F32)

    wq = l0_w_uq.reshape(Q_LORA, MLA_HEADS, QK_NOPE + QK_ROPE)
    wq_ext = jnp.concatenate([wq, _rope_swap_cols(wq[..., QK_NOPE:])], axis=-1)
    wq_ext = wq_ext.reshape(Q_LORA, MLA_HEADS * (QK_NOPE + 2 * QK_ROPE))
    tq_rows = 256
    per_b = l // tq_rows
    lat0 = lc // tq_rows

    def lat_rows(i):
        return (i // per_b) * (t // tq_rows) + lat0 + i % per_b

    q = _matmul(z_all, [wq_ext], name="q_up", tm=tq_rows, tn=2048, out_dtype=BF16, k=Q_LORA, x_col_block=0,
                x_row_map=lat_rows, m_out=n_tok, prologue="rms", gain=l0_q_norm)
    kv = _matmul(z_all, [l0_w_ukv], name="kv_up", tm=1088, tn=1024, out_dtype=BF16, k=KV_LORA, x_col_block=1,
                 prologue="rms", gain=l0_kv_norm)
    cos_l, sin_l = _rope_tables(l)
    cos_t = np.concatenate([np.ones((lc, QK_ROPE), np.float32), cos_l], axis=0)
    sin_t = np.concatenate([np.zeros((lc, QK_ROPE), np.float32), sin_l], axis=0)
    kr = _krope(z_all, cos_t, sin_t, col_block=(Q_LORA + KV_LORA) // LANES)
    att = _attention(q, kv.reshape(b, t, -1), kr.reshape(b, t, LANES), cos_l, sin_l, b)

    p0 = dict(mu_prev=l0_mu_prev, mu_next=l0_mu_next, w0=l0_w0, w_up=l0_w_up, a0=l0_a0,
              a_up=l0_a_up, g_up=l0_g_up, k_k=l0_k_k, k_a=l0_k_a, r_k=l0_r_k,
              lnx_w=l0_lnx_w, lnx_b=l0_lnx_b)
    mix = _rwkv_mixer(z_rw, att, p0, b, lc, t)
    h = _matmul(mix, [l0_w_o], name="mix_out", tm=1024, tn=512, out_dtype=F32, epilogue="resid",
                resid=x.reshape(n_tok, d), gate=g1, rows_per_gate=l)

    xn = _norm_mod(h.reshape(b, l, d), l0_norm2, sh2, sc2).reshape(n_tok, d)
    hmid = _matmul(xn, [l0_ffn_w1, l0_ffn_w3], name="ffn_up", tm=1024, tn=512, out_dtype=BF16,
                   epilogue="swiglu")
    h = _matmul(hmid, [l0_ffn_w2], name="ffn_down", tm=512, tn=512, out_dtype=F32, epilogue="resid",
                resid=h, gate=g2, rows_per_gate=l, single_buffer_w=True)

    sh1, sc1, g1, sh2, sc2, g2 = mods(mod1, 0, b)
    xn = _norm_mod(h.reshape(b, l, d), l1_norm1, sh1, sc1).reshape(n_tok, d)
    hg = _matmul(xn, [l1_w_in], name="gmlp_in", tm=1024, tn=512, out_dtype=BF16, epilogue="gelu")
    gated = _spatial_gate(hg, l1_v_ln_w, l1_v_ln_b, l1_w_s, l1_b_s)
    h = _matmul(gated, [l1_w_o], name="gmlp_out", tm=1024, tn=512, out_dtype=F32, epilogue="resid",
                resid=h, gate=g1, rows_per_gate=l)

    xn, logits = _norm_mod(h.reshape(b, l, d), l1_norm2, sh2, sc2, router=l1_router)
    ya, yb, gate = _moe(xn.reshape(n_tok, d), logits.reshape(n_tok, LANES)[:, :N_EXPERTS],
                        l1_moe_w1, l1_moe_w3, l1_moe_w2)
    return _combine_norm(h, ya, yb, gate, g2, final_norm, l).reshape(b, l, d)
```

```python
import functools

import jax
import jax.numpy as jnp
import numpy as np
from jax import lax
from jax.experimental import pallas as pl
from jax.experimental.pallas import tpu as pltpu

F32 = jnp.float32
BF16 = jnp.bfloat16

D_MODEL = 2048
GRID_W = 64
N_MOD = 6
NORM_EPS = 1e-6
V_HEAD = 128
MLA_HEADS = 8
Q_LORA = 512
KV_LORA = 512
QK_NOPE = 128
QK_ROPE = 64
ROPE_FREQS = 16
ROPE_THETA = 10000.0
RWKV_HEAD = 64
RWKV_HEADS = 16
RWKV_DIM = 1024
DECAY_LORA = 64
ICLR_LORA = 64
GATE_LORA = 160
LNX_EPS = 64e-5
RWKV_IN = 3 * RWKV_DIM + 2 * DECAY_LORA + 2 * ICLR_LORA + GATE_LORA
RWKV_PAD = 3584
RW_CHUNK = 64
PREP_TM = 128
N_PAIRS = RWKV_HEADS // 2
HI = lax.Precision.HIGHEST
CHUNK = 128
SG_GROUPS = 16
D_FF = 7168
N_EXPERTS = 8
TOP_K = 2

VMEM_LIMIT_BYTES = 56 * 1024 * 1024
LANES = 128

MOE_TM = 512


def _cparams(n_axes):
    return pltpu.CompilerParams(
        dimension_semantics=("arbitrary",) * n_axes,
        vmem_limit_bytes=VMEM_LIMIT_BYTES)


def _mm_body(*refs, n_w, prologue, epilogue, grouped):
    refs = list(refs)
    if grouped:
        te_ref, tv_ref, _ = refs[:3]
        refs = refs[3:]
    x_ref = refs.pop(0)
    w_refs = [refs.pop(0) for _ in range(n_w)]
    gain_ref = refs.pop(0) if prologue == "rms" else None
    bias_ref = refs.pop(0) if epilogue == "bias" else None
    if epilogue == "resid":
        resid_ref = refs.pop(0)
        gate_ref = refs.pop(0)
    o_ref, wbf_ref = refs

    i = pl.program_id(1)
    if grouped:
        new_w = jnp.logical_or(i == 0, te_ref[i] != te_ref[jnp.maximum(i - 1, 0)])
    else:
        new_w = i == 0

    @pl.when(new_w)
    def _():
        for n in range(n_w):
            wbf_ref[n] = w_refs[n][...].astype(BF16)

    def compute():
        x = x_ref[...]
        if prologue == "rms":
            xf = x.astype(F32)
            ms = jnp.mean(xf * xf, axis=-1, keepdims=True)
            x = xf * lax.rsqrt(ms + NORM_EPS) * gain_ref[...]
        elif prologue == "silu":
            xf = x.astype(F32)
            x = xf * jax.nn.sigmoid(xf)
        x = x.astype(BF16)
        acc = [jnp.dot(x, wbf_ref[n], preferred_element_type=F32) for n in range(n_w)]
        if epilogue == "swiglu":
            a = acc[0]
            out = a * jax.nn.sigmoid(a) * acc[1]
        elif epilogue == "gelu":
            a = acc[0]
            out = 0.5 * a * (1.0 + lax.erf(a * (2.0 ** -0.5)))
        elif epilogue == "bias":
            out = acc[0] + bias_ref[...]
        elif epilogue == "resid":
            out = resid_ref[...] + gate_ref[0] * acc[0]
        else:
            out = acc[0]
        o_ref[...] = out.astype(o_ref.dtype)

    if grouped:
        @pl.when(tv_ref[i] == 1)
        def _():
            compute()

        @pl.when(tv_ref[i] == 0)
        def _():
            o_ref[...] = jnp.zeros(o_ref.shape, o_ref.dtype)
    else:
        compute()


def _matmul(x, ws, *, name, tm, tn, out_dtype, k=None, x_col_block=0, x_row_map=None, m_out=None,
            prologue=None, gain=None, epilogue=None, bias=None, resid=None, gate=None,
            rows_per_gate=None, group=None, single_buffer_w=False):
    grouped = group is not None
    kdim = k if k is not None else x.shape[1]
    n = ws[0].shape[-1]
    m = m_out if m_out is not None else x.shape[0]
    assert m % tm == 0 and n % tn == 0, (m, tm, n, tn)
    n_w = len(ws)
    grid = (n // tn, m // tm)
    w_mode = dict(pipeline_mode=pl.Buffered(1)) if single_buffer_w else {}

    if grouped:
        def xmap(j, i, te, tv, ts):
            return (ts[i], x_col_block)

        def wmap(j, i, te, tv, ts):
            return (te[i], 0, j)

        def omap(j, i, te, tv, ts):
            return (i, j)
        w_spec = pl.BlockSpec((None, kdim, tn), wmap, **w_mode)
    else:
        def xmap(j, i):
            return ((x_row_map(i) if x_row_map is not None else i), x_col_block)

        def wmap(j, i):
            return (0, j)

        def omap(j, i):
            return (i, j)
        w_spec = pl.BlockSpec((kdim, tn), wmap, **w_mode)

    in_specs = [pl.BlockSpec((tm, kdim), xmap)] + [w_spec] * n_w
    args = [x] + list(ws)
    if prologue == "rms":
        in_specs.append(pl.BlockSpec((1, kdim), lambda j, i, *_: (0, 0)))
        args.append(gain.reshape(1, kdim))
    if epilogue == "bias":
        in_specs.append(pl.BlockSpec((1, tn), lambda j, i, *_: (0, j)))
        args.append(bias.reshape(1, n))
    if epilogue == "resid":
        in_specs.append(pl.BlockSpec((tm, tn), omap))
        args.append(resid)
        assert rows_per_gate % tm == 0
        tiles_per_gate = rows_per_gate // tm
        in_specs.append(pl.BlockSpec((1, 1, tn), lambda j, i, *_: (i // tiles_per_gate, 0, j)))
        args.append(gate)

    body = functools.partial(_mm_body, n_w=n_w, prologue=prologue, epilogue=epilogue,
                             grouped=grouped)
    gs = pltpu.PrefetchScalarGridSpec(
        num_scalar_prefetch=3 if grouped else 0,
        grid=grid,
        in_specs=in_specs,
        out_specs=pl.BlockSpec((tm, tn), omap),
        scratch_shapes=[pltpu.VMEM((n_w, kdim, tn), BF16)])
    call = pl.pallas_call(
        body, grid_spec=gs, name=name,
        out_shape=jax.ShapeDtypeStruct((m, n), out_dtype),
        compiler_params=_cparams(2))
    if grouped:
        return call(*group, *args)
    return call(*args)


def _norm_mod_math(v, g, sh, sc):
    ms = jnp.mean(v * v, axis=-1, keepdims=True)
    y = v * lax.rsqrt(ms + NORM_EPS) * g
    return y * (1.0 + sc) + sh


def _nm_merge_body(x_ref, c_ref, g_ref, sh_ref, sc_ref, csh_ref, csc_ref, o_ref):
    r = pl.program_id(1)

    @pl.when(r == 0)
    def _():
        o_ref[0] = _norm_mod_math(c_ref[0], g_ref[...], csh_ref[...], csc_ref[...]).astype(BF16)

    @pl.when(r > 0)
    def _():
        o_ref[0] = _norm_mod_math(x_ref[0], g_ref[...], sh_ref[0], sc_ref[0]).astype(BF16)


def _norm_mod_merge(x, ctx, gain, sh, sc, csh, csc):
    b, l, d = x.shape
    lc = ctx.shape[1]
    tm = lc
    assert l % tm == 0
    nt = l // tm + 1
    return pl.pallas_call(
        _nm_merge_body, name="norm_mod_merge",
        grid=(b, nt),
        in_specs=[
            pl.BlockSpec((1, tm, d), lambda bi, r: (bi, jnp.maximum(r - 1, 0), 0)),
            pl.BlockSpec((1, lc, d), lambda bi, r: (bi, 0, 0)),
            pl.BlockSpec((1, d), lambda bi, r: (0, 0)),
            pl.BlockSpec((1, 1, d), lambda bi, r: (bi, 0, 0)),
            pl.BlockSpec((1, 1, d), lambda bi, r: (bi, 0, 0)),
            pl.BlockSpec((1, d), lambda bi, r: (0, 0)),
            pl.BlockSpec((1, d), lambda bi, r: (0, 0)),
        ],
        out_specs=pl.BlockSpec((1, tm, d), lambda bi, r: (bi, r, 0)),
        out_shape=jax.ShapeDtypeStruct((b, lc + l, d), BF16),
        compiler_params=_cparams(2),
    )(x, ctx, gain.reshape(1, d), sh, sc, csh.reshape(1, d), csc.reshape(1, d))


def _nm_body(x_ref, g_ref, sh_ref, sc_ref, o_ref):
    o_ref[0] = _norm_mod_math(x_ref[0], g_ref[...], sh_ref[0], sc_ref[0]).astype(BF16)


def _nm_router_body(x_ref, g_ref, sh_ref, sc_ref, r_ref, o_ref, lg_ref):
    y = _norm_mod_math(x_ref[0], g_ref[...], sh_ref[0], sc_ref[0])
    o_ref[0] = y
    lg_ref[0] = jnp.dot(y, r_ref[...], precision=lax.Precision.HIGHEST,
                        preferred_element_type=F32)


def _norm_mod(h, gain, sh, sc, router=None, tm=512):
    b, l, d = h.shape
    assert l % tm == 0
    in_specs = [
        pl.BlockSpec((1, tm, d), lambda bi, r: (bi, r, 0)),
        pl.BlockSpec((1, d), lambda bi, r: (0, 0)),
        pl.BlockSpec((1, 1, d), lambda bi, r: (bi, 0, 0)),
        pl.BlockSpec((1, 1, d), lambda bi, r: (bi, 0, 0)),
    ]
    o_spec = pl.BlockSpec((1, tm, d), lambda bi, r: (bi, r, 0))
    o_shape = jax.ShapeDtypeStruct((b, l, d), BF16)
    if router is None:
        return pl.pallas_call(
            _nm_body, name="norm_mod", grid=(b, l // tm), in_specs=in_specs, out_specs=o_spec,
            out_shape=o_shape, compiler_params=_cparams(2),
        )(h, gain.reshape(1, d), sh, sc)
    ne = router.shape[1]
    router_pad = jnp.pad(router, ((0, 0), (0, LANES - ne)))
    return pl.pallas_call(
        _nm_router_body, name="norm_mod_router", grid=(b, l // tm),
        in_specs=in_specs + [pl.BlockSpec((d, LANES), lambda bi, r: (0, 0))],
        out_specs=[o_spec, pl.BlockSpec((1, tm, LANES), lambda bi, r: (bi, r, 0))],
        out_shape=[jax.ShapeDtypeStruct((b, l, d), F32),
                   jax.ShapeDtypeStruct((b, l, LANES), F32)],
        compiler_params=_cparams(2),
    )(h, gain.reshape(1, d), sh, sc, router_pad)


def _combine_norm_body(h_ref, ya_ref, yb_ref, gt_ref, g2_ref, gain_ref, o_ref):
    gt = gt_ref[...]
    y = ya_ref[...] * gt[:, 0:1] + yb_ref[...] * gt[:, 1:2]
    v = h_ref[...] + g2_ref[0] * y
    ms = jnp.mean(v * v, axis=-1, keepdims=True)
    o_ref[...] = v * lax.rsqrt(ms + NORM_EPS) * gain_ref[...]


def _combine_norm(h, ya, yb, gate, g2, gain, rows_per_gate, tm=256):
    m, d = h.shape
    gt = jnp.pad(gate, ((0, 0), (0, LANES - gate.shape[1])))
    tiles_per_gate = rows_per_gate // tm
    row = pl.BlockSpec((tm, d), lambda i: (i, 0))
    return pl.pallas_call(
        _combine_norm_body, name="combine_norm", grid=(m // tm,),
        in_specs=[row, row, row, pl.BlockSpec((tm, LANES), lambda i: (i, 0)),
                  pl.BlockSpec((1, 1, d), lambda i: (i // tiles_per_gate, 0, 0)),
                  pl.BlockSpec((1, d), lambda i: (0, 0))],
        out_specs=row,
        out_shape=jax.ShapeDtypeStruct((m, d), F32),
        compiler_params=_cparams(1),
    )(h, ya, yb, gt, g2, gain.reshape(1, d))


def _krope_body(z_ref, cos_ref, sin_ref, o_ref):
    z = z_ref[...]
    rot = z[:, :QK_ROPE] * cos_ref[...] + z[:, QK_ROPE:] * sin_ref[...]
    o_ref[...] = jnp.concatenate([rot, jnp.zeros_like(rot)], axis=1).astype(BF16)


def _krope(z_all, cos_t, sin_t, col_block, tm=256):
    m = z_all.shape[0]
    t = cos_t.shape[0]
    tiles_per_batch = t // tm
    return pl.pallas_call(
        _krope_body, name="krope", grid=(m // tm,),
        in_specs=[pl.BlockSpec((tm, LANES), lambda i: (i, col_block)),
                  pl.BlockSpec((tm, QK_ROPE), lambda i: (i % tiles_per_batch, 0)),
                  pl.BlockSpec((tm, QK_ROPE), lambda i: (i % tiles_per_batch, 0))],
        out_specs=pl.BlockSpec((tm, LANES), lambda i: (i, 0)),
        out_shape=jax.ShapeDtypeStruct((m, LANES), BF16),
        compiler_params=_cparams(1),
    )(z_all, cos_t, sin_t)


def _attn_body(q_ref, kn_ref, v_ref, kr_ref, cos_ref, sin_ref, o_ref, kfull_ref, *, scale):
    qi = pl.program_id(2)

    @pl.when(qi == 0)
    def _():
        kfull_ref[:, :QK_NOPE] = kn_ref[0]
        kfull_ref[:, QK_NOPE:] = kr_ref[0]

    q = q_ref[...].astype(F32)
    qn = q[:, :QK_NOPE]
    qr = (q[:, QK_NOPE:QK_NOPE + QK_ROPE] * cos_ref[...]
          + q[:, QK_NOPE + QK_ROPE:] * sin_ref[...])
    qf = (jnp.concatenate([qn, qr, jnp.zeros_like(qr)], axis=1) * scale).astype(BF16)
    s = lax.dot_general(qf, kfull_ref[...], (((1,), (1,)), ((), ())),
                        preferred_element_type=F32)
    m = jnp.max(s, axis=-1, keepdims=True)
    p = jnp.exp2(s - m)
    l = jnp.sum(p, axis=-1, keepdims=True)
    o = jnp.dot(p.astype(BF16), v_ref[0], preferred_element_type=F32)
    o_ref[...] = (o / l).astype(BF16)


def _attention(q, kv, kr, cos_q, sin_q, batch, tq=256):
    m = q.shape[0]
    l = m // batch
    t = kv.shape[1]
    nq = l // tq
    scale = float((QK_NOPE + QK_ROPE) ** -0.5 * np.log2(np.e))
    hw = QK_NOPE + 2 * QK_ROPE
    return pl.pallas_call(
        functools.partial(_attn_body, scale=scale), name="attention",
        grid=(batch, MLA_HEADS, nq),
        in_specs=[
            pl.BlockSpec((tq, hw), lambda b, h, i: (b * nq + i, h)),
            pl.BlockSpec((1, t, QK_NOPE), lambda b, h, i: (b, 0, 2 * h)),
            pl.BlockSpec((1, t, V_HEAD), lambda b, h, i: (b, 0, 2 * h + 1)),
            pl.BlockSpec((1, t, LANES), lambda b, h, i: (b, 0, 0)),
            pl.BlockSpec((tq, QK_ROPE), lambda b, h, i: (i, 0)),
            pl.BlockSpec((tq, QK_ROPE), lambda b, h, i: (i, 0)),
        ],
        out_specs=pl.BlockSpec((tq, V_HEAD), lambda b, h, i: (b * nq + i, h)),
        out_shape=jax.ShapeDtypeStruct((m, MLA_HEADS * V_HEAD), BF16),
        scratch_shapes=[pltpu.VMEM((t, QK_NOPE + LANES), BF16)],
        compiler_params=_cparams(3),
    )(q, kv, kv, kr, cos_q, sin_q)


def _sgu_body(u_ref, v_ref, lw_ref, lb_ref, ws_ref, bs_ref, o_ref, vn_ref):
    v = v_ref[...].astype(F32)
    mu = jnp.mean(v, axis=-1, keepdims=True)
    vc = v - mu
    var = jnp.mean(vc * vc, axis=-1, keepdims=True)
    vn_ref[...] = (vc * lax.rsqrt(var + 1e-5) * lw_ref[...] + lb_ref[...]).astype(BF16)
    n_chunks = v_ref.shape[0] // CHUNK
    for n in range(n_chunks):
        rows = slice(n * CHUNK, (n + 1) * CHUNK)
        for g in range(SG_GROUPS):
            cols = slice(g * LANES, (g + 1) * LANES)
            vm = jnp.dot(ws_ref[g], vn_ref[rows, cols], preferred_element_type=F32)
            vm = vm + bs_ref[:, cols]
            o_ref[rows, cols] = (u_ref[rows, cols].astype(F32) * vm).astype(BF16)


def _spatial_gate(hg, ln_w, ln_b, w_s, b_s, tm=256):
    m = hg.shape[0]
    d = hg.shape[1] // 2
    bs_full = jnp.repeat(b_s.T, d // SG_GROUPS, axis=1)
    return pl.pallas_call(
        _sgu_body, name="spatial_gate", grid=(m // tm,),
        in_specs=[
            pl.BlockSpec((tm, d), lambda i: (i, 0)),
            pl.BlockSpec((tm, d), lambda i: (i, 1)),
            pl.BlockSpec((1, d), lambda i: (0, 0)),
            pl.BlockSpec((1, d), lambda i: (0, 0)),
            pl.BlockSpec((SG_GROUPS, CHUNK, CHUNK), lambda i: (0, 0, 0)),
            pl.BlockSpec((CHUNK, d), lambda i: (0, 0)),
        ],
        out_specs=pl.BlockSpec((tm, d), lambda i: (i, 0)),
        out_shape=jax.ShapeDtypeStruct((m, d), BF16),
        scratch_shapes=[pltpu.VMEM((tm, d), BF16)],
        compiler_params=_cparams(1),
    )(hg, hg, ln_w.reshape(1, d), ln_b.reshape(1, d), w_s.astype(BF16), bs_full)


def _split_bf16(x, pieces):
    out = []
    for _ in range(pieces):
        p = x.astype(BF16)
        out.append(p)
        x = x - p.astype(F32)
    return out


def _dot_exact_rhs(x, w, pieces=2):
    return sum(jnp.dot(p, w, preferred_element_type=F32) for p in _split_bf16(x, pieces))


def _dot_split(x, w_hi, w_lo):
    x_hi, x_lo = _split_bf16(x, 2)
    return (jnp.dot(x_hi, w_hi, preferred_element_type=F32)
            + (jnp.dot(x_hi, w_lo, preferred_element_type=F32)
               + jnp.dot(x_lo, w_hi, preferred_element_type=F32)))


def _hi_lo(w):
    hi = w.astype(BF16)
    return hi, (w - hi.astype(F32)).astype(BF16)


def _prep_body(z_ref, zp_ref, zn_ref, mup_ref, mun_ref, kk_ref, ka_ref, w0_ref, a0_ref,
               wup_ref, aup_ref, gup_ref, rk_ref, e_ref, et_ref,
               lw0_o, lw1_o, k0_o, k1_o, b0_o, b1_o, v_o, kkn_o, r_o, g_o, bonus_o,
               *, tiles_ctx, tiles_total):
    tm = z_ref.shape[0]
    rt = pl.program_id(0) % tiles_total
    in_ctx = rt < tiles_ctx
    has_prev = jnp.where(in_ctx, rt > 0, rt > tiles_ctx)
    has_next = jnp.where(in_ctx, rt < tiles_ctx - 1, rt < tiles_total - 1)

    z = z_ref[...]
    rows = lax.broadcasted_iota(jnp.int32, z.shape, 0)
    prev_row = jnp.where(has_prev, zp_ref[7:8, :], 0.0)
    next_row = jnp.where(has_next, zn_ref[0:1, :], 0.0)
    z_prev = jnp.where(rows == 0, prev_row, pltpu.roll(z, 1, 0))
    z_next = jnp.where(rows == tm - 1, next_row, pltpu.roll(z, tm - 1, 0))
    zs = z + mup_ref[...] * (z_prev - z) + mun_ref[...] * (z_next - z)

    d = RWKV_DIM
    r = zs[:, 0:d]
    k = zs[:, d:2 * d]
    v = zs[:, 2 * d:3 * d]
    wd = zs[:, 3 * d:3 * d + LANES]
    ad = zs[:, 3 * d + LANES:3 * d + 2 * LANES]
    gd = zs[:, 3 * d + 2 * LANES:3 * d + 4 * LANES]

    def head_sum(x):
        return _dot_exact_rhs(_dot_exact_rhs(x, e_ref[...]), et_ref[...])

    kk = k * kk_ref[...]
    nrm = jnp.sqrt(head_sum(kk * kk))
    kkn = kk / jnp.maximum(nrm, 1e-12)

    wx = w0_ref[...] + _dot_split(jnp.tanh(wd), wup_ref[0], wup_ref[1])
    lw = -float(np.exp(-0.5)) * jax.nn.sigmoid(wx)
    ax = a0_ref[...] + _dot_split(ad, aup_ref[0], aup_ref[1])
    iclr = jax.nn.sigmoid(ax)
    g = _dot_split(jax.nn.sigmoid(gd), gup_ref[0], gup_ref[1])

    ka = ka_ref[...]
    k0 = k * (1.0 + (iclr[:, :d] - 1.0) * ka)
    k1 = k * (1.0 + (iclr[:, d:] - 1.0) * ka)
    lw0_o[...] = lw[:, :d]
    lw1_o[...] = lw[:, d:]
    k0_o[...] = k0
    k1_o[...] = k1
    b0_o[...] = iclr[:, :d] * kkn
    b1_o[...] = iclr[:, d:] * kkn
    v_o[...] = v
    kkn_o[...] = kkn
    r_o[...] = r
    g_o[...] = g
    bonus_o[...] = head_sum(r * (k0 + k1) * rk_ref[...]) * v


def _head_indicator():
    head_of = jnp.arange(RWKV_DIM) // RWKV_HEAD
    return (head_of[:, None] == jnp.arange(LANES)[None, :]).astype(F32)


def _rwkv_prep(z_rw, p, rows_ctx, rows_total):
    m, w = z_rw.shape
    tm = PREP_TM
    d = RWKV_DIM
    assert rows_ctx % tm == 0 and rows_total % tm == 0 and m % rows_total == 0
    pad = w - RWKV_IN
    mup = jnp.pad(p['mu_prev'], (0, pad)).reshape(1, w)
    mun = jnp.pad(p['mu_next'], (0, pad)).reshape(1, w)
    zero = jnp.zeros((DECAY_LORA, d), F32)
    wup = jnp.concatenate([jnp.concatenate([p['w_up'][0], zero], axis=1),
                           jnp.concatenate([zero, p['w_up'][1]], axis=1)], axis=0)
    aup = jnp.concatenate([jnp.concatenate([p['a_up'][0], zero], axis=1),
                           jnp.concatenate([zero, p['a_up'][1]], axis=1)], axis=0)
    gup = jnp.pad(p['g_up'], ((0, 2 * LANES - GATE_LORA), (0, 0)))
    wup, aup, gup = (jnp.stack(_hi_lo(w)) for w in (wup, aup, gup))
    e = _head_indicator().astype(BF16)

    def full(shape):
        return pl.BlockSpec(shape, lambda i: (0,) * len(shape))

    n8 = m // 8
    out_spec = pl.BlockSpec((tm, d), lambda i: (i, 0))
    out_sds = jax.ShapeDtypeStruct((m, d), F32)
    return pl.pallas_call(
        functools.partial(_prep_body, tiles_ctx=rows_ctx // tm, tiles_total=rows_total // tm),
        name="rwkv_prep",
        grid=(m // tm,),
        in_specs=[
            pl.BlockSpec((tm, w), lambda i: (i, 0)),
            pl.BlockSpec((8, w), lambda i: (jnp.maximum(i * (tm // 8) - 1, 0), 0)),
            pl.BlockSpec((8, w), lambda i: (jnp.minimum((i + 1) * (tm // 8), n8 - 1), 0)),
            full((1, w)), full((1, w)), full((1, d)), full((1, d)), full((1, 2 * d)),
            full((1, 2 * d)), full((2, LANES, 2 * d)), full((2, LANES, 2 * d)),
            full((2, 2 * LANES, d)), full((1, d)), full((d, LANES)), full((LANES, d)),
        ],
        out_specs=[out_spec] * 11,
        out_shape=[out_sds] * 11,
        compiler_params=_cparams(1),
    )(z_rw, z_rw, z_rw, mup, mun, p['k_k'].reshape(1, d), p['k_a'].reshape(1, d),
      p['w0'].reshape(1, 2 * d), p['a0'].reshape(1, 2 * d), wup, aup, gup,
      p['r_k'].reshape(1, d), e, e.T)


def _bdot(x, y):
    return jnp.dot(x.astype(BF16), y.astype(BF16), preferred_element_type=F32)


def _chunk_units(units):
    c = RW_CHUNK
    row = lax.broadcasted_iota(jnp.int32, (c, LANES), 0)
    lane = lax.broadcasted_iota(jnp.int32, (c, LANES), 1)
    pos = lane % c
    lo = lane < c
    tr = lax.broadcasted_iota(jnp.int32, (c, c), 0)
    tc = lax.broadcasted_iota(jnp.int32, (c, c), 1)
    tri = {False: (tc <= tr).astype(BF16), True: (tc >= tr).astype(BF16)}
    strict = {False: pos < row, True: pos > row}
    incl = {False: pos <= row, True: pos >= row}
    last = {False: c - 1, True: 0}
    rev = [u[6] for u in units]
    nu = range(len(units))

    def sb(x):
        return jnp.concatenate([jnp.where(lo, x, 0.0), jnp.where(lo, 0.0, x)], axis=0)

    def nt(x, y):
        return lax.dot_general(x.astype(BF16), y.astype(BF16), (((1,), (1,)), ((), ())),
                               preferred_element_type=F32)

    def tn(x, y):
        return lax.dot_general(x.astype(BF16), y.astype(BF16), (((0,), (0,)), ((), ())),
                               preferred_element_type=F32)

    def fold(x):
        return jnp.where(lo, x[:c], 0.0) + jnp.where(lo, 0.0, x[c:])

    lw3 = [_split_bf16(units[i][0], 3) for i in nu]
    lc = [sum(jnp.dot(tri[rev[i]], piece, preferred_element_type=F32) for piece in lw3[i])
          for i in nu]
    ltot = [lc[i][last[rev[i]]:last[rev[i]] + 1, :] for i in nu]
    e_neg = [jnp.exp(-lc[i]) for i in nu]
    e_h = [jnp.exp(ltot[i] - lc[i]) for i in nu]
    at = [-units[i][3] * jnp.exp(lc[i] - units[i][0]) for i in nu]
    rt = [units[i][5] * jnp.exp(lc[i]) for i in nu]
    bt = [units[i][2] * e_neg[i] for i in nu]
    kt = [units[i][1] * e_neg[i] for i in nu]
    bh = [units[i][2] * e_h[i] for i in nu]
    kh = [units[i][1] * e_h[i] for i in nu]

    ar = [jnp.concatenate([at[i], rt[i]], axis=0) for i in nu]
    m_b = [nt(ar[i], sb(bt[i])) for i in nu]
    m_k = [nt(ar[i], sb(kt[i])) for i in nu]
    n = [jnp.where(strict[rev[i]], m_b[i][:c], 0.0) for i in nu]
    m_ak = [jnp.where(strict[rev[i]], m_k[i][:c], 0.0) for i in nu]
    m_rb = [jnp.where(incl[rev[i]], m_b[i][c:], 0.0) for i in nu]
    m_rk = [jnp.where(incl[rev[i]], m_k[i][c:], 0.0) for i in nu]
    mv = [_bdot(jnp.concatenate([m_ak[i], m_rk[i]], axis=0), sb(units[i][4])) for i in nu]
    xa = at
    xu = [mv[i][:c] for i in nu]
    n_steps = c.bit_length() - 1
    for step in range(n_steps):
        upd = [_bdot(n[i], jnp.concatenate([sb(xa[i]), sb(xu[i])], axis=1)) for i in nu]
        xa = [xa[i] + upd[i][:, :LANES] for i in nu]
        xu = [xu[i] + upd[i][:, LANES:] for i in nu]
        if step < n_steps - 1:
            n = [_bdot(n[i], sb(n[i])) for i in nu]
    rx = [_bdot(m_rb[i], jnp.concatenate([sb(xa[i]), sb(xu[i])], axis=1)) for i in nu]
    gh = [tn(bh[i], jnp.concatenate([xa[i], xu[i]], axis=1)) for i in nu]
    hk = [tn(kh[i], units[i][4]) for i in nu]
    out = []
    for i in nu:
        rp = rt[i] + rx[i][:, :LANES]
        y0 = rx[i][:, LANES:] + mv[i][c:]
        g = fold(gh[i][:, :LANES]) + jnp.where(pos == row, jnp.exp(ltot[i]), 0.0)
        h = fold(gh[i][:, LANES:] + hk[i])
        out.append((g, h, rp, y0))
    return out


def _chunk_body(lw0_ref, lw1_ref, k0_ref, k1_ref, b0_ref, b1_ref, v_ref, kkn_ref, r_ref,
                g0_o, h0_o, rp0_o, y00_o, g1_o, h1_o, rp1_o, y01_o):
    n_chunks = v_ref.shape[0] // RW_CHUNK
    units = []
    for ci in range(n_chunks):
        rows = slice(ci * RW_CHUNK, (ci + 1) * RW_CHUNK)
        v = v_ref[rows, :]
        kkn = kkn_ref[rows, :]
        r = r_ref[rows, :]
        units.append((lw0_ref[rows, :], k0_ref[rows, :], b0_ref[rows, :], kkn, v, r, False))
        units.append((lw1_ref[rows, :], k1_ref[rows, :], b1_ref[rows, :], kkn, v, r, True))
    res = _chunk_units(units)
    for ci in range(n_chunks):
        rows = slice(ci * RW_CHUNK, (ci + 1) * RW_CHUNK)
        for dr, outs in enumerate(((g0_o, h0_o, rp0_o, y00_o), (g1_o, h1_o, rp1_o, y01_o))):
            for o_ref, val in zip(outs, res[2 * ci + dr]):
                o_ref[rows, :] = val


def _rwkv_chunk(prep, tm=512):
    lw0, lw1, k0, k1, b0, b1, v, kkn, r = prep
    m, d = v.shape
    spec = pl.BlockSpec((tm, LANES), lambda i, pr: (i, pr))
    sds = jax.ShapeDtypeStruct((m, d), F32)
    return pl.pallas_call(
        _chunk_body, name="rwkv_chunk", grid=(m // tm, d // LANES),
        in_specs=[spec] * 9, out_specs=[spec] * 8, out_shape=[sds] * 8,
        compiler_params=_cparams(2),
    )(lw0, lw1, k0, k1, b0, b1, v, kkn, r)


def _scan_body(gf_ref, hf_ref, rpf_ref, y0f_ref, gb_ref, hb_ref, rpb_ref, y0b_ref,
               yf_ref, yb_ref, s_ref):
    c = RW_CHUNK
    step = pl.program_id(1)

    @pl.when(step == 0)
    def _():
        s_ref[...] = jnp.zeros(s_ref.shape, F32)

    lane = lax.broadcasted_iota(jnp.int32, (c, LANES), 1)
    lo = lane < c

    def sb(x):
        return jnp.concatenate([jnp.where(lo, x, 0.0), jnp.where(lo, 0.0, x)], axis=0)

    dirs = ((gf_ref, hf_ref, rpf_ref, y0f_ref, yf_ref), (gb_ref, hb_ref, rpb_ref, y0b_ref, yb_ref))
    cols = [slice(pr * LANES, (pr + 1) * LANES) for pr in range(N_PAIRS)]
    out = []
    for dr, (g_ref, _, rp_ref, _, _) in enumerate(dirs):
        lhs = [jnp.concatenate([rp_ref[:, cl], sb(g_ref[:, cl])], axis=0) for cl in cols]
        out.append([_bdot(lhs[pr], s_ref[dr, pr]) for pr in range(N_PAIRS)])
    for dr, (_, h_ref, _, y0_ref, y_ref) in enumerate(dirs):
        for pr, cl in enumerate(cols):
            y_ref[:, cl] = out[dr][pr][:c] + y0_ref[:, cl]
            s_ref[dr, pr] = out[dr][pr][c:] + sb(h_ref[:, cl])


def _rwkv_scan(mats, batch, chunks_ctx):
    m, d = mats[0].shape
    c = RW_CHUNK
    n_chunks = m // c // batch

    def chunk_of(b, s, reverse):
        if not reverse:
            return b * n_chunks + s
        rev = jnp.where(s < chunks_ctx, chunks_ctx - 1 - s, n_chunks - 1 - (s - chunks_ctx))
        return b * n_chunks + rev

    fwd = pl.BlockSpec((c, d), lambda b, s: (chunk_of(b, s, False), 0))
    bwd = pl.BlockSpec((c, d), lambda b, s: (chunk_of(b, s, True), 0))
    sds = jax.ShapeDtypeStruct((m, d), F32)
    return pl.pallas_call(
        _scan_body, name="rwkv_scan", grid=(batch, n_chunks),
        in_specs=[fwd] * 4 + [bwd] * 4, out_specs=[fwd, bwd], out_shape=[sds, sds],
        scratch_shapes=[pltpu.VMEM((2, N_PAIRS, LANES, LANES), F32)],
        compiler_params=_cparams(2),
    )(*mats)


def _readout_body(y0_ref, y1_ref, bonus_ref, g_ref, att_ref, lw_ref, lb_ref, e_ref, et_ref, o_ref):
    def head_mean(x):
        return _dot_exact_rhs(_dot_exact_rhs(x, e_ref[...]), et_ref[...]) * (1.0 / RWKV_HEAD)

    y = y0_ref[...] + y1_ref[...]
    mu = head_mean(y)
    yc = y - mu
    var = head_mean(yc * yc)
    yn = yc * lax.rsqrt(var + LNX_EPS) * lw_ref[...] + lb_ref[...]
    d_att = att_ref.shape[1]
    o_ref[:, :d_att] = att_ref[...]
    o_ref[:, d_att:] = ((yn + bonus_ref[...]) * g_ref[...]).astype(BF16)


def _rwkv_readout(y0, y1, bonus, g, att, p, rows_ctx, rows_total, tm=256):
    d = RWKV_DIM
    n_lat, d_att = att.shape
    assert rows_ctx % tm == 0 and rows_total % tm == 0
    per_b = (rows_total - rows_ctx) // tm
    lat0 = rows_ctx // tm
    tot = rows_total // tm
    e = _head_indicator().astype(BF16)
    rw_spec = pl.BlockSpec((tm, d), lambda i: ((i // per_b) * tot + lat0 + i % per_b, 0))

    def const(shape):
        return pl.BlockSpec(shape, lambda i: (0, 0))

    return pl.pallas_call(
        _readout_body, name="rwkv_readout", grid=(n_lat // tm,),
        in_specs=[rw_spec, rw_spec, rw_spec, rw_spec,
                  pl.BlockSpec((tm, d_att), lambda i: (i, 0)),
                  const((1, d)), const((1, d)), const((d, LANES)), const((LANES, d))],
        out_specs=pl.BlockSpec((tm, d_att + d), lambda i: (i, 0)),
        out_shape=jax.ShapeDtypeStruct((n_lat, d_att + d), BF16),
        compiler_params=_cparams(1),
    )(y0, y1, bonus, g, att, p['lnx_w'].reshape(1, d), p['lnx_b'].reshape(1, d), e, e.T)


def _rwkv_mixer(z_rw, att, p, batch, rows_ctx, rows_total):
    outs = _rwkv_prep(z_rw, p, rows_ctx, rows_total)
    g, bonus = outs[9], outs[10]
    mats = _rwkv_chunk(outs[:9])
    y0, y1 = _rwkv_scan(mats, batch, rows_ctx // RW_CHUNK)
    return _rwkv_readout(y0, y1, bonus, g, att, p, rows_ctx, rows_total)


def _rope_swap_cols(w):
    f = ROPE_FREQS
    parts = []
    for a in range(2):
        x1 = w[..., (2 * a) * f:(2 * a + 1) * f]
        x2 = w[..., (2 * a + 1) * f:(2 * a + 2) * f]
        parts += [-x2, x1]
    return jnp.concatenate(parts, axis=-1)


def _rope_tables(l):
    pos = np.arange(l)
    inv = ROPE_THETA ** (-np.arange(ROPE_FREQS, dtype=np.float64) / ROPE_FREQS)
    ar = (pos // GRID_W)[:, None] * inv
    ac = (pos % GRID_W)[:, None] * inv
    cos = np.concatenate([np.cos(ar), np.cos(ar), np.cos(ac), np.cos(ac)], axis=1)
    sin = np.concatenate([np.sin(ar), np.sin(ar), np.sin(ac), np.sin(ac)], axis=1)
    return cos.astype(np.float32), sin.astype(np.float32)


GATHER_UNROLL = 8


def _gather_body(tok_ref, tv_ref, x_hbm, o_ref, buf_ref, sem):
    i = pl.program_id(0)
    n_tiles = pl.num_programs(0)
    tm = o_ref.shape[0]

    def row_copy(tile, r):
        slot = tile % 2
        return pltpu.make_async_copy(x_hbm.at[pl.ds(tok_ref[tile * tm + r], 1), :],
                                     buf_ref.at[slot, pl.ds(r, 1), :], sem.at[slot])

    def start_tile(tile):
        def body(g, carry):
            for k in range(GATHER_UNROLL):
                row_copy(tile, g * GATHER_UNROLL + k).start(priority=k % 2)
            return carry
        lax.fori_loop(0, tm // GATHER_UNROLL, body, 0)

    def wait_tile(tile):
        def body(g, carry):
            for k in range(GATHER_UNROLL):
                row_copy(tile, g * GATHER_UNROLL + k).wait()
            return carry
        lax.fori_loop(0, tm // GATHER_UNROLL, body, 0)

    @pl.when(tv_ref[i] == 1)
    def _():
        @pl.when(i == 0)
        def _():
            start_tile(i)

        nxt = jnp.minimum(i + 1, n_tiles - 1)

        @pl.when(jnp.logical_and(i + 1 < n_tiles, tv_ref[nxt] == 1))
        def _():
            start_tile(i + 1)

        wait_tile(i)
        o_ref[...] = buf_ref[i % 2].astype(BF16)

    @pl.when(tv_ref[i] == 0)
    def _():
        o_ref[...] = jnp.zeros(o_ref.shape, o_ref.dtype)


def _gather_rows_bf16(x, row_tok, tile_valid, tm):
    d = x.shape[1]
    n_rows = row_tok.shape[0]
    gs = pltpu.PrefetchScalarGridSpec(
        num_scalar_prefetch=2,
        grid=(n_rows // tm,),
        in_specs=[pl.BlockSpec(memory_space=pl.ANY)],
        out_specs=pl.BlockSpec((tm, d), lambda i, tok, tv: (i, 0)),
        scratch_shapes=[pltpu.VMEM((2, tm, d), F32), pltpu.SemaphoreType.DMA((2,))])
    return pl.pallas_call(
        _gather_body, grid_spec=gs, name="moe_gather",
        out_shape=jax.ShapeDtypeStruct((n_rows, d), BF16),
        compiler_params=_cparams(1),
    )(row_tok, tile_valid, x)


def _moe(xn, logits, w1, w3, w2):
    n_tok = xn.shape[0]
    tm = MOE_TM
    top_val, top_idx = lax.top_k(logits, TOP_K)
    gate = jax.nn.softmax(top_val, axis=-1)
    n_assign = n_tok * TOP_K
    e_flat = top_idx.reshape(-1).astype(jnp.int32)
    experts = jnp.arange(N_EXPERTS, dtype=jnp.int32)
    order = jnp.argsort(e_flat).astype(jnp.int32)
    rank = jnp.argsort(order).astype(jnp.int32)
    counts = jnp.sum((e_flat[:, None] == experts[None, :]).astype(jnp.int32), axis=0)
    padded = (counts + tm - 1) // tm * tm
    pad_end = jnp.cumsum(padded)
    pad_start = pad_end - padded
    grp_start = jnp.cumsum(counts) - counts
    pos = (pad_start[e_flat] + rank - grp_start[e_flat]).reshape(n_tok, TOP_K)
    n_tiles = n_assign // tm + N_EXPERTS
    n_rows = n_tiles * tm

    def expert_of(row):
        return jnp.minimum(jnp.sum((row[:, None] >= pad_end[None, :]).astype(jnp.int32), axis=1),
                           N_EXPERTS - 1)

    rows = jnp.arange(n_rows, dtype=jnp.int32)
    e_row = expert_of(rows)
    off = rows - pad_start[e_row]
    src = jnp.clip(grp_start[e_row] + off, 0, n_assign - 1)
    row_tok = jnp.where(off < counts[e_row], order[src] // TOP_K, 0)
    tile_start = jnp.arange(n_tiles, dtype=jnp.int32) * tm
    n_valid = pad_end[-1] // tm
    tile_src = jnp.minimum(jnp.arange(n_tiles, dtype=jnp.int32), n_valid - 1).astype(jnp.int32)
    tile_exp = expert_of(tile_start)[tile_src]
    tile_valid = (tile_start < pad_end[-1]).astype(jnp.int32)
    group = (tile_exp, tile_valid, tile_src)

    def rows_of(a, idx):
        return a.at[idx].get(mode="promise_in_bounds")

    xs = _gather_rows_bf16(xn, row_tok.astype(jnp.int32), tile_valid, tm)
    hmid = _matmul(xs, [w1, w3], name="moe_up", tm=tm, tn=512, out_dtype=BF16,
                   epilogue="swiglu", group=group)
    ys = _matmul(hmid, [w2], name="moe_down", tm=tm, tn=512, out_dtype=F32, group=group,
                 single_buffer_w=True)
    return rows_of(ys, pos[:, 0]), rows_of(ys, pos[:, 1]), gate


def kernel(x, c, ctx, c_ctx, l0_ada_w, l0_ada_b, l0_norm1, l0_norm2, l0_w_in, l0_q_norm, l0_w_uq, l0_kv_norm, l0_w_ukv, l0_mu_prev, l0_mu_next, l0_w0, l0_w_up, l0_a0, l0_a_up, l0_g_up, l0_k_k, l0_k_a, l0_r_k, l0_lnx_w, l0_lnx_b, l0_w_o, l0_ffn_w1, l0_ffn_w3, l0_ffn_w2, l1_ada_w, l1_ada_b, l1_norm1, l1_norm2, l1_w_in, l1_v_ln_w, l1_v_ln_b, l1_w_s, l1_b_s, l1_w_o, l1_router, l1_moe_w1, l1_moe_w3, l1_moe_w2, final_norm):
    b, l, d = x.shape
    lc = ctx.shape[1]
    t = lc + l
    n_tok = b * l

    cond = jnp.zeros((8, d), F32).at[:b].set(c).at[b].set(c_ctx)
    mod0 = _matmul(cond, [l0_ada_w], name="ada_mod", tm=8, tn=1536, out_dtype=F32, prologue="silu",
                   epilogue="bias", bias=l0_ada_b)
    mod1 = _matmul(cond, [l1_ada_w], name="ada_mod", tm=8, tn=1536, out_dtype=F32, prologue="silu",
                   epilogue="bias", bias=l1_ada_b)

    def mods(mod, row0, nrows):
        return [mod[row0:row0 + nrows, i * d:(i + 1) * d].reshape(nrows, 1, d) for i in range(N_MOD)]

    sh1, sc1, g1, sh2, sc2, g2 = mods(mod0, 0, b)
    csh1, csc1 = mods(mod0, b, 1)[:2]

    xall = _norm_mod_merge(x, ctx, l0_norm1, sh1, sc1, csh1, csc1)
    mla_in = Q_LORA + KV_LORA + QK_ROPE
    w_kr = l0_w_in[:, Q_LORA + KV_LORA:mla_in]
    w_mla = jnp.concatenate([l0_w_in[:, :mla_in], _rope_swap_cols(w_kr)], axis=1)
    w_rw = jnp.pad(l0_w_in[:, mla_in:], ((0, 0), (0, RWKV_PAD - RWKV_IN)))
    xall2 = xall.reshape(b * t, d)
    z_all = _matmul(xall2, [w_mla], name="in_mla", tm=1088, tn=384, out_dtype=F32)
    z_rw = _matmul(xall2, [w_rw], name="in_rwkv", tm=1088, tn=512, out_dtype=F32)

    wq = l0_w_uq.reshape(Q_LORA, MLA_HEADS, QK_NOPE + QK_ROPE)
    wq_ext = jnp.concatenate([wq, _rope_swap_cols(wq[..., QK_NOPE:])], axis=-1)
    wq_ext = wq_ext.reshape(Q_LORA, MLA_HEADS * (QK_NOPE + 2 * QK_ROPE))
    tq_rows = 256
    per_b = l // tq_rows
    lat0 = lc // tq_rows

    def lat_rows(i):
        return (i // per_b) * (t // tq_rows) + lat0 + i % per_b

    q = _matmul(z_all, [wq_ext], name="q_up", tm=tq_rows, tn=2048, out_dtype=BF16, k=Q_LORA, x_col_block=0,
                x_row_map=lat_rows, m_out=n_tok, prologue="rms", gain=l0_q_norm)
    kv = _matmul(z_all, [l0_w_ukv], name="kv_up", tm=1088, tn=1024, out_dtype=BF16, k=KV_LORA, x_col_block=1,
                 prologue="rms", gain=l0_kv_norm)
    cos_l, sin_l = _rope_tables(l)
    cos_t = np.concatenate([np.ones((lc, QK_ROPE), np.float32), cos_l], axis=0)
    sin_t = np.concatenate([np.zeros((lc, QK_ROPE), np.float32), sin_l], axis=0)
    kr = _krope(z_all, cos_t, sin_t, col_block=(Q_LORA + KV_LORA) // LANES)
    att = _attention(q, kv.reshape(b, t, -1), kr.reshape(b, t, LANES), cos_l, sin_l, b)

    p0 = dict(mu_prev=l0_mu_prev, mu_next=l0_mu_next, w0=l0_w0, w_up=l0_w_up, a0=l0_a0,
              a_up=l0_a_up, g_up=l0_g_up, k_k=l0_k_k, k_a=l0_k_a, r_k=l0_r_k,
              lnx_w=l0_lnx_w, lnx_b=l0_lnx_b)
    mix = _rwkv_mixer(z_rw, att, p0, b, lc, t)
    h = _matmul(mix, [l0_w_o], name="mix_out", tm=1024, tn=512, out_dtype=F32, epilogue="resid",
                resid=x.reshape(n_tok, d), gate=g1, rows_per_gate=l)

    xn = _norm_mod(h.reshape(b, l, d), l0_norm2, sh2, sc2).reshape(n_tok, d)
    hmid = _matmul(xn, [l0_ffn_w1, l0_ffn_w3], name="ffn_up", tm=1024, tn=512, out_dtype=BF16,
                   epilogue="swiglu")
    h = _matmul(hmid, [l0_ffn_w2], name="ffn_down", tm=512, tn=512, out_dtype=F32, epilogue="resid",
                resid=h, gate=g2, rows_per_gate=l, single_buffer_w=True)

    sh1, sc1, g1, sh2, sc2, g2 = mods(mod1, 0, b)
    xn = _norm_mod(h.reshape(b, l, d), l1_norm1, sh1, sc1).reshape(n_tok, d)
    hg = _matmul(xn, [l1_w_in], name="gmlp_in", tm=1024, tn=512, out_dtype=BF16, epilogue="gelu")
    gated = _spatial_gate(hg, l1_v_ln_w, l1_v_ln_b, l1_w_s, l1_b_s)
    h = _matmul(gated, [l1_w_o], name="gmlp_out", tm=1024, tn=512, out_dtype=F32, epilogue="resid",
                resid=h, gate=g1, rows_per_gate=l)

    xn, logits = _norm_mod(h.reshape(b, l, d), l1_norm2, sh2, sc2, router=l1_router)
    ya, yb, gate = _moe(xn.reshape(n_tok, d), logits.reshape(n_tok, LANES)[:, :N_EXPERTS],
                        l1_moe_w1, l1_moe_w3, l1_moe_w2)
    return _combine_norm(h, ya, yb, gate, g2, final_norm, l).reshape(b, l, d)
```

```python
import functools

import jax
import jax.numpy as jnp
import numpy as np
from jax import lax
from jax.experimental import pallas as pl
from jax.experimental.pallas import tpu as pltpu

F32 = jnp.float32
BF16 = jnp.bfloat16

D_MODEL = 2048
GRID_W = 64
N_MOD = 6
NORM_EPS = 1e-6
V_HEAD = 128
MLA_HEADS = 8
Q_LORA = 512
KV_LORA = 512
QK_NOPE = 128
QK_ROPE = 64
ROPE_FREQS = 16
ROPE_THETA = 10000.0
RWKV_HEAD = 64
RWKV_HEADS = 16
RWKV_DIM = 1024
DECAY_LORA = 64
ICLR_LORA = 64
GATE_LORA = 160
LNX_EPS = 64e-5
RWKV_IN = 3 * RWKV_DIM + 2 * DECAY_LORA + 2 * ICLR_LORA + GATE_LORA
RWKV_PAD = 3584
RW_CHUNK = 64
PREP_TM = 128
N_PAIRS = RWKV_HEADS // 2
HI = lax.Precision.HIGHEST
CHUNK = 128
SG_GROUPS = 16
D_FF = 7168
N_EXPERTS = 8
TOP_K = 2

VMEM_LIMIT_BYTES = 56 * 1024 * 1024
LANES = 128

MOE_TM = 512


def _cparams(n_axes):
    return pltpu.CompilerParams(
        dimension_semantics=("arbitrary",) * n_axes,
        vmem_limit_bytes=VMEM_LIMIT_BYTES)


def _mm_body(*refs, n_w, prologue, epilogue, grouped):
    refs = list(refs)
    if grouped:
        te_ref, tv_ref, _ = refs[:3]
        refs = refs[3:]
    x_ref = refs.pop(0)
    w_refs = [refs.pop(0) for _ in range(n_w)]
    gain_ref = refs.pop(0) if prologue == "rms" else None
    bias_ref = refs.pop(0) if epilogue == "bias" else None
    if epilogue == "resid":
        resid_ref = refs.pop(0)
        gate_ref = refs.pop(0)
    o_ref, wbf_ref = refs

    i = pl.program_id(1)
    if grouped:
        new_w = jnp.logical_or(i == 0, te_ref[i] != te_ref[jnp.maximum(i - 1, 0)])
    else:
        new_w = i == 0

    @pl.when(new_w)
    def _():
        for n in range(n_w):
            wbf_ref[n] = w_refs[n][...].astype(BF16)

    def compute():
        x = x_ref[...]
        if prologue == "rms":
            xf = x.astype(F32)
            ms = jnp.mean(xf * xf, axis=-1, keepdims=True)
            x = xf * lax.rsqrt(ms + NORM_EPS) * gain_ref[...]
        elif prologue == "silu":
            xf = x.astype(F32)
            x = xf * jax.nn.sigmoid(xf)
        x = x.astype(BF16)
        acc = [jnp.dot(x, wbf_ref[n], preferred_element_type=F32) for n in range(n_w)]
        if epilogue == "swiglu":
            a = acc[0]
            out = a * jax.nn.sigmoid(a) * acc[1]
        elif epilogue == "gelu":
            a = acc[0]
            out = 0.5 * a * (1.0 + lax.erf(a * (2.0 ** -0.5)))
        elif epilogue == "bias":
            out = acc[0] + bias_ref[...]
        elif epilogue == "resid":
            out = resid_ref[...] + gate_ref[0] * acc[0]
        else:
            out = acc[0]
        o_ref[...] = out.astype(o_ref.dtype)

    if grouped:
        @pl.when(tv_ref[i] == 1)
        def _():
            compute()

        @pl.when(tv_ref[i] == 0)
        def _():
            o_ref[...] = jnp.zeros(o_ref.shape, o_ref.dtype)
    else:
        compute()


def _matmul(x, ws, *, name, tm, tn, out_dtype, k=None, x_col_block=0, x_row_map=None, m_out=None,
            prologue=None, gain=None, epilogue=None, bias=None, resid=None, gate=None,
            rows_per_gate=None, group=None, single_buffer_w=False):
    grouped = group is not None
    kdim = k if k is not None else x.shape[1]
    n = ws[0].shape[-1]
    m = m_out if m_out is not None else x.shape[0]
    assert m % tm == 0 and n % tn == 0, (m, tm, n, tn)
    n_w = len(ws)
    grid = (n // tn, m // tm)
    w_mode = dict(pipeline_mode=pl.Buffered(1)) if single_buffer_w else {}

    if grouped:
        def xmap(j, i, te, tv, ts):
            return (ts[i], x_col_block)

        def wmap(j, i, te, tv, ts):
            return (te[i], 0, j)

        def omap(j, i, te, tv, ts):
            return (i, j)
        w_spec = pl.BlockSpec((None, kdim, tn), wmap, **w_mode)
    else:
        def xmap(j, i):
            return ((x_row_map(i) if x_row_map is not None else i), x_col_block)

        def wmap(j, i):
            return (0, j)

        def omap(j, i):
            return (i, j)
        w_spec = pl.BlockSpec((kdim, tn), wmap, **w_mode)

    in_specs = [pl.BlockSpec((tm, kdim), xmap)] + [w_spec] * n_w
    args = [x] + list(ws)
    if prologue == "rms":
        in_specs.append(pl.BlockSpec((1, kdim), lambda j, i, *_: (0, 0)))
        args.append(gain.reshape(1, kdim))
    if epilogue == "bias":
        in_specs.append(pl.BlockSpec((1, tn), lambda j, i, *_: (0, j)))
        args.append(bias.reshape(1, n))
    if epilogue == "resid":
        in_specs.append(pl.BlockSpec((tm, tn), omap))
        args.append(resid)
        assert rows_per_gate % tm == 0
        tiles_per_gate = rows_per_gate // tm
        in_specs.append(pl.BlockSpec((1, 1, tn), lambda j, i, *_: (i // tiles_per_gate, 0, j)))
        args.append(gate)

    body = functools.partial(_mm_body, n_w=n_w, prologue=prologue, epilogue=epilogue,
                             grouped=grouped)
    gs = pltpu.PrefetchScalarGridSpec(
        num_scalar_prefetch=3 if grouped else 0,
        grid=grid,
        in_specs=in_specs,
        out_specs=pl.BlockSpec((tm, tn), omap),
        scratch_shapes=[pltpu.VMEM((n_w, kdim, tn), BF16)])
    call = pl.pallas_call(
        body, grid_spec=gs, name=name,
        out_shape=jax.ShapeDtypeStruct((m, n), out_dtype),
        compiler_params=_cparams(2))
    if grouped:
        return call(*group, *args)
    return call(*args)


def _norm_mod_math(v, g, sh, sc):
    ms = jnp.mean(v * v, axis=-1, keepdims=True)
    y = v * lax.rsqrt(ms + NORM_EPS) * g
    return y * (1.0 + sc) + sh


def _nm_merge_body(x_ref, c_ref, g_ref, sh_ref, sc_ref, csh_ref, csc_ref, o_ref):
    r = pl.program_id(1)

    @pl.when(r == 0)
    def _():
        o_ref[0] = _norm_mod_math(c_ref[0], g_ref[...], csh_ref[...], csc_ref[...]).astype(BF16)

    @pl.when(r > 0)
    def _():
        o_ref[0] = _norm_mod_math(x_ref[0], g_ref[...], sh_ref[0], sc_ref[0]).astype(BF16)


def _norm_mod_merge(x, ctx, gain, sh, sc, csh, csc):
    b, l, d = x.shape
    lc = ctx.shape[1]
    tm = lc
    assert l % tm == 0
    nt = l // tm + 1
    return pl.pallas_call(
        _nm_merge_body, name="norm_mod_merge",
        grid=(b, nt),
        in_specs=[
            pl.BlockSpec((1, tm, d), lambda bi, r: (bi, jnp.maximum(r - 1, 0), 0)),
            pl.BlockSpec((1, lc, d), lambda bi, r: (bi, 0, 0)),
            pl.BlockSpec((1, d), lambda bi, r: (0, 0)),
            pl.BlockSpec((1, 1, d), lambda bi, r: (bi, 0, 0)),
            pl.BlockSpec((1, 1, d), lambda bi, r: (bi, 0, 0)),
            pl.BlockSpec((1, d), lambda bi, r: (0, 0)),
            pl.BlockSpec((1, d), lambda bi, r: (0, 0)),
        ],
        out_specs=pl.BlockSpec((1, tm, d), lambda bi, r: (bi, r, 0)),
        out_shape=jax.ShapeDtypeStruct((b, lc + l, d), BF16),
        compiler_params=_cparams(2),
    )(x, ctx, gain.reshape(1, d), sh, sc, csh.reshape(1, d), csc.reshape(1, d))


def _nm_body(x_ref, g_ref, sh_ref, sc_ref, o_ref):
    o_ref[0] = _norm_mod_math(x_ref[0], g_ref[...], sh_ref[0], sc_ref[0]).astype(BF16)


def _nm_router_body(x_ref, g_ref, sh_ref, sc_ref, r_ref, o_ref, lg_ref):
    y = _norm_mod_math(x_ref[0], g_ref[...], sh_ref[0], sc_ref[0])
    tm = y.shape[0]
    nc = y.shape[1] // LANES
    for c in range(nc):
        o_ref[pl.ds(c, tm, stride=nc), :] = y[:, c * LANES:(c + 1) * LANES]
    lg_ref[0] = jnp.dot(y, r_ref[...], precision=lax.Precision.HIGHEST,
                        preferred_element_type=F32)


def _norm_mod(h, gain, sh, sc, router=None, tm=512):
    b, l, d = h.shape
    assert l % tm == 0
    in_specs = [
        pl.BlockSpec((1, tm, d), lambda bi, r: (bi, r, 0)),
        pl.BlockSpec((1, d), lambda bi, r: (0, 0)),
        pl.BlockSpec((1, 1, d), lambda bi, r: (bi, 0, 0)),
        pl.BlockSpec((1, 1, d), lambda bi, r: (bi, 0, 0)),
    ]
    o_spec = pl.BlockSpec((1, tm, d), lambda bi, r: (bi, r, 0))
    o_shape = jax.ShapeDtypeStruct((b, l, d), BF16)
    if router is None:
        return pl.pallas_call(
            _nm_body, name="norm_mod", grid=(b, l // tm), in_specs=in_specs, out_specs=o_spec,
            out_shape=o_shape, compiler_params=_cparams(2),
        )(h, gain.reshape(1, d), sh, sc)
    ne = router.shape[1]
    router_pad = jnp.pad(router, ((0, 0), (0, LANES - ne)))
    return pl.pallas_call(
        _nm_router_body, name="norm_mod_router", grid=(b, l // tm),
        in_specs=in_specs + [pl.BlockSpec((d, LANES), lambda bi, r: (0, 0))],
        out_specs=[pl.BlockSpec((tm * (d // LANES), LANES), lambda bi, r: (bi * (l // tm) + r, 0)),
                   pl.BlockSpec((1, tm, LANES), lambda bi, r: (bi, r, 0))],
        out_shape=[jax.ShapeDtypeStruct((b * l * (d // LANES), LANES), F32),
                   jax.ShapeDtypeStruct((b, l, LANES), F32)],
        compiler_params=_cparams(2),
    )(h, gain.reshape(1, d), sh, sc, router_pad)


def _combine_norm_body(h_ref, ya_ref, yb_ref, gt_ref, g2_ref, gain_ref, o_ref):
    gt = gt_ref[...]
    y = ya_ref[...] * gt[:, 0:1] + yb_ref[...] * gt[:, 1:2]
    v = h_ref[...] + g2_ref[0] * y
    ms = jnp.mean(v * v, axis=-1, keepdims=True)
    o_ref[...] = v * lax.rsqrt(ms + NORM_EPS) * gain_ref[...]


def _combine_norm(h, ya, yb, gate, g2, gain, rows_per_gate, tm=256):
    m, d = h.shape
    gt = jnp.pad(gate, ((0, 0), (0, LANES - gate.shape[1])))
    tiles_per_gate = rows_per_gate // tm
    row = pl.BlockSpec((tm, d), lambda i: (i, 0))
    return pl.pallas_call(
        _combine_norm_body, name="combine_norm", grid=(m // tm,),
        in_specs=[row, row, row, pl.BlockSpec((tm, LANES), lambda i: (i, 0)),
                  pl.BlockSpec((1, 1, d), lambda i: (i // tiles_per_gate, 0, 0)),
                  pl.BlockSpec((1, d), lambda i: (0, 0))],
        out_specs=row,
        out_shape=jax.ShapeDtypeStruct((m, d), F32),
        compiler_params=_cparams(1),
    )(h, ya, yb, gt, g2, gain.reshape(1, d))


def _krope_body(z_ref, cos_ref, sin_ref, o_ref):
    z = z_ref[...]
    rot = z[:, :QK_ROPE] * cos_ref[...] + z[:, QK_ROPE:] * sin_ref[...]
    o_ref[...] = jnp.concatenate([rot, jnp.zeros_like(rot)], axis=1).astype(BF16)


def _krope(z_all, cos_t, sin_t, col_block, tm=256):
    m = z_all.shape[0]
    t = cos_t.shape[0]
    tiles_per_batch = t // tm
    return pl.pallas_call(
        _krope_body, name="krope", grid=(m // tm,),
        in_specs=[pl.BlockSpec((tm, LANES), lambda i: (i, col_block)),
                  pl.BlockSpec((tm, QK_ROPE), lambda i: (i % tiles_per_batch, 0)),
                  pl.BlockSpec((tm, QK_ROPE), lambda i: (i % tiles_per_batch, 0))],
        out_specs=pl.BlockSpec((tm, LANES), lambda i: (i, 0)),
        out_shape=jax.ShapeDtypeStruct((m, LANES), BF16),
        compiler_params=_cparams(1),
    )(z_all, cos_t, sin_t)


def _attn_body(q_ref, kn_ref, v_ref, kr_ref, cos_ref, sin_ref, o_ref, kfull_ref, *, scale):
    qi = pl.program_id(2)

    @pl.when(qi == 0)
    def _():
        kfull_ref[:, :QK_NOPE] = kn_ref[0]
        kfull_ref[:, QK_NOPE:] = kr_ref[0]

    q = q_ref[...].astype(F32)
    qn = q[:, :QK_NOPE]
    qr = (q[:, QK_NOPE:QK_NOPE + QK_ROPE] * cos_ref[...]
          + q[:, QK_NOPE + QK_ROPE:] * sin_ref[...])
    qf = (jnp.concatenate([qn, qr, jnp.zeros_like(qr)], axis=1) * scale).astype(BF16)
    s = lax.dot_general(qf, kfull_ref[...], (((1,), (1,)), ((), ())),
                        preferred_element_type=F32)
    m = jnp.max(s, axis=-1, keepdims=True)
    p = jnp.exp2(s - m)
    l = jnp.sum(p, axis=-1, keepdims=True)
    o = jnp.dot(p.astype(BF16), v_ref[0], preferred_element_type=F32)
    o_ref[...] = (o / l).astype(BF16)


def _attention(q, kv, kr, cos_q, sin_q, batch, tq=256):
    m = q.shape[0]
    l = m // batch
    t = kv.shape[1]
    nq = l // tq
    scale = float((QK_NOPE + QK_ROPE) ** -0.5 * np.log2(np.e))
    hw = QK_NOPE + 2 * QK_ROPE
    return pl.pallas_call(
        functools.partial(_attn_body, scale=scale), name="attention",
        grid=(batch, MLA_HEADS, nq),
        in_specs=[
            pl.BlockSpec((tq, hw), lambda b, h, i: (b * nq + i, h)),
            pl.BlockSpec((1, t, QK_NOPE), lambda b, h, i: (b, 0, 2 * h)),
            pl.BlockSpec((1, t, V_HEAD), lambda b, h, i: (b, 0, 2 * h + 1)),
            pl.BlockSpec((1, t, LANES), lambda b, h, i: (b, 0, 0)),
            pl.BlockSpec((tq, QK_ROPE), lambda b, h, i: (i, 0)),
            pl.BlockSpec((tq, QK_ROPE), lambda b, h, i: (i, 0)),
        ],
        out_specs=pl.BlockSpec((tq, V_HEAD), lambda b, h, i: (b * nq + i, h)),
        out_shape=jax.ShapeDtypeStruct((m, MLA_HEADS * V_HEAD), BF16),
        scratch_shapes=[pltpu.VMEM((t, QK_NOPE + LANES), BF16)],
        compiler_params=_cparams(3),
    )(q, kv, kv, kr, cos_q, sin_q)


def _sgu_body(u_ref, v_ref, lw_ref, lb_ref, ws_ref, bs_ref, o_ref, vn_ref):
    v = v_ref[...].astype(F32)
    mu = jnp.mean(v, axis=-1, keepdims=True)
    vc = v - mu
    var = jnp.mean(vc * vc, axis=-1, keepdims=True)
    vn_ref[...] = (vc * lax.rsqrt(var + 1e-5) * lw_ref[...] + lb_ref[...]).astype(BF16)
    n_chunks = v_ref.shape[0] // CHUNK
    for n in range(n_chunks):
        rows = slice(n * CHUNK, (n + 1) * CHUNK)
        for g in range(SG_GROUPS):
            cols = slice(g * LANES, (g + 1) * LANES)
            vm = jnp.dot(ws_ref[g], vn_ref[rows, cols], preferred_element_type=F32)
            vm = vm + bs_ref[:, cols]
            o_ref[rows, cols] = (u_ref[rows, cols].astype(F32) * vm).astype(BF16)


def _spatial_gate(hg, ln_w, ln_b, w_s, b_s, tm=256):
    m = hg.shape[0]
    d = hg.shape[1] // 2
    bs_full = jnp.repeat(b_s.T, d // SG_GROUPS, axis=1)
    return pl.pallas_call(
        _sgu_body, name="spatial_gate", grid=(m // tm,),
        in_specs=[
            pl.BlockSpec((tm, d), lambda i: (i, 0)),
            pl.BlockSpec((tm, d), lambda i: (i, 1)),
            pl.BlockSpec((1, d), lambda i: (0, 0)),
            pl.BlockSpec((1, d), lambda i: (0, 0)),
            pl.BlockSpec((SG_GROUPS, CHUNK, CHUNK), lambda i: (0, 0, 0)),
            pl.BlockSpec((CHUNK, d), lambda i: (0, 0)),
        ],
        out_specs=pl.BlockSpec((tm, d), lambda i: (i, 0)),
        out_shape=jax.ShapeDtypeStruct((m, d), BF16),
        scratch_shapes=[pltpu.VMEM((tm, d), BF16)],
        compiler_params=_cparams(1),
    )(hg, hg, ln_w.reshape(1, d), ln_b.reshape(1, d), w_s.astype(BF16), bs_full)


def _split_bf16(x, pieces):
    out = []
    for _ in range(pieces):
        p = x.astype(BF16)
        out.append(p)
        x = x - p.astype(F32)
    return out


def _dot_exact_rhs(x, w, pieces=2):
    return sum(jnp.dot(p, w, preferred_element_type=F32) for p in _split_bf16(x, pieces))


def _dot_split(x, w_hi, w_lo):
    x_hi, x_lo = _split_bf16(x, 2)
    return (jnp.dot(x_hi, w_hi, preferred_element_type=F32)
            + (jnp.dot(x_hi, w_lo, preferred_element_type=F32)
               + jnp.dot(x_lo, w_hi, preferred_element_type=F32)))


def _hi_lo(w):
    hi = w.astype(BF16)
    return hi, (w - hi.astype(F32)).astype(BF16)


def _prep_body(z_ref, zp_ref, zn_ref, mup_ref, mun_ref, kk_ref, ka_ref, w0_ref, a0_ref,
               wup_ref, aup_ref, gup_ref, rk_ref, e_ref, et_ref,
               lw0_o, lw1_o, k0_o, k1_o, b0_o, b1_o, v_o, kkn_o, r_o, g_o, bonus_o,
               *, tiles_ctx, tiles_total):
    tm = z_ref.shape[0]
    rt = pl.program_id(0) % tiles_total
    in_ctx = rt < tiles_ctx
    has_prev = jnp.where(in_ctx, rt > 0, rt > tiles_ctx)
    has_next = jnp.where(in_ctx, rt < tiles_ctx - 1, rt < tiles_total - 1)

    z = z_ref[...]
    rows = lax.broadcasted_iota(jnp.int32, z.shape, 0)
    prev_row = jnp.where(has_prev, zp_ref[7:8, :], 0.0)
    next_row = jnp.where(has_next, zn_ref[0:1, :], 0.0)
    z_prev = jnp.where(rows == 0, prev_row, pltpu.roll(z, 1, 0))
    z_next = jnp.where(rows == tm - 1, next_row, pltpu.roll(z, tm - 1, 0))
    zs = z + mup_ref[...] * (z_prev - z) + mun_ref[...] * (z_next - z)

    d = RWKV_DIM
    r = zs[:, 0:d]
    k = zs[:, d:2 * d]
    v = zs[:, 2 * d:3 * d]
    wd = zs[:, 3 * d:3 * d + LANES]
    ad = zs[:, 3 * d + LANES:3 * d + 2 * LANES]
    gd = zs[:, 3 * d + 2 * LANES:3 * d + 4 * LANES]

    def head_sum(x):
        return _dot_exact_rhs(_dot_exact_rhs(x, e_ref[...]), et_ref[...])

    kk = k * kk_ref[...]
    nrm = jnp.sqrt(head_sum(kk * kk))
    kkn = kk / jnp.maximum(nrm, 1e-12)

    wx = w0_ref[...] + _dot_split(jnp.tanh(wd), wup_ref[0], wup_ref[1])
    lw = -float(np.exp(-0.5)) * jax.nn.sigmoid(wx)
    ax = a0_ref[...] + _dot_split(ad, aup_ref[0], aup_ref[1])
    iclr = jax.nn.sigmoid(ax)
    g = _dot_split(jax.nn.sigmoid(gd), gup_ref[0], gup_ref[1])

    ka = ka_ref[...]
    k0 = k * (1.0 + (iclr[:, :d] - 1.0) * ka)
    k1 = k * (1.0 + (iclr[:, d:] - 1.0) * ka)
    lw0_o[...] = lw[:, :d]
    lw1_o[...] = lw[:, d:]
    k0_o[...] = k0
    k1_o[...] = k1
    b0_o[...] = iclr[:, :d] * kkn
    b1_o[...] = iclr[:, d:] * kkn
    v_o[...] = v
    kkn_o[...] = kkn
    r_o[...] = r
    g_o[...] = g
    bonus_o[...] = head_sum(r * (k0 + k1) * rk_ref[...]) * v


def _head_indicator():
    head_of = jnp.arange(RWKV_DIM) // RWKV_HEAD
    return (head_of[:, None] == jnp.arange(LANES)[None, :]).astype(F32)


def _rwkv_prep(z_rw, p, rows_ctx, rows_total):
    m, w = z_rw.shape
    tm = PREP_TM
    d = RWKV_DIM
    assert rows_ctx % tm == 0 and rows_total % tm == 0 and m % rows_total == 0
    pad = w - RWKV_IN
    mup = jnp.pad(p['mu_prev'], (0, pad)).reshape(1, w)
    mun = jnp.pad(p['mu_next'], (0, pad)).reshape(1, w)
    zero = jnp.zeros((DECAY_LORA, d), F32)
    wup = jnp.concatenate([jnp.concatenate([p['w_up'][0], zero], axis=1),
                           jnp.concatenate([zero, p['w_up'][1]], axis=1)], axis=0)
    aup = jnp.concatenate([jnp.concatenate([p['a_up'][0], zero], axis=1),
                           jnp.concatenate([zero, p['a_up'][1]], axis=1)], axis=0)
    gup = jnp.pad(p['g_up'], ((0, 2 * LANES - GATE_LORA), (0, 0)))
    wup, aup, gup = (jnp.stack(_hi_lo(w)) for w in (wup, aup, gup))
    e = _head_indicator().astype(BF16)

    def full(shape):
        return pl.BlockSpec(shape, lambda i: (0,) * len(shape))

    n8 = m // 8
    out_spec = pl.BlockSpec((tm, d), lambda i: (i, 0))
    out_sds = jax.ShapeDtypeStruct((m, d), F32)
    return pl.pallas_call(
        functools.partial(_prep_body, tiles_ctx=rows_ctx // tm, tiles_total=rows_total // tm),
        name="rwkv_prep",
        grid=(m // tm,),
        in_specs=[
            pl.BlockSpec((tm, w), lambda i: (i, 0)),
            pl.BlockSpec((8, w), lambda i: (jnp.maximum(i * (tm // 8) - 1, 0), 0)),
            pl.BlockSpec((8, w), lambda i: (jnp.minimum((i + 1) * (tm // 8), n8 - 1), 0)),
            full((1, w)), full((1, w)), full((1, d)), full((1, d)), full((1, 2 * d)),
            full((1, 2 * d)), full((2, LANES, 2 * d)), full((2, LANES, 2 * d)),
            full((2, 2 * LANES, d)), full((1, d)), full((d, LANES)), full((LANES, d)),
        ],
        out_specs=[out_spec] * 11,
        out_shape=[out_sds] * 11,
        compiler_params=_cparams(1),
    )(z_rw, z_rw, z_rw, mup, mun, p['k_k'].reshape(1, d), p['k_a'].reshape(1, d),
      p['w0'].reshape(1, 2 * d), p['a0'].reshape(1, 2 * d), wup, aup, gup,
      p['r_k'].reshape(1, d), e, e.T)


def _bdot(x, y):
    return jnp.dot(x.astype(BF16), y.astype(BF16), preferred_element_type=F32)


def _chunk_units(units):
    c = RW_CHUNK
    row = lax.broadcasted_iota(jnp.int32, (c, LANES), 0)
    lane = lax.broadcasted_iota(jnp.int32, (c, LANES), 1)
    pos = lane % c
    lo = lane < c
    tr = lax.broadcasted_iota(jnp.int32, (c, c), 0)
    tc = lax.broadcasted_iota(jnp.int32, (c, c), 1)
    tri = {False: (tc <= tr).astype(BF16), True: (tc >= tr).astype(BF16)}
    strict = {False: pos < row, True: pos > row}
    incl = {False: pos <= row, True: pos >= row}
    last = {False: c - 1, True: 0}
    rev = [u[6] for u in units]
    nu = range(len(units))

    def sb(x):
        return jnp.concatenate([jnp.where(lo, x, 0.0), jnp.where(lo, 0.0, x)], axis=0)

    def nt(x, y):
        return lax.dot_general(x.astype(BF16), y.astype(BF16), (((1,), (1,)), ((), ())),
                               preferred_element_type=F32)

    def tn(x, y):
        return lax.dot_general(x.astype(BF16), y.astype(BF16), (((0,), (0,)), ((), ())),
                               preferred_element_type=F32)

    def fold(x):
        return jnp.where(lo, x[:c], 0.0) + jnp.where(lo, 0.0, x[c:])

    lw3 = [_split_bf16(units[i][0], 3) for i in nu]
    lc = [sum(jnp.dot(tri[rev[i]], piece, preferred_element_type=F32) for piece in lw3[i])
          for i in nu]
    ltot = [lc[i][last[rev[i]]:last[rev[i]] + 1, :] for i in nu]
    e_neg = [jnp.exp(-lc[i]) for i in nu]
    e_h = [jnp.exp(ltot[i] - lc[i]) for i in nu]
    at = [-units[i][3] * jnp.exp(lc[i] - units[i][0]) for i in nu]
    rt = [units[i][5] * jnp.exp(lc[i]) for i in nu]
    bt = [units[i][2] * e_neg[i] for i in nu]
    kt = [units[i][1] * e_neg[i] for i in nu]
    bh = [units[i][2] * e_h[i] for i in nu]
    kh = [units[i][1] * e_h[i] for i in nu]

    ar = [jnp.concatenate([at[i], rt[i]], axis=0) for i in nu]
    m_b = [nt(ar[i], sb(bt[i])) for i in nu]
    m_k = [nt(ar[i], sb(kt[i])) for i in nu]
    n = [jnp.where(strict[rev[i]], m_b[i][:c], 0.0) for i in nu]
    m_ak = [jnp.where(strict[rev[i]], m_k[i][:c], 0.0) for i in nu]
    m_rb = [jnp.where(incl[rev[i]], m_b[i][c:], 0.0) for i in nu]
    m_rk = [jnp.where(incl[rev[i]], m_k[i][c:], 0.0) for i in nu]
    mv = [_bdot(jnp.concatenate([m_ak[i], m_rk[i]], axis=0), sb(units[i][4])) for i in nu]
    xa = at
    xu = [mv[i][:c] for i in nu]
    n_steps = c.bit_length() - 1
    for step in range(n_steps):
        upd = [_bdot(n[i], jnp.concatenate([sb(xa[i]), sb(xu[i])], axis=1)) for i in nu]
        xa = [xa[i] + upd[i][:, :LANES] for i in nu]
        xu = [xu[i] + upd[i][:, LANES:] for i in nu]
        if step < n_steps - 1:
            n = [_bdot(n[i], sb(n[i])) for i in nu]
    rx = [_bdot(m_rb[i], jnp.concatenate([sb(xa[i]), sb(xu[i])], axis=1)) for i in nu]
    gh = [tn(bh[i], jnp.concatenate([xa[i], xu[i]], axis=1)) for i in nu]
    hk = [tn(kh[i], units[i][4]) for i in nu]
    out = []
    for i in nu:
        rp = rt[i] + rx[i][:, :LANES]
        y0 = rx[i][:, LANES:] + mv[i][c:]
        g = fold(gh[i][:, :LANES]) + jnp.where(pos == row, jnp.exp(ltot[i]), 0.0)
        h = fold(gh[i][:, LANES:] + hk[i])
        out.append((g, h, rp, y0))
    return out


def _chunk_body(lw0_ref, lw1_ref, k0_ref, k1_ref, b0_ref, b1_ref, v_ref, kkn_ref, r_ref,
                g0_o, h0_o, rp0_o, y00_o, g1_o, h1_o, rp1_o, y01_o):
    n_chunks = v_ref.shape[0] // RW_CHUNK
    units = []
    for ci in range(n_chunks):
        rows = slice(ci * RW_CHUNK, (ci + 1) * RW_CHUNK)
        v = v_ref[rows, :]
        kkn = kkn_ref[rows, :]
        r = r_ref[rows, :]
        units.append((lw0_ref[rows, :], k0_ref[rows, :], b0_ref[rows, :], kkn, v, r, False))
        units.append((lw1_ref[rows, :], k1_ref[rows, :], b1_ref[rows, :], kkn, v, r, True))
    res = _chunk_units(units)
    for ci in range(n_chunks):
        rows = slice(ci * RW_CHUNK, (ci + 1) * RW_CHUNK)
        for dr, outs in enumerate(((g0_o, h0_o, rp0_o, y00_o), (g1_o, h1_o, rp1_o, y01_o))):
            for o_ref, val in zip(outs, res[2 * ci + dr]):
                o_ref[rows, :] = val


def _rwkv_chunk(prep, tm=512):
    lw0, lw1, k0, k1, b0, b1, v, kkn, r = prep
    m, d = v.shape
    spec = pl.BlockSpec((tm, LANES), lambda i, pr: (i, pr))
    sds = jax.ShapeDtypeStruct((m, d), F32)
    return pl.pallas_call(
        _chunk_body, name="rwkv_chunk", grid=(m // tm, d // LANES),
        in_specs=[spec] * 9, out_specs=[spec] * 8, out_shape=[sds] * 8,
        compiler_params=_cparams(2),
    )(lw0, lw1, k0, k1, b0, b1, v, kkn, r)


def _scan_body(gf_ref, hf_ref, rpf_ref, y0f_ref, gb_ref, hb_ref, rpb_ref, y0b_ref,
               yf_ref, yb_ref, s_ref):
    c = RW_CHUNK
    step = pl.program_id(1)

    @pl.when(step == 0)
    def _():
        s_ref[...] = jnp.zeros(s_ref.shape, F32)

    lane = lax.broadcasted_iota(jnp.int32, (c, LANES), 1)
    lo = lane < c

    def sb(x):
        return jnp.concatenate([jnp.where(lo, x, 0.0), jnp.where(lo, 0.0, x)], axis=0)

    dirs = ((gf_ref, hf_ref, rpf_ref, y0f_ref, yf_ref), (gb_ref, hb_ref, rpb_ref, y0b_ref, yb_ref))
    cols = [slice(pr * LANES, (pr + 1) * LANES) for pr in range(N_PAIRS)]
    out = []
    for dr, (g_ref, _, rp_ref, _, _) in enumerate(dirs):
        lhs = [jnp.concatenate([rp_ref[:, cl], sb(g_ref[:, cl])], axis=0) for cl in cols]
        out.append([_bdot(lhs[pr], s_ref[dr, pr]) for pr in range(N_PAIRS)])
    for dr, (_, h_ref, _, y0_ref, y_ref) in enumerate(dirs):
        for pr, cl in enumerate(cols):
            y_ref[:, cl] = out[dr][pr][:c] + y0_ref[:, cl]
            s_ref[dr, pr] = out[dr][pr][c:] + sb(h_ref[:, cl])


def _rwkv_scan(mats, batch, chunks_ctx):
    m, d = mats[0].shape
    c = RW_CHUNK
    n_chunks = m // c // batch

    def chunk_of(b, s, reverse):
        if not reverse:
            return b * n_chunks + s
        rev = jnp.where(s < chunks_ctx, chunks_ctx - 1 - s, n_chunks - 1 - (s - chunks_ctx))
        return b * n_chunks + rev

    fwd = pl.BlockSpec((c, d), lambda b, s: (chunk_of(b, s, False), 0))
    bwd = pl.BlockSpec((c, d), lambda b, s: (chunk_of(b, s, True), 0))
    sds = jax.ShapeDtypeStruct((m, d), F32)
    return pl.pallas_call(
        _scan_body, name="rwkv_scan", grid=(batch, n_chunks),
        in_specs=[fwd] * 4 + [bwd] * 4, out_specs=[fwd, bwd], out_shape=[sds, sds],
        scratch_shapes=[pltpu.VMEM((2, N_PAIRS, LANES, LANES), F32)],
        compiler_params=_cparams(2),
    )(*mats)


def _readout_body(y0_ref, y1_ref, bonus_ref, g_ref, att_ref, lw_ref, lb_ref, e_ref, et_ref, o_ref):
    def head_mean(x):
        return _dot_exact_rhs(_dot_exact_rhs(x, e_ref[...]), et_ref[...]) * (1.0 / RWKV_HEAD)

    y = y0_ref[...] + y1_ref[...]
    mu = head_mean(y)
    yc = y - mu
    var = head_mean(yc * yc)
    yn = yc * lax.rsqrt(var + LNX_EPS) * lw_ref[...] + lb_ref[...]
    d_att = att_ref.shape[1]
    o_ref[:, :d_att] = att_ref[...]
    o_ref[:, d_att:] = ((yn + bonus_ref[...]) * g_ref[...]).astype(BF16)


def _rwkv_readout(y0, y1, bonus, g, att, p, rows_ctx, rows_total, tm=256):
    d = RWKV_DIM
    n_lat, d_att = att.shape
    assert rows_ctx % tm == 0 and rows_total % tm == 0
    per_b = (rows_total - rows_ctx) // tm
    lat0 = rows_ctx // tm
    tot = rows_total // tm
    e = _head_indicator().astype(BF16)
    rw_spec = pl.BlockSpec((tm, d), lambda i: ((i // per_b) * tot + lat0 + i % per_b, 0))

    def const(shape):
        return pl.BlockSpec(shape, lambda i: (0, 0))

    return pl.pallas_call(
        _readout_body, name="rwkv_readout", grid=(n_lat // tm,),
        in_specs=[rw_spec, rw_spec, rw_spec, rw_spec,
                  pl.BlockSpec((tm, d_att), lambda i: (i, 0)),
                  const((1, d)), const((1, d)), const((d, LANES)), const((LANES, d))],
        out_specs=pl.BlockSpec((tm, d_att + d), lambda i: (i, 0)),
        out_shape=jax.ShapeDtypeStruct((n_lat, d_att + d), BF16),
        compiler_params=_cparams(1),
    )(y0, y1, bonus, g, att, p['lnx_w'].reshape(1, d), p['lnx_b'].reshape(1, d), e, e.T)


def _rwkv_mixer(z_rw, att, p, batch, rows_ctx, rows_total):
    outs = _rwkv_prep(z_rw, p, rows_ctx, rows_total)
    g, bonus = outs[9], outs[10]
    mats = _rwkv_chunk(outs[:9])
    y0, y1 = _rwkv_scan(mats, batch, rows_ctx // RW_CHUNK)
    return _rwkv_readout(y0, y1, bonus, g, att, p, rows_ctx, rows_total)


def _rope_swap_cols(w):
    f = ROPE_FREQS
    parts = []
    for a in range(2):
        x1 = w[..., (2 * a) * f:(2 * a + 1) * f]
        x2 = w[..., (2 * a + 1) * f:(2 * a + 2) * f]
        parts += [-x2, x1]
    return jnp.concatenate(parts, axis=-1)


def _rope_tables(l):
    pos = np.arange(l)
    inv = ROPE_THETA ** (-np.arange(ROPE_FREQS, dtype=np.float64) / ROPE_FREQS)
    ar = (pos // GRID_W)[:, None] * inv
    ac = (pos % GRID_W)[:, None] * inv
    cos = np.concatenate([np.cos(ar), np.cos(ar), np.cos(ac), np.cos(ac)], axis=1)
    sin = np.concatenate([np.sin(ar), np.sin(ar), np.sin(ac), np.sin(ac)], axis=1)
    return cos.astype(np.float32), sin.astype(np.float32)


GATHER_UNROLL = 8


def _gather_body(tok_ref, tv_ref, x_hbm, o_ref, buf_ref, sem):
    i = pl.program_id(0)
    n_tiles = pl.num_programs(0)
    tm = o_ref.shape[0]
    nc = o_ref.shape[1] // LANES

    def row_copy(tile, r):
        slot = tile % 2
        src = pl.multiple_of(tok_ref[tile * tm + r] * nc, nc)
        dst = pl.multiple_of(r * nc, nc)
        return pltpu.make_async_copy(x_hbm.at[pl.ds(src, nc), :],
                                     buf_ref.at[slot, pl.ds(dst, nc), :], sem.at[slot])

    def start_tile(tile):
        def body(g, carry):
            for k in range(GATHER_UNROLL):
                row_copy(tile, g * GATHER_UNROLL + k).start(priority=k % 2)
            return carry
        lax.fori_loop(0, tm // GATHER_UNROLL, body, 0)

    def wait_tile(tile):
        def body(g, carry):
            for k in range(GATHER_UNROLL):
                row_copy(tile, g * GATHER_UNROLL + k).wait()
            return carry
        lax.fori_loop(0, tm // GATHER_UNROLL, body, 0)

    @pl.when(tv_ref[i] == 1)
    def _():
        @pl.when(i == 0)
        def _():
            start_tile(i)

        nxt = jnp.minimum(i + 1, n_tiles - 1)

        @pl.when(jnp.logical_and(i + 1 < n_tiles, tv_ref[nxt] == 1))
        def _():
            start_tile(i + 1)

        wait_tile(i)
        for c in range(nc):
            o_ref[:, c * LANES:(c + 1) * LANES] = (
                buf_ref[i % 2, pl.ds(c, tm, stride=nc), :].astype(BF16))

    @pl.when(tv_ref[i] == 0)
    def _():
        o_ref[...] = jnp.zeros(o_ref.shape, o_ref.dtype)


def _gather_rows_bf16(x, d, row_tok, tile_valid, tm):
    n_rows = row_tok.shape[0]
    gs = pltpu.PrefetchScalarGridSpec(
        num_scalar_prefetch=2,
        grid=(n_rows // tm,),
        in_specs=[pl.BlockSpec(memory_space=pl.ANY)],
        out_specs=pl.BlockSpec((tm, d), lambda i, tok, tv: (i, 0)),
        scratch_shapes=[pltpu.VMEM((2, tm * (d // LANES), LANES), F32),
                        pltpu.SemaphoreType.DMA((2,))])
    return pl.pallas_call(
        _gather_body, grid_spec=gs, name="moe_gather",
        out_shape=jax.ShapeDtypeStruct((n_rows, d), BF16),
        compiler_params=_cparams(1),
    )(row_tok, tile_valid, x)


def _moe(xn, logits, w1, w3, w2):
    n_tok = logits.shape[0]
    d = w1.shape[1]
    tm = MOE_TM
    top_val, top_idx = lax.top_k(logits, TOP_K)
    gate = jax.nn.softmax(top_val, axis=-1)
    n_assign = n_tok * TOP_K
    e_flat = top_idx.reshape(-1).astype(jnp.int32)
    experts = jnp.arange(N_EXPERTS, dtype=jnp.int32)
    order = jnp.argsort(e_flat).astype(jnp.int32)
    rank = jnp.argsort(order).astype(jnp.int32)
    counts = jnp.sum((e_flat[:, None] == experts[None, :]).astype(jnp.int32), axis=0)
    padded = (counts + tm - 1) // tm * tm
    pad_end = jnp.cumsum(padded)
    pad_start = pad_end - padded
    grp_start = jnp.cumsum(counts) - counts
    pos = (pad_start[e_flat] + rank - grp_start[e_flat]).reshape(n_tok, TOP_K)
    n_tiles = n_assign // tm + N_EXPERTS
    n_rows = n_tiles * tm

    def expert_of(row):
        return jnp.minimum(jnp.sum((row[:, None] >= pad_end[None, :]).astype(jnp.int32), axis=1),
                           N_EXPERTS - 1)

    rows = jnp.arange(n_rows, dtype=jnp.int32)
    e_row = expert_of(rows)
    off = rows - pad_start[e_row]
    src = jnp.clip(grp_start[e_row] + off, 0, n_assign - 1)
    row_tok = jnp.where(off < counts[e_row], order[src] // TOP_K, 0)
    tile_start = jnp.arange(n_tiles, dtype=jnp.int32) * tm
    n_valid = pad_end[-1] // tm
    tile_src = jnp.minimum(jnp.arange(n_tiles, dtype=jnp.int32), n_valid - 1).astype(jnp.int32)
    tile_exp = expert_of(tile_start)[tile_src]
    tile_valid = (tile_start < pad_end[-1]).astype(jnp.int32)
    group = (tile_exp, tile_valid, tile_src)

    def rows_of(a, idx):
        return a.at[idx].get(mode="promise_in_bounds")

    xs = _gather_rows_bf16(xn, d, row_tok.astype(jnp.int32), tile_valid, tm)
    hmid = _matmul(xs, [w1, w3], name="moe_up", tm=tm, tn=512, out_dtype=BF16,
                   epilogue="swiglu", group=group)
    ys = _matmul(hmid, [w2], name="moe_down", tm=tm, tn=512, out_dtype=F32, group=group,
                 single_buffer_w=True)
    return rows_of(ys, pos[:, 0]), rows_of(ys, pos[:, 1]), gate


def kernel(x, c, ctx, c_ctx, l0_ada_w, l0_ada_b, l0_norm1, l0_norm2, l0_w_in, l0_q_norm, l0_w_uq, l0_kv_norm, l0_w_ukv, l0_mu_prev, l0_mu_next, l0_w0, l0_w_up, l0_a0, l0_a_up, l0_g_up, l0_k_k, l0_k_a, l0_r_k, l0_lnx_w, l0_lnx_b, l0_w_o, l0_ffn_w1, l0_ffn_w3, l0_ffn_w2, l1_ada_w, l1_ada_b, l1_norm1, l1_norm2, l1_w_in, l1_v_ln_w, l1_v_ln_b, l1_w_s, l1_b_s, l1_w_o, l1_router, l1_moe_w1, l1_moe_w3, l1_moe_w2, final_norm):
    b, l, d = x.shape
    lc = ctx.shape[1]
    t = lc + l
    n_tok = b * l

    cond = jnp.zeros((8, d), F32).at[:b].set(c).at[b].set(c_ctx)
    mod0 = _matmul(cond, [l0_ada_w], name="ada_mod", tm=8, tn=1536, out_dtype=F32, prologue="silu",
                   epilogue="bias", bias=l0_ada_b)
    mod1 = _matmul(cond, [l1_ada_w], name="ada_mod", tm=8, tn=1536, out_dtype=F32, prologue="silu",
                   epilogue="bias", bias=l1_ada_b)

    def mods(mod, row0, nrows):
        return [mod[row0:row0 + nrows, i * d:(i + 1) * d].reshape(nrows, 1, d) for i in range(N_MOD)]

    sh1, sc1, g1, sh2, sc2, g2 = mods(mod0, 0, b)
    csh1, csc1 = mods(mod0, b, 1)[:2]

    xall = _norm_mod_merge(x, ctx, l0_norm1, sh1, sc1, csh1, csc1)
    mla_in = Q_LORA + KV_LORA + QK_ROPE
    w_kr = l0_w_in[:, Q_LORA + KV_LORA:mla_in]
    w_mla = jnp.concatenate([l0_w_in[:, :mla_in], _rope_swap_cols(w_kr)], axis=1)
    w_rw = jnp.pad(l0_w_in[:, mla_in:], ((0, 0), (0, RWKV_PAD - RWKV_IN)))
    xall2 = xall.reshape(b * t, d)
    z_all = _matmul(xall2, [w_mla], name="in_mla", tm=1088, tn=384, out_dtype=F32)
    z_rw = _matmul(xall2, [w_rw], name="in_rwkv", tm=1088, tn=512, out_dtype=F32)

    wq = l0_w_uq.reshape(Q_LORA, MLA_HEADS, QK_NOPE + QK_ROPE)
    wq_ext = jnp.concatenate([wq, _rope_swap_cols(wq[..., QK_NOPE:])], axis=-1)
    wq_ext = wq_ext.reshape(Q_LORA, MLA_HEADS * (QK_NOPE + 2 * QK_ROPE))
    tq_rows = 256
    per_b = l // tq_rows
    lat0 = lc // tq_rows

    def lat_rows(i):
        return (i // per_b) * (t // tq_rows) + lat0 + i % per_b

    q = _matmul(z_all, [wq_ext], name="q_up", tm=tq_rows, tn=2048, out_dtype=BF16, k=Q_LORA, x_col_block=0,
                x_row_map=lat_rows, m_out=n_tok, prologue="rms", gain=l0_q_norm)
    kv = _matmul(z_all, [l0_w_ukv], name="kv_up", tm=1088, tn=1024, out_dtype=BF16, k=KV_LORA, x_col_block=1,
                 prologue="rms", gain=l0_kv_norm)
    cos_l, sin_l = _rope_tables(l)
    cos_t = np.concatenate([np.ones((lc, QK_ROPE), np.float32), cos_l], axis=0)
    sin_t = np.concatenate([np.zeros((lc, QK_ROPE), np.float32), sin_l], axis=0)
    kr = _krope(z_all, cos_t, sin_t, col_block=(Q_LORA + KV_LORA) // LANES)
    att = _attention(q, kv.reshape(b, t, -1), kr.reshape(b, t, LANES), cos_l, sin_l, b)

    p0 = dict(mu_prev=l0_mu_prev, mu_next=l0_mu_next, w0=l0_w0, w_up=l0_w_up, a0=l0_a0,
              a_up=l0_a_up, g_up=l0_g_up, k_k=l0_k_k, k_a=l0_k_a, r_k=l0_r_k,
              lnx_w=l0_lnx_w, lnx_b=l0_lnx_b)
    mix = _rwkv_mixer(z_rw, att, p0, b, lc, t)
    h = _matmul(mix, [l0_w_o], name="mix_out", tm=1024, tn=512, out_dtype=F32, epilogue="resid",
                resid=x.reshape(n_tok, d), gate=g1, rows_per_gate=l)

    xn = _norm_mod(h.reshape(b, l, d), l0_norm2, sh2, sc2).reshape(n_tok, d)
    hmid = _matmul(xn, [l0_ffn_w1, l0_ffn_w3], name="ffn_up", tm=1024, tn=512, out_dtype=BF16,
                   epilogue="swiglu")
    h = _matmul(hmid, [l0_ffn_w2], name="ffn_down", tm=512, tn=512, out_dtype=F32, epilogue="resid",
                resid=h, gate=g2, rows_per_gate=l, single_buffer_w=True)

    sh1, sc1, g1, sh2, sc2, g2 = mods(mod1, 0, b)
    xn = _norm_mod(h.reshape(b, l, d), l1_norm1, sh1, sc1).reshape(n_tok, d)
    hg = _matmul(xn, [l1_w_in], name="gmlp_in", tm=1024, tn=512, out_dtype=BF16, epilogue="gelu")
    gated = _spatial_gate(hg, l1_v_ln_w, l1_v_ln_b, l1_w_s, l1_b_s)
    h = _matmul(gated, [l1_w_o], name="gmlp_out", tm=1024, tn=512, out_dtype=F32, epilogue="resid",
                resid=h, gate=g1, rows_per_gate=l)

    xn, logits = _norm_mod(h.reshape(b, l, d), l1_norm2, sh2, sc2, router=l1_router)
    ya, yb, gate = _moe(xn, logits.reshape(n_tok, LANES)[:, :N_EXPERTS],
                        l1_moe_w1, l1_moe_w3, l1_moe_w2)
    return _combine_norm(h, ya, yb, gate, g2, final_norm, l).reshape(b, l, d)
```

```python
import functools

import jax
import jax.numpy as jnp
import numpy as np
from jax import lax
from jax.experimental import pallas as pl
from jax.experimental.pallas import tpu as pltpu

F32 = jnp.float32
BF16 = jnp.bfloat16

D_MODEL = 2048
GRID_W = 64
N_MOD = 6
NORM_EPS = 1e-6
V_HEAD = 128
MLA_HEADS = 8
Q_LORA = 512
KV_LORA = 512
QK_NOPE = 128
QK_ROPE = 64
ROPE_FREQS = 16
ROPE_THETA = 10000.0
RWKV_HEAD = 64
RWKV_HEADS = 16
RWKV_DIM = 1024
DECAY_LORA = 64
ICLR_LORA = 64
GATE_LORA = 160
LNX_EPS = 64e-5
RWKV_IN = 3 * RWKV_DIM + 2 * DECAY_LORA + 2 * ICLR_LORA + GATE_LORA
RWKV_PAD = 3584
RW_CHUNK = 64
PREP_TM = 128
N_PAIRS = RWKV_HEADS // 2
HI = lax.Precision.HIGHEST
CHUNK = 128
SG_GROUPS = 16
D_FF = 7168
N_EXPERTS = 8
TOP_K = 2

VMEM_LIMIT_BYTES = 56 * 1024 * 1024
LANES = 128

MOE_TM = 512


def _cparams(n_axes):
    return pltpu.CompilerParams(
        dimension_semantics=("arbitrary",) * n_axes,
        vmem_limit_bytes=VMEM_LIMIT_BYTES)


def _mm_body(*refs, n_w, prologue, epilogue, grouped):
    refs = list(refs)
    if grouped:
        te_ref, tv_ref, _ = refs[:3]
        refs = refs[3:]
    x_ref = refs.pop(0)
    w_refs = [refs.pop(0) for _ in range(n_w)]
    gain_ref = refs.pop(0) if prologue == "rms" else None
    bias_ref = refs.pop(0) if epilogue == "bias" else None
    if epilogue == "resid":
        resid_ref = refs.pop(0)
        gate_ref = refs.pop(0)
    o_ref, wbf_ref = refs

    i = pl.program_id(1)
    if grouped:
        new_w = jnp.logical_or(i == 0, te_ref[i] != te_ref[jnp.maximum(i - 1, 0)])
    else:
        new_w = i == 0

    @pl.when(new_w)
    def _():
        for n in range(n_w):
            wbf_ref[n] = w_refs[n][...].astype(BF16)

    def compute():
        x = x_ref[...]
        if prologue == "rms":
            xf = x.astype(F32)
            ms = jnp.mean(xf * xf, axis=-1, keepdims=True)
            x = xf * lax.rsqrt(ms + NORM_EPS) * gain_ref[...]
        elif prologue == "silu":
            xf = x.astype(F32)
            x = xf * jax.nn.sigmoid(xf)
        x = x.astype(BF16)
        acc = [jnp.dot(x, wbf_ref[n], preferred_element_type=F32) for n in range(n_w)]
        if epilogue == "swiglu":
            a = acc[0]
            out = a * jax.nn.sigmoid(a) * acc[1]
        elif epilogue == "gelu":
            a = acc[0]
            out = 0.5 * a * (1.0 + lax.erf(a * (2.0 ** -0.5)))
        elif epilogue == "bias":
            out = acc[0] + bias_ref[...]
        elif epilogue == "resid":
            out = resid_ref[...] + gate_ref[0] * acc[0]
        else:
            out = acc[0]
        o_ref[...] = out.astype(o_ref.dtype)

    if grouped:
        @pl.when(tv_ref[i] == 1)
        def _():
            compute()

        @pl.when(tv_ref[i] == 0)
        def _():
            o_ref[...] = jnp.zeros(o_ref.shape, o_ref.dtype)
    else:
        compute()


def _matmul(x, ws, *, name, tm, tn, out_dtype, k=None, x_col_block=0, x_row_map=None, m_out=None,
            prologue=None, gain=None, epilogue=None, bias=None, resid=None, gate=None,
            rows_per_gate=None, group=None, single_buffer_w=False):
    grouped = group is not None
    kdim = k if k is not None else x.shape[1]
    n = ws[0].shape[-1]
    m = m_out if m_out is not None else x.shape[0]
    assert m % tm == 0 and n % tn == 0, (m, tm, n, tn)
    n_w = len(ws)
    grid = (n // tn, m // tm)
    w_mode = dict(pipeline_mode=pl.Buffered(1)) if single_buffer_w else {}

    if grouped:
        def xmap(j, i, te, tv, ts):
            return (ts[i], x_col_block)

        def wmap(j, i, te, tv, ts):
            return (te[i], 0, j)

        def omap(j, i, te, tv, ts):
            return (i, j)
        w_spec = pl.BlockSpec((None, kdim, tn), wmap, **w_mode)
    else:
        def xmap(j, i):
            return ((x_row_map(i) if x_row_map is not None else i), x_col_block)

        def wmap(j, i):
            return (0, j)

        def omap(j, i):
            return (i, j)
        w_spec = pl.BlockSpec((kdim, tn), wmap, **w_mode)

    in_specs = [pl.BlockSpec((tm, kdim), xmap)] + [w_spec] * n_w
    args = [x] + list(ws)
    if prologue == "rms":
        in_specs.append(pl.BlockSpec((1, kdim), lambda j, i, *_: (0, 0)))
        args.append(gain.reshape(1, kdim))
    if epilogue == "bias":
        in_specs.append(pl.BlockSpec((1, tn), lambda j, i, *_: (0, j)))
        args.append(bias.reshape(1, n))
    if epilogue == "resid":
        in_specs.append(pl.BlockSpec((tm, tn), omap))
        args.append(resid)
        assert rows_per_gate % tm == 0
        tiles_per_gate = rows_per_gate // tm
        in_specs.append(pl.BlockSpec((1, 1, tn), lambda j, i, *_: (i // tiles_per_gate, 0, j)))
        args.append(gate)

    body = functools.partial(_mm_body, n_w=n_w, prologue=prologue, epilogue=epilogue,
                             grouped=grouped)
    gs = pltpu.PrefetchScalarGridSpec(
        num_scalar_prefetch=3 if grouped else 0,
        grid=grid,
        in_specs=in_specs,
        out_specs=pl.BlockSpec((tm, tn), omap),
        scratch_shapes=[pltpu.VMEM((n_w, kdim, tn), BF16)])
    call = pl.pallas_call(
        body, grid_spec=gs, name=name,
        out_shape=jax.ShapeDtypeStruct((m, n), out_dtype),
        compiler_params=_cparams(2))
    if grouped:
        return call(*group, *args)
    return call(*args)


def _norm_mod_math(v, g, sh, sc):
    ms = jnp.mean(v * v, axis=-1, keepdims=True)
    y = v * lax.rsqrt(ms + NORM_EPS) * g
    return y * (1.0 + sc) + sh


def _nm_merge_body(x_ref, c_ref, g_ref, sh_ref, sc_ref, csh_ref, csc_ref, o_ref):
    r = pl.program_id(1)

    @pl.when(r == 0)
    def _():
        o_ref[0] = _norm_mod_math(c_ref[0], g_ref[...], csh_ref[...], csc_ref[...]).astype(BF16)

    @pl.when(r > 0)
    def _():
        o_ref[0] = _norm_mod_math(x_ref[0], g_ref[...], sh_ref[0], sc_ref[0]).astype(BF16)


def _norm_mod_merge(x, ctx, gain, sh, sc, csh, csc):
    b, l, d = x.shape
    lc = ctx.shape[1]
    tm = lc
    assert l % tm == 0
    nt = l // tm + 1
    return pl.pallas_call(
        _nm_merge_body, name="norm_mod_merge",
        grid=(b, nt),
        in_specs=[
            pl.BlockSpec((1, tm, d), lambda bi, r: (bi, jnp.maximum(r - 1, 0), 0)),
            pl.BlockSpec((1, lc, d), lambda bi, r: (bi, 0, 0)),
            pl.BlockSpec((1, d), lambda bi, r: (0, 0)),
            pl.BlockSpec((1, 1, d), lambda bi, r: (bi, 0, 0)),
            pl.BlockSpec((1, 1, d), lambda bi, r: (bi, 0, 0)),
            pl.BlockSpec((1, d), lambda bi, r: (0, 0)),
            pl.BlockSpec((1, d), lambda bi, r: (0, 0)),
        ],
        out_specs=pl.BlockSpec((1, tm, d), lambda bi, r: (bi, r, 0)),
        out_shape=jax.ShapeDtypeStruct((b, lc + l, d), BF16),
        compiler_params=_cparams(2),
    )(x, ctx, gain.reshape(1, d), sh, sc, csh.reshape(1, d), csc.reshape(1, d))


def _nm_body(x_ref, g_ref, sh_ref, sc_ref, o_ref):
    o_ref[0] = _norm_mod_math(x_ref[0], g_ref[...], sh_ref[0], sc_ref[0]).astype(BF16)


def _nm_router_body(x_ref, g_ref, sh_ref, sc_ref, r_ref, o_ref, lg_ref):
    y = _norm_mod_math(x_ref[0], g_ref[...], sh_ref[0], sc_ref[0])
    o_ref[0] = y
    lg_ref[0] = jnp.dot(y, r_ref[...], precision=lax.Precision.HIGHEST,
                        preferred_element_type=F32)


def _norm_mod(h, gain, sh, sc, router=None, tm=512):
    b, l, d = h.shape
    assert l % tm == 0
    in_specs = [
        pl.BlockSpec((1, tm, d), lambda bi, r: (bi, r, 0)),
        pl.BlockSpec((1, d), lambda bi, r: (0, 0)),
        pl.BlockSpec((1, 1, d), lambda bi, r: (bi, 0, 0)),
        pl.BlockSpec((1, 1, d), lambda bi, r: (bi, 0, 0)),
    ]
    o_spec = pl.BlockSpec((1, tm, d), lambda bi, r: (bi, r, 0))
    o_shape = jax.ShapeDtypeStruct((b, l, d), BF16)
    if router is None:
        return pl.pallas_call(
            _nm_body, name="norm_mod", grid=(b, l // tm), in_specs=in_specs, out_specs=o_spec,
            out_shape=o_shape, compiler_params=_cparams(2),
        )(h, gain.reshape(1, d), sh, sc)
    ne = router.shape[1]
    router_pad = jnp.pad(router, ((0, 0), (0, LANES - ne)))
    return pl.pallas_call(
        _nm_router_body, name="norm_mod_router", grid=(b, l // tm),
        in_specs=in_specs + [pl.BlockSpec((d, LANES), lambda bi, r: (0, 0))],
        out_specs=[o_spec, pl.BlockSpec((1, tm, LANES), lambda bi, r: (bi, r, 0))],
        out_shape=[jax.ShapeDtypeStruct((b, l, d), F32),
                   jax.ShapeDtypeStruct((b, l, LANES), F32)],
        compiler_params=_cparams(2),
    )(h, gain.reshape(1, d), sh, sc, router_pad)


def _combine_norm_body(h_ref, ya_ref, yb_ref, gt_ref, g2_ref, gain_ref, o_ref):
    gt = gt_ref[...]
    y = ya_ref[...] * gt[:, 0:1] + yb_ref[...] * gt[:, 1:2]
    v = h_ref[...] + g2_ref[0] * y
    ms = jnp.mean(v * v, axis=-1, keepdims=True)
    o_ref[...] = v * lax.rsqrt(ms + NORM_EPS) * gain_ref[...]


def _combine_norm(h, ya, yb, gate, g2, gain, rows_per_gate, tm=256):
    m, d = h.shape
    gt = jnp.pad(gate, ((0, 0), (0, LANES - gate.shape[1])))
    tiles_per_gate = rows_per_gate // tm
    row = pl.BlockSpec((tm, d), lambda i: (i, 0))
    return pl.pallas_call(
        _combine_norm_body, name="combine_norm", grid=(m // tm,),
        in_specs=[row, row, row, pl.BlockSpec((tm, LANES), lambda i: (i, 0)),
                  pl.BlockSpec((1, 1, d), lambda i: (i // tiles_per_gate, 0, 0)),
                  pl.BlockSpec((1, d), lambda i: (0, 0))],
        out_specs=row,
        out_shape=jax.ShapeDtypeStruct((m, d), F32),
        compiler_params=_cparams(1),
    )(h, ya, yb, gt, g2, gain.reshape(1, d))


def _krope_body(z_ref, cos_ref, sin_ref, o_ref):
    z = z_ref[...]
    rot = z[:, :QK_ROPE] * cos_ref[...] + z[:, QK_ROPE:] * sin_ref[...]
    o_ref[...] = jnp.concatenate([rot, jnp.zeros_like(rot)], axis=1).astype(BF16)


def _krope(z_all, cos_t, sin_t, col_block, tm=256):
    m = z_all.shape[0]
    t = cos_t.shape[0]
    tiles_per_batch = t // tm
    return pl.pallas_call(
        _krope_body, name="krope", grid=(m // tm,),
        in_specs=[pl.BlockSpec((tm, LANES), lambda i: (i, col_block)),
                  pl.BlockSpec((tm, QK_ROPE), lambda i: (i % tiles_per_batch, 0)),
                  pl.BlockSpec((tm, QK_ROPE), lambda i: (i % tiles_per_batch, 0))],
        out_specs=pl.BlockSpec((tm, LANES), lambda i: (i, 0)),
        out_shape=jax.ShapeDtypeStruct((m, LANES), BF16),
        compiler_params=_cparams(1),
    )(z_all, cos_t, sin_t)


def _attn_body(q_ref, kn_ref, v_ref, kr_ref, cos_ref, sin_ref, o_ref, kfull_ref, *, scale):
    qi = pl.program_id(2)

    @pl.when(qi == 0)
    def _():
        kfull_ref[:, :QK_NOPE] = kn_ref[0]
        kfull_ref[:, QK_NOPE:] = kr_ref[0]

    q = q_ref[...].astype(F32)
    qn = q[:, :QK_NOPE]
    qr = (q[:, QK_NOPE:QK_NOPE + QK_ROPE] * cos_ref[...]
          + q[:, QK_NOPE + QK_ROPE:] * sin_ref[...])
    qf = (jnp.concatenate([qn, qr, jnp.zeros_like(qr)], axis=1) * scale).astype(BF16)
    s = lax.dot_general(qf, kfull_ref[...], (((1,), (1,)), ((), ())),
                        preferred_element_type=F32)
    m = jnp.max(s, axis=-1, keepdims=True)
    p = jnp.exp2(s - m)
    l = jnp.sum(p, axis=-1, keepdims=True)
    o = jnp.dot(p.astype(BF16), v_ref[0], preferred_element_type=F32)
    o_ref[...] = (o / l).astype(BF16)


def _attention(q, kv, kr, cos_q, sin_q, batch, tq=256):
    m = q.shape[0]
    l = m // batch
    t = kv.shape[1]
    nq = l // tq
    scale = float((QK_NOPE + QK_ROPE) ** -0.5 * np.log2(np.e))
    hw = QK_NOPE + 2 * QK_ROPE
    return pl.pallas_call(
        functools.partial(_attn_body, scale=scale), name="attention",
        grid=(batch, MLA_HEADS, nq),
        in_specs=[
            pl.BlockSpec((tq, hw), lambda b, h, i: (b * nq + i, h)),
            pl.BlockSpec((1, t, QK_NOPE), lambda b, h, i: (b, 0, 2 * h)),
            pl.BlockSpec((1, t, V_HEAD), lambda b, h, i: (b, 0, 2 * h + 1)),
            pl.BlockSpec((1, t, LANES), lambda b, h, i: (b, 0, 0)),
            pl.BlockSpec((tq, QK_ROPE), lambda b, h, i: (i, 0)),
            pl.BlockSpec((tq, QK_ROPE), lambda b, h, i: (i, 0)),
        ],
        out_specs=pl.BlockSpec((tq, V_HEAD), lambda b, h, i: (b * nq + i, h)),
        out_shape=jax.ShapeDtypeStruct((m, MLA_HEADS * V_HEAD), BF16),
        scratch_shapes=[pltpu.VMEM((t, QK_NOPE + LANES), BF16)],
        compiler_params=_cparams(3),
    )(q, kv, kv, kr, cos_q, sin_q)


def _sgu_body(u_ref, v_ref, lw_ref, lb_ref, ws_ref, bs_ref, o_ref, vn_ref):
    v = v_ref[...].astype(F32)
    mu = jnp.mean(v, axis=-1, keepdims=True)
    vc = v - mu
    var = jnp.mean(vc * vc, axis=-1, keepdims=True)
    vn_ref[...] = (vc * lax.rsqrt(var + 1e-5) * lw_ref[...] + lb_ref[...]).astype(BF16)
    n_chunks = v_ref.shape[0] // CHUNK
    for n in range(n_chunks):
        rows = slice(n * CHUNK, (n + 1) * CHUNK)
        for g in range(SG_GROUPS):
            cols = slice(g * LANES, (g + 1) * LANES)
            vm = jnp.dot(ws_ref[g], vn_ref[rows, cols], preferred_element_type=F32)
            vm = vm + bs_ref[:, cols]
            o_ref[rows, cols] = (u_ref[rows, cols].astype(F32) * vm).astype(BF16)


def _spatial_gate(hg, ln_w, ln_b, w_s, b_s, tm=256):
    m = hg.shape[0]
    d = hg.shape[1] // 2
    bs_full = jnp.repeat(b_s.T, d // SG_GROUPS, axis=1)
    return pl.pallas_call(
        _sgu_body, name="spatial_gate", grid=(m // tm,),
        in_specs=[
            pl.BlockSpec((tm, d), lambda i: (i, 0)),
            pl.BlockSpec((tm, d), lambda i: (i, 1)),
            pl.BlockSpec((1, d), lambda i: (0, 0)),
            pl.BlockSpec((1, d), lambda i: (0, 0)),
            pl.BlockSpec((SG_GROUPS, CHUNK, CHUNK), lambda i: (0, 0, 0)),
            pl.BlockSpec((CHUNK, d), lambda i: (0, 0)),
        ],
        out_specs=pl.BlockSpec((tm, d), lambda i: (i, 0)),
        out_shape=jax.ShapeDtypeStruct((m, d), BF16),
        scratch_shapes=[pltpu.VMEM((tm, d), BF16)],
        compiler_params=_cparams(1),
    )(hg, hg, ln_w.reshape(1, d), ln_b.reshape(1, d), w_s.astype(BF16), bs_full)


def _split_bf16(x, pieces):
    out = []
    for _ in range(pieces):
        p = x.astype(BF16)
        out.append(p)
        x = x - p.astype(F32)
    return out


def _dot_exact_rhs(x, w, pieces=2):
    return sum(jnp.dot(p, w, preferred_element_type=F32) for p in _split_bf16(x, pieces))


def _dot_split(x, w_hi, w_lo):
    x_hi, x_lo = _split_bf16(x, 2)
    return (jnp.dot(x_hi, w_hi, preferred_element_type=F32)
            + (jnp.dot(x_hi, w_lo, preferred_element_type=F32)
               + jnp.dot(x_lo, w_hi, preferred_element_type=F32)))


def _hi_lo(w):
    hi = w.astype(BF16)
    return hi, (w - hi.astype(F32)).astype(BF16)


def _prep_body(z_ref, zp_ref, zn_ref, mup_ref, mun_ref, kk_ref, ka_ref, w0_ref, a0_ref,
               wup_ref, aup_ref, gup_ref, rk_ref, e_ref, et_ref,
               lw0_o, lw1_o, k0_o, k1_o, b0_o, b1_o, v_o, kkn_o, r_o, g_o, bonus_o,
               *, tiles_ctx, tiles_total):
    tm = z_ref.shape[0]
    rt = pl.program_id(0) % tiles_total
    in_ctx = rt < tiles_ctx
    has_prev = jnp.where(in_ctx, rt > 0, rt > tiles_ctx)
    has_next = jnp.where(in_ctx, rt < tiles_ctx - 1, rt < tiles_total - 1)

    z = z_ref[...]
    rows = lax.broadcasted_iota(jnp.int32, z.shape, 0)
    prev_row = jnp.where(has_prev, zp_ref[7:8, :], 0.0)
    next_row = jnp.where(has_next, zn_ref[0:1, :], 0.0)
    z_prev = jnp.where(rows == 0, prev_row, pltpu.roll(z, 1, 0))
    z_next = jnp.where(rows == tm - 1, next_row, pltpu.roll(z, tm - 1, 0))
    zs = z + mup_ref[...] * (z_prev - z) + mun_ref[...] * (z_next - z)

    d = RWKV_DIM
    r = zs[:, 0:d]
    k = zs[:, d:2 * d]
    v = zs[:, 2 * d:3 * d]
    wd = zs[:, 3 * d:3 * d + LANES]
    ad = zs[:, 3 * d + LANES:3 * d + 2 * LANES]
    gd = zs[:, 3 * d + 2 * LANES:3 * d + 4 * LANES]

    def head_sum(x):
        return _dot_exact_rhs(_dot_exact_rhs(x, e_ref[...]), et_ref[...])

    kk = k * kk_ref[...]
    nrm = jnp.sqrt(head_sum(kk * kk))
    kkn = kk / jnp.maximum(nrm, 1e-12)

    wx = w0_ref[...] + _dot_split(jnp.tanh(wd), wup_ref[0], wup_ref[1])
    lw = -float(np.exp(-0.5)) * jax.nn.sigmoid(wx)
    ax = a0_ref[...] + _dot_split(ad, aup_ref[0], aup_ref[1])
    iclr = jax.nn.sigmoid(ax)
    g = _dot_split(jax.nn.sigmoid(gd), gup_ref[0], gup_ref[1])

    ka = ka_ref[...]
    k0 = k * (1.0 + (iclr[:, :d] - 1.0) * ka)
    k1 = k * (1.0 + (iclr[:, d:] - 1.0) * ka)
    lw0_o[...] = lw[:, :d]
    lw1_o[...] = lw[:, d:]
    k0_o[...] = k0
    k1_o[...] = k1
    b0_o[...] = iclr[:, :d] * kkn
    b1_o[...] = iclr[:, d:] * kkn
    v_o[...] = v
    kkn_o[...] = kkn
    r_o[...] = r
    g_o[...] = g
    bonus_o[...] = head_sum(r * (k0 + k1) * rk_ref[...]) * v


def _head_indicator():
    head_of = jnp.arange(RWKV_DIM) // RWKV_HEAD
    return (head_of[:, None] == jnp.arange(LANES)[None, :]).astype(F32)


def _rwkv_prep(z_rw, p, rows_ctx, rows_total):
    m, w = z_rw.shape
    tm = PREP_TM
    d = RWKV_DIM
    assert rows_ctx % tm == 0 and rows_total % tm == 0 and m % rows_total == 0
    pad = w - RWKV_IN
    mup = jnp.pad(p['mu_prev'], (0, pad)).reshape(1, w)
    mun = jnp.pad(p['mu_next'], (0, pad)).reshape(1, w)
    zero = jnp.zeros((DECAY_LORA, d), F32)
    wup = jnp.concatenate([jnp.concatenate([p['w_up'][0], zero], axis=1),
                           jnp.concatenate([zero, p['w_up'][1]], axis=1)], axis=0)
    aup = jnp.concatenate([jnp.concatenate([p['a_up'][0], zero], axis=1),
                           jnp.concatenate([zero, p['a_up'][1]], axis=1)], axis=0)
    gup = jnp.pad(p['g_up'], ((0, 2 * LANES - GATE_LORA), (0, 0)))
    wup, aup, gup = (jnp.stack(_hi_lo(w)) for w in (wup, aup, gup))
    e = _head_indicator().astype(BF16)

    def full(shape):
        return pl.BlockSpec(shape, lambda i: (0,) * len(shape))

    n8 = m // 8
    out_spec = pl.BlockSpec((tm, d), lambda i: (i, 0))
    out_sds = jax.ShapeDtypeStruct((m, d), F32)
    return pl.pallas_call(
        functools.partial(_prep_body, tiles_ctx=rows_ctx // tm, tiles_total=rows_total // tm),
        name="rwkv_prep",
        grid=(m // tm,),
        in_specs=[
            pl.BlockSpec((tm, w), lambda i: (i, 0)),
            pl.BlockSpec((8, w), lambda i: (jnp.maximum(i * (tm // 8) - 1, 0), 0)),
            pl.BlockSpec((8, w), lambda i: (jnp.minimum((i + 1) * (tm // 8), n8 - 1), 0)),
            full((1, w)), full((1, w)), full((1, d)), full((1, d)), full((1, 2 * d)),
            full((1, 2 * d)), full((2, LANES, 2 * d)), full((2, LANES, 2 * d)),
            full((2, 2 * LANES, d)), full((1, d)), full((d, LANES)), full((LANES, d)),
        ],
        out_specs=[out_spec] * 11,
        out_shape=[out_sds] * 11,
        compiler_params=_cparams(1),
    )(z_rw, z_rw, z_rw, mup, mun, p['k_k'].reshape(1, d), p['k_a'].reshape(1, d),
      p['w0'].reshape(1, 2 * d), p['a0'].reshape(1, 2 * d), wup, aup, gup,
      p['r_k'].reshape(1, d), e, e.T)


def _bdot(x, y):
    return jnp.dot(x.astype(BF16), y.astype(BF16), preferred_element_type=F32)


def _chunk_units(units):
    c = RW_CHUNK
    row = lax.broadcasted_iota(jnp.int32, (c, LANES), 0)
    lane = lax.broadcasted_iota(jnp.int32, (c, LANES), 1)
    pos = lane % c
    lo = lane < c
    tr = lax.broadcasted_iota(jnp.int32, (c, c), 0)
    tc = lax.broadcasted_iota(jnp.int32, (c, c), 1)
    tri = {False: (tc <= tr).astype(BF16), True: (tc >= tr).astype(BF16)}
    strict = {False: pos < row, True: pos > row}
    incl = {False: pos <= row, True: pos >= row}
    last = {False: c - 1, True: 0}
    rev = [u[6] for u in units]
    nu = range(len(units))

    def sb(x):
        return jnp.concatenate([jnp.where(lo, x, 0.0), jnp.where(lo, 0.0, x)], axis=0)

    def nt(x, y):
        return lax.dot_general(x.astype(BF16), y.astype(BF16), (((1,), (1,)), ((), ())),
                               preferred_element_type=F32)

    def tn(x, y):
        return lax.dot_general(x.astype(BF16), y.astype(BF16), (((0,), (0,)), ((), ())),
                               preferred_element_type=F32)

    def fold(x):
        return jnp.where(lo, x[:c], 0.0) + jnp.where(lo, 0.0, x[c:])

    lw3 = [_split_bf16(units[i][0], 3) for i in nu]
    lc = [sum(jnp.dot(tri[rev[i]], piece, preferred_element_type=F32) for piece in lw3[i])
          for i in nu]
    ltot = [lc[i][last[rev[i]]:last[rev[i]] + 1, :] for i in nu]
    e_neg = [jnp.exp(-lc[i]) for i in nu]
    e_h = [jnp.exp(ltot[i] - lc[i]) for i in nu]
    at = [-units[i][3] * jnp.exp(lc[i] - units[i][0]) for i in nu]
    rt = [units[i][5] * jnp.exp(lc[i]) for i in nu]
    bt = [units[i][2] * e_neg[i] for i in nu]
    kt = [units[i][1] * e_neg[i] for i in nu]
    bh = [units[i][2] * e_h[i] for i in nu]
    kh = [units[i][1] * e_h[i] for i in nu]

    ar = [jnp.concatenate([at[i], rt[i]], axis=0) for i in nu]
    m_b = [nt(ar[i], sb(bt[i])) for i in nu]
    m_k = [nt(ar[i], sb(kt[i])) for i in nu]
    n = [jnp.where(strict[rev[i]], m_b[i][:c], 0.0) for i in nu]
    m_ak = [jnp.where(strict[rev[i]], m_k[i][:c], 0.0) for i in nu]
    m_rb = [jnp.where(incl[rev[i]], m_b[i][c:], 0.0) for i in nu]
    m_rk = [jnp.where(incl[rev[i]], m_k[i][c:], 0.0) for i in nu]
    mv = [_bdot(jnp.concatenate([m_ak[i], m_rk[i]], axis=0), sb(units[i][4])) for i in nu]
    xa = at
    xu = [mv[i][:c] for i in nu]
    n_steps = c.bit_length() - 1
    for step in range(n_steps):
        upd = [_bdot(n[i], jnp.concatenate([sb(xa[i]), sb(xu[i])], axis=1)) for i in nu]
        xa = [xa[i] + upd[i][:, :LANES] for i in nu]
        xu = [xu[i] + upd[i][:, LANES:] for i in nu]
        if step < n_steps - 1:
            n = [_bdot(n[i], sb(n[i])) for i in nu]
    rx = [_bdot(m_rb[i], jnp.concatenate([sb(xa[i]), sb(xu[i])], axis=1)) for i in nu]
    gh = [tn(bh[i], jnp.concatenate([xa[i], xu[i]], axis=1)) for i in nu]
    hk = [tn(kh[i], units[i][4]) for i in nu]
    out = []
    for i in nu:
        rp = rt[i] + rx[i][:, :LANES]
        y0 = rx[i][:, LANES:] + mv[i][c:]
        g = fold(gh[i][:, :LANES]) + jnp.where(pos == row, jnp.exp(ltot[i]), 0.0)
        h = fold(gh[i][:, LANES:] + hk[i])
        out.append((g, h, rp, y0))
    return out


def _chunk_body(lw0_ref, lw1_ref, k0_ref, k1_ref, b0_ref, b1_ref, v_ref, kkn_ref, r_ref,
                g0_o, h0_o, rp0_o, y00_o, g1_o, h1_o, rp1_o, y01_o):
    n_chunks = v_ref.shape[0] // RW_CHUNK
    units = []
    for ci in range(n_chunks):
        rows = slice(ci * RW_CHUNK, (ci + 1) * RW_CHUNK)
        v = v_ref[rows, :]
        kkn = kkn_ref[rows, :]
        r = r_ref[rows, :]
        units.append((lw0_ref[rows, :], k0_ref[rows, :], b0_ref[rows, :], kkn, v, r, False))
        units.append((lw1_ref[rows, :], k1_ref[rows, :], b1_ref[rows, :], kkn, v, r, True))
    res = _chunk_units(units)
    for ci in range(n_chunks):
        rows = slice(ci * RW_CHUNK, (ci + 1) * RW_CHUNK)
        for dr, outs in enumerate(((g0_o, h0_o, rp0_o, y00_o), (g1_o, h1_o, rp1_o, y01_o))):
            for o_ref, val in zip(outs, res[2 * ci + dr]):
                o_ref[rows, :] = val


def _rwkv_chunk(prep, tm=512):
    lw0, lw1, k0, k1, b0, b1, v, kkn, r = prep
    m, d = v.shape
    spec = pl.BlockSpec((tm, LANES), lambda i, pr: (i, pr))
    sds = jax.ShapeDtypeStruct((m, d), F32)
    return pl.pallas_call(
        _chunk_body, name="rwkv_chunk", grid=(m // tm, d // LANES),
        in_specs=[spec] * 9, out_specs=[spec] * 8, out_shape=[sds] * 8,
        compiler_params=_cparams(2),
    )(lw0, lw1, k0, k1, b0, b1, v, kkn, r)


def _scan_body(gf_ref, hf_ref, rpf_ref, y0f_ref, gb_ref, hb_ref, rpb_ref, y0b_ref,
               yf_ref, yb_ref, s_ref):
    c = RW_CHUNK
    step = pl.program_id(1)

    @pl.when(step == 0)
    def _():
        s_ref[...] = jnp.zeros(s_ref.shape, F32)

    lane = lax.broadcasted_iota(jnp.int32, (c, LANES), 1)
    lo = lane < c

    def sb(x):
        return jnp.concatenate([jnp.where(lo, x, 0.0), jnp.where(lo, 0.0, x)], axis=0)

    dirs = ((gf_ref, hf_ref, rpf_ref, y0f_ref, yf_ref), (gb_ref, hb_ref, rpb_ref, y0b_ref, yb_ref))
    cols = [slice(pr * LANES, (pr + 1) * LANES) for pr in range(N_PAIRS)]
    out = []
    for dr, (g_ref, _, rp_ref, _, _) in enumerate(dirs):
        lhs = [jnp.concatenate([rp_ref[:, cl], sb(g_ref[:, cl])], axis=0) for cl in cols]
        out.append([_bdot(lhs[pr], s_ref[dr, pr]) for pr in range(N_PAIRS)])
    for dr, (_, h_ref, _, y0_ref, y_ref) in enumerate(dirs):
        for pr, cl in enumerate(cols):
            y_ref[:, cl] = out[dr][pr][:c] + y0_ref[:, cl]
            s_ref[dr, pr] = out[dr][pr][c:] + sb(h_ref[:, cl])


def _rwkv_scan(mats, batch, chunks_ctx):
    m, d = mats[0].shape
    c = RW_CHUNK
    n_chunks = m // c // batch

    def chunk_of(b, s, reverse):
        if not reverse:
            return b * n_chunks + s
        rev = jnp.where(s < chunks_ctx, chunks_ctx - 1 - s, n_chunks - 1 - (s - chunks_ctx))
        return b * n_chunks + rev

    fwd = pl.BlockSpec((c, d), lambda b, s: (chunk_of(b, s, False), 0))
    bwd = pl.BlockSpec((c, d), lambda b, s: (chunk_of(b, s, True), 0))
    sds = jax.ShapeDtypeStruct((m, d), F32)
    return pl.pallas_call(
        _scan_body, name="rwkv_scan", grid=(batch, n_chunks),
        in_specs=[fwd] * 4 + [bwd] * 4, out_specs=[fwd, bwd], out_shape=[sds, sds],
        scratch_shapes=[pltpu.VMEM((2, N_PAIRS, LANES, LANES), F32)],
        compiler_params=_cparams(2),
    )(*mats)


def _readout_body(y0_ref, y1_ref, bonus_ref, g_ref, att_ref, lw_ref, lb_ref, e_ref, et_ref, o_ref):
    def head_mean(x):
        return _dot_exact_rhs(_dot_exact_rhs(x, e_ref[...]), et_ref[...]) * (1.0 / RWKV_HEAD)

    y = y0_ref[...] + y1_ref[...]
    mu = head_mean(y)
    yc = y - mu
    var = head_mean(yc * yc)
    yn = yc * lax.rsqrt(var + LNX_EPS) * lw_ref[...] + lb_ref[...]
    d_att = att_ref.shape[1]
    o_ref[:, :d_att] = att_ref[...]
    o_ref[:, d_att:] = ((yn + bonus_ref[...]) * g_ref[...]).astype(BF16)


def _rwkv_readout(y0, y1, bonus, g, att, p, rows_ctx, rows_total, tm=256):
    d = RWKV_DIM
    n_lat, d_att = att.shape
    assert rows_ctx % tm == 0 and rows_total % tm == 0
    per_b = (rows_total - rows_ctx) // tm
    lat0 = rows_ctx // tm
    tot = rows_total // tm
    e = _head_indicator().astype(BF16)
    rw_spec = pl.BlockSpec((tm, d), lambda i: ((i // per_b) * tot + lat0 + i % per_b, 0))

    def const(shape):
        return pl.BlockSpec(shape, lambda i: (0, 0))

    return pl.pallas_call(
        _readout_body, name="rwkv_readout", grid=(n_lat // tm,),
        in_specs=[rw_spec, rw_spec, rw_spec, rw_spec,
                  pl.BlockSpec((tm, d_att), lambda i: (i, 0)),
                  const((1, d)), const((1, d)), const((d, LANES)), const((LANES, d))],
        out_specs=pl.BlockSpec((tm, d_att + d), lambda i: (i, 0)),
        out_shape=jax.ShapeDtypeStruct((n_lat, d_att + d), BF16),
        compiler_params=_cparams(1),
    )(y0, y1, bonus, g, att, p['lnx_w'].reshape(1, d), p['lnx_b'].reshape(1, d), e, e.T)


def _rwkv_mixer(z_rw, att, p, batch, rows_ctx, rows_total):
    outs = _rwkv_prep(z_rw, p, rows_ctx, rows_total)
    g, bonus = outs[9], outs[10]
    mats = _rwkv_chunk(outs[:9])
    y0, y1 = _rwkv_scan(mats, batch, rows_ctx // RW_CHUNK)
    return _rwkv_readout(y0, y1, bonus, g, att, p, rows_ctx, rows_total)


def _rope_swap_cols(w):
    f = ROPE_FREQS
    parts = []
    for a in range(2):
        x1 = w[..., (2 * a) * f:(2 * a + 1) * f]
        x2 = w[..., (2 * a + 1) * f:(2 * a + 2) * f]
        parts += [-x2, x1]
    return jnp.concatenate(parts, axis=-1)


def _rope_tables(l):
    pos = np.arange(l)
    inv = ROPE_THETA ** (-np.arange(ROPE_FREQS, dtype=np.float64) / ROPE_FREQS)
    ar = (pos // GRID_W)[:, None] * inv
    ac = (pos % GRID_W)[:, None] * inv
    cos = np.concatenate([np.cos(ar), np.cos(ar), np.cos(ac), np.cos(ac)], axis=1)
    sin = np.concatenate([np.sin(ar), np.sin(ar), np.sin(ac), np.sin(ac)], axis=1)
    return cos.astype(np.float32), sin.astype(np.float32)


GATHER_UNROLL = 8


def _gather_body(tok_ref, tv_ref, x_hbm, o_ref, buf_ref, sem):
    i = pl.program_id(0)
    n_tiles = pl.num_programs(0)
    tm = o_ref.shape[0]

    def row_copy(tile, r):
        slot = tile % 2
        return pltpu.make_async_copy(x_hbm.at[pl.ds(tok_ref[tile * tm + r], 1), :],
                                     buf_ref.at[slot, pl.ds(r, 1), :], sem.at[slot])

    def start_tile(tile):
        def body(g, carry):
            for k in range(GATHER_UNROLL):
                row_copy(tile, g * GATHER_UNROLL + k).start(priority=k % 2)
            return carry
        lax.fori_loop(0, tm // GATHER_UNROLL, body, 0)

    def wait_tile(tile):
        def body(g, carry):
            for k in range(GATHER_UNROLL):
                row_copy(tile, g * GATHER_UNROLL + k).wait()
            return carry
        lax.fori_loop(0, tm // GATHER_UNROLL, body, 0)

    @pl.when(tv_ref[i] == 1)
    def _():
        @pl.when(i == 0)
        def _():
            start_tile(i)

        nxt = jnp.minimum(i + 1, n_tiles - 1)

        @pl.when(jnp.logical_and(i + 1 < n_tiles, tv_ref[nxt] == 1))
        def _():
            start_tile(i + 1)

        wait_tile(i)
        o_ref[...] = buf_ref[i % 2].astype(BF16)

    @pl.when(tv_ref[i] == 0)
    def _():
        o_ref[...] = jnp.zeros(o_ref.shape, o_ref.dtype)


def _gather_rows_bf16(x, row_tok, tile_valid, tm):
    d = x.shape[1]
    n_rows = row_tok.shape[0]
    gs = pltpu.PrefetchScalarGridSpec(
        num_scalar_prefetch=2,
        grid=(n_rows // tm,),
        in_specs=[pl.BlockSpec(memory_space=pl.ANY)],
        out_specs=pl.BlockSpec((tm, d), lambda i, tok, tv: (i, 0)),
        scratch_shapes=[pltpu.VMEM((2, tm, d), F32), pltpu.SemaphoreType.DMA((2,))])
    return pl.pallas_call(
        _gather_body, grid_spec=gs, name="moe_gather",
        out_shape=jax.ShapeDtypeStruct((n_rows, d), BF16),
        compiler_params=_cparams(1),
    )(row_tok, tile_valid, x)


def _moe(xn, logits, w1, w3, w2):
    n_tok = xn.shape[0]
    tm = MOE_TM
    top_val, top_idx = lax.top_k(logits, TOP_K)
    gate = jax.nn.softmax(top_val, axis=-1)
    n_assign = n_tok * TOP_K
    e_flat = top_idx.reshape(-1).astype(jnp.int32)
    experts = jnp.arange(N_EXPERTS, dtype=jnp.int32)
    order = jnp.argsort(e_flat).astype(jnp.int32)
    rank = jnp.argsort(order).astype(jnp.int32)
    counts = jnp.sum((e_flat[:, None] == experts[None, :]).astype(jnp.int32), axis=0)
    padded = (counts + tm - 1) // tm * tm
    pad_end = jnp.cumsum(padded)
    pad_start = pad_end - padded
    grp_start = jnp.cumsum(counts) - counts
    pos = (pad_start[e_flat] + rank - grp_start[e_flat]).reshape(n_tok, TOP_K)
    n_tiles = n_assign // tm + N_EXPERTS
    n_rows = n_tiles * tm

    def expert_of(row):
        return jnp.minimum(jnp.sum((row[:, None] >= pad_end[None, :]).astype(jnp.int32), axis=1),
                           N_EXPERTS - 1)

    rows = jnp.arange(n_rows, dtype=jnp.int32)
    e_row = expert_of(rows)
    off = rows - pad_start[e_row]
    src = jnp.clip(grp_start[e_row] + off, 0, n_assign - 1)
    row_tok = jnp.where(off < counts[e_row], order[src] // TOP_K, 0)
    tile_start = jnp.arange(n_tiles, dtype=jnp.int32) * tm
    n_valid = pad_end[-1] // tm
    tile_src = jnp.minimum(jnp.arange(n_tiles, dtype=jnp.int32), n_valid - 1).astype(jnp.int32)
    tile_exp = expert_of(tile_start)[tile_src]
    tile_valid = (tile_start < pad_end[-1]).astype(jnp.int32)
    group = (tile_exp, tile_valid, tile_src)

    def rows_of(a, idx):
        return a.at[idx].get(mode="promise_in_bounds")

    xs = _gather_rows_bf16(xn, row_tok.astype(jnp.int32), tile_valid, tm)
    hmid = _matmul(xs, [w1, w3], name="moe_up", tm=tm, tn=1024, out_dtype=BF16,
                   epilogue="swiglu", group=group)
    ys = _matmul(hmid, [w2], name="moe_down", tm=tm, tn=512, out_dtype=F32, group=group,
                 single_buffer_w=True)
    return rows_of(ys, pos[:, 0]), rows_of(ys, pos[:, 1]), gate


def kernel(x, c, ctx, c_ctx, l0_ada_w, l0_ada_b, l0_norm1, l0_norm2, l0_w_in, l0_q_norm, l0_w_uq, l0_kv_norm, l0_w_ukv, l0_mu_prev, l0_mu_next, l0_w0, l0_w_up, l0_a0, l0_a_up, l0_g_up, l0_k_k, l0_k_a, l0_r_k, l0_lnx_w, l0_lnx_b, l0_w_o, l0_ffn_w1, l0_ffn_w3, l0_ffn_w2, l1_ada_w, l1_ada_b, l1_norm1, l1_norm2, l1_w_in, l1_v_ln_w, l1_v_ln_b, l1_w_s, l1_b_s, l1_w_o, l1_router, l1_moe_w1, l1_moe_w3, l1_moe_w2, final_norm):
    b, l, d = x.shape
    lc = ctx.shape[1]
    t = lc + l
    n_tok = b * l

    cond = jnp.zeros((8, d), F32).at[:b].set(c).at[b].set(c_ctx)
    mod0 = _matmul(cond, [l0_ada_w], name="ada_mod", tm=8, tn=1536, out_dtype=F32, prologue="silu",
                   epilogue="bias", bias=l0_ada_b)
    mod1 = _matmul(cond, [l1_ada_w], name="ada_mod", tm=8, tn=1536, out_dtype=F32, prologue="silu",
                   epilogue="bias", bias=l1_ada_b)

    def mods(mod, row0, nrows):
        return [mod[row0:row0 + nrows, i * d:(i + 1) * d].reshape(nrows, 1, d) for i in range(N_MOD)]

    sh1, sc1, g1, sh2, sc2, g2 = mods(mod0, 0, b)
    csh1, csc1 = mods(mod0, b, 1)[:2]

    xall = _norm_mod_merge(x, ctx, l0_norm1, sh1, sc1, csh1, csc1)
    mla_in = Q_LORA + KV_LORA + QK_ROPE
    w_kr = l0_w_in[:, Q_LORA + KV_LORA:mla_in]
    w_mla = jnp.concatenate([l0_w_in[:, :mla_in], _rope_swap_cols(w_kr)], axis=1)
    w_rw = jnp.pad(l0_w_in[:, mla_in:], ((0, 0), (0, RWKV_PAD - RWKV_IN)))
    xall2 = xall.reshape(b * t, d)
    z_all = _matmul(xall2, [w_mla], name="in_mla", tm=1088, tn=384, out_dtype=F32)
    z_rw = _matmul(xall2, [w_rw], name="in_rwkv", tm=1088, tn=512, out_dtype=F32)

    wq = l0_w_uq.reshape(Q_LORA, MLA_HEADS, QK_NOPE + QK_ROPE)
    wq_ext = jnp.concatenate([wq, _rope_swap_cols(wq[..., QK_NOPE:])], axis=-1)
    wq_ext = wq_ext.reshape(Q_LORA, MLA_HEADS * (QK_NOPE + 2 * QK_ROPE))
    tq_rows = 256
    per_b = l // tq_rows
    lat0 = lc // tq_rows

    def lat_rows(i):
        return (i // per_b) * (t // tq_rows) + lat0 + i % per_b

    q = _matmul(z_all, [wq_ext], name="q_up", tm=tq_rows, tn=2048, out_dtype=BF16, k=Q_LORA, x_col_block=0,
                x_row_map=lat_rows, m_out=n_tok, prologue="rms", gain=l0_q_norm)
    kv = _matmul(z_all, [l0_w_ukv], name="kv_up", tm=1088, tn=1024, out_dtype=BF16, k=KV_LORA, x_col_block=1,
                 prologue="rms", gain=l0_kv_norm)
    cos_l, sin_l = _rope_tables(l)
    cos_t = np.concatenate([np.ones((lc, QK_ROPE), np.float32), cos_l], axis=0)
    sin_t = np.concatenate([np.zeros((lc, QK_ROPE), np.float32), sin_l], axis=0)
    kr = _krope(z_all, cos_t, sin_t, col_block=(Q_LORA + KV_LORA) // LANES)
    att = _attention(q, kv.reshape(b, t, -1), kr.reshape(b, t, LANES), cos_l, sin_l, b)

    p0 = dict(mu_prev=l0_mu_prev, mu_next=l0_mu_next, w0=l0_w0, w_up=l0_w_up, a0=l0_a0,
              a_up=l0_a_up, g_up=l0_g_up, k_k=l0_k_k, k_a=l0_k_a, r_k=l0_r_k,
              lnx_w=l0_lnx_w, lnx_b=l0_lnx_b)
    mix = _rwkv_mixer(z_rw, att, p0, b, lc, t)
    h = _matmul(mix, [l0_w_o], name="mix_out", tm=1024, tn=512, out_dtype=F32, epilogue="resid",
                resid=x.reshape(n_tok, d), gate=g1, rows_per_gate=l)

    xn = _norm_mod(h.reshape(b, l, d), l0_norm2, sh2, sc2).reshape(n_tok, d)
    hmid = _matmul(xn, [l0_ffn_w1, l0_ffn_w3], name="ffn_up", tm=1024, tn=512, out_dtype=BF16,
                   epilogue="swiglu")
    h = _matmul(hmid, [l0_ffn_w2], name="ffn_down", tm=512, tn=512, out_dtype=F32, epilogue="resid",
                resid=h, gate=g2, rows_per_gate=l, single_buffer_w=True)

    sh1, sc1, g1, sh2, sc2, g2 = mods(mod1, 0, b)
    xn = _norm_mod(h.reshape(b, l, d), l1_norm1, sh1, sc1).reshape(n_tok, d)
    hg = _matmul(xn, [l1_w_in], name="gmlp_in", tm=1024, tn=512, out_dtype=BF16, epilogue="gelu")
    gated = _spatial_gate(hg, l1_v_ln_w, l1_v_ln_b, l1_w_s, l1_b_s)
    h = _matmul(gated, [l1_w_o], name="gmlp_out", tm=1024, tn=512, out_dtype=F32, epilogue="resid",
                resid=h, gate=g1, rows_per_gate=l)

    xn, logits = _norm_mod(h.reshape(b, l, d), l1_norm2, sh2, sc2, router=l1_router)
    ya, yb, gate = _moe(xn.reshape(n_tok, d), logits.reshape(n_tok, LANES)[:, :N_EXPERTS],
                        l1_moe_w1, l1_moe_w3, l1_moe_w2)
    return _combine_norm(h, ya, yb, gate, g2, final_norm, l).reshape(b, l, d)
```

```python
import functools

import jax
import jax.numpy as jnp
import numpy as np
from jax import lax
from jax.experimental import pallas as pl
from jax.experimental.pallas import tpu as pltpu

F32 = jnp.float32
BF16 = jnp.bfloat16

D_MODEL = 2048
GRID_W = 64
N_MOD = 6
NORM_EPS = 1e-6
V_HEAD = 128
MLA_HEADS = 8
Q_LORA = 512
KV_LORA = 512
QK_NOPE = 128
QK_ROPE = 64
ROPE_FREQS = 16
ROPE_THETA = 10000.0
RWKV_HEAD = 64
RWKV_HEADS = 16
RWKV_DIM = 1024
DECAY_LORA = 64
ICLR_LORA = 64
GATE_LORA = 160
LNX_EPS = 64e-5
RWKV_IN = 3 * RWKV_DIM + 2 * DECAY_LORA + 2 * ICLR_LORA + GATE_LORA
RWKV_PAD = 3584
RW_CHUNK = 64
PREP_TM = 128
N_PAIRS = RWKV_HEADS // 2
HI = lax.Precision.HIGHEST
CHUNK = 128
SG_GROUPS = 16
D_FF = 7168
N_EXPERTS = 8
TOP_K = 2

VMEM_LIMIT_BYTES = 56 * 1024 * 1024
LANES = 128

MOE_TM = 512


def _cparams(n_axes):
    return pltpu.CompilerParams(
        dimension_semantics=("arbitrary",) * n_axes,
        vmem_limit_bytes=VMEM_LIMIT_BYTES)


def _mm_body(*refs, n_w, prologue, epilogue, grouped):
    refs = list(refs)
    if grouped:
        te_ref, tv_ref, _ = refs[:3]
        refs = refs[3:]
    x_ref = refs.pop(0)
    w_refs = [refs.pop(0) for _ in range(n_w)]
    gain_ref = refs.pop(0) if prologue == "rms" else None
    bias_ref = refs.pop(0) if epilogue == "bias" else None
    if epilogue == "resid":
        resid_ref = refs.pop(0)
        gate_ref = refs.pop(0)
    o_ref, wbf_ref = refs

    i = pl.program_id(1)
    if grouped:
        new_w = jnp.logical_or(i == 0, te_ref[i] != te_ref[jnp.maximum(i - 1, 0)])
    else:
        new_w = i == 0

    @pl.when(new_w)
    def _():
        for n in range(n_w):
            wbf_ref[n] = w_refs[n][...].astype(BF16)

    def compute():
        x = x_ref[...]
        if prologue == "rms":
            xf = x.astype(F32)
            ms = jnp.mean(xf * xf, axis=-1, keepdims=True)
            x = xf * lax.rsqrt(ms + NORM_EPS) * gain_ref[...]
        elif prologue == "silu":
            xf = x.astype(F32)
            x = xf * jax.nn.sigmoid(xf)
        x = x.astype(BF16)
        acc = [jnp.dot(x, wbf_ref[n], preferred_element_type=F32) for n in range(n_w)]
        if epilogue == "swiglu":
            a = acc[0]
            out = a * jax.nn.sigmoid(a) * acc[1]
        elif epilogue == "gelu":
            a = acc[0]
            out = 0.5 * a * (1.0 + lax.erf(a * (2.0 ** -0.5)))
        elif epilogue == "bias":
            out = acc[0] + bias_ref[...]
        elif epilogue == "resid":
            out = resid_ref[...] + gate_ref[0] * acc[0]
        else:
            out = acc[0]
        o_ref[...] = out.astype(o_ref.dtype)

    if grouped:
        @pl.when(tv_ref[i] == 1)
        def _():
            compute()

        @pl.when(tv_ref[i] == 0)
        def _():
            o_ref[...] = jnp.zeros(o_ref.shape, o_ref.dtype)
    else:
        compute()


def _matmul(x, ws, *, name, tm, tn, out_dtype, k=None, x_col_block=0, x_row_map=None, m_out=None,
            prologue=None, gain=None, epilogue=None, bias=None, resid=None, gate=None,
            rows_per_gate=None, group=None):
    grouped = group is not None
    kdim = k if k is not None else x.shape[1]
    n = ws[0].shape[-1]
    m = m_out if m_out is not None else x.shape[0]
    assert m % tm == 0 and n % tn == 0, (m, tm, n, tn)
    n_w = len(ws)
    grid = (n // tn, m // tm)

    if grouped:
        def xmap(j, i, te, tv, ts):
            return (ts[i], x_col_block)

        def wmap(j, i, te, tv, ts):
            return (te[i], 0, j)

        def omap(j, i, te, tv, ts):
            return (i, j)
        w_spec = pl.BlockSpec((None, kdim, tn), wmap)
    else:
        def xmap(j, i):
            return ((x_row_map(i) if x_row_map is not None else i), x_col_block)

        def wmap(j, i):
            return (0, j)

        def omap(j, i):
            return (i, j)
        w_spec = pl.BlockSpec((kdim, tn), wmap)

    in_specs = [pl.BlockSpec((tm, kdim), xmap)] + [w_spec] * n_w
    args = [x] + list(ws)
    if prologue == "rms":
        in_specs.append(pl.BlockSpec((1, kdim), lambda j, i, *_: (0, 0)))
        args.append(gain.reshape(1, kdim))
    if epilogue == "bias":
        in_specs.append(pl.BlockSpec((1, tn), lambda j, i, *_: (0, j)))
        args.append(bias.reshape(1, n))
    if epilogue == "resid":
        in_specs.append(pl.BlockSpec((tm, tn), omap))
        args.append(resid)
        assert rows_per_gate % tm == 0
        tiles_per_gate = rows_per_gate // tm
        in_specs.append(pl.BlockSpec((1, 1, tn), lambda j, i, *_: (i // tiles_per_gate, 0, j)))
        args.append(gate)

    body = functools.partial(_mm_body, n_w=n_w, prologue=prologue, epilogue=epilogue,
                             grouped=grouped)
    gs = pltpu.PrefetchScalarGridSpec(
        num_scalar_prefetch=3 if grouped else 0,
        grid=grid,
        in_specs=in_specs,
        out_specs=pl.BlockSpec((tm, tn), omap),
        scratch_shapes=[pltpu.VMEM((n_w, kdim, tn), BF16)])
    call = pl.pallas_call(
        body, grid_spec=gs, name=name,
        out_shape=jax.ShapeDtypeStruct((m, n), out_dtype),
        compiler_params=_cparams(2))
    if grouped:
        return call(*group, *args)
    return call(*args)


def _norm_mod_math(v, g, sh, sc):
    ms = jnp.mean(v * v, axis=-1, keepdims=True)
    y = v * lax.rsqrt(ms + NORM_EPS) * g
    return y * (1.0 + sc) + sh


def _nm_merge_body(x_ref, c_ref, g_ref, sh_ref, sc_ref, csh_ref, csc_ref, o_ref):
    r = pl.program_id(1)

    @pl.when(r == 0)
    def _():
        o_ref[0] = _norm_mod_math(c_ref[0], g_ref[...], csh_ref[...], csc_ref[...]).astype(BF16)

    @pl.when(r > 0)
    def _():
        o_ref[0] = _norm_mod_math(x_ref[0], g_ref[...], sh_ref[0], sc_ref[0]).astype(BF16)


def _norm_mod_merge(x, ctx, gain, sh, sc, csh, csc):
    b, l, d = x.shape
    lc = ctx.shape[1]
    tm = lc
    assert l % tm == 0
    nt = l // tm + 1
    return pl.pallas_call(
        _nm_merge_body, name="norm_mod_merge",
        grid=(b, nt),
        in_specs=[
            pl.BlockSpec((1, tm, d), lambda bi, r: (bi, jnp.maximum(r - 1, 0), 0)),
            pl.BlockSpec((1, lc, d), lambda bi, r: (bi, 0, 0)),
            pl.BlockSpec((1, d), lambda bi, r: (0, 0)),
            pl.BlockSpec((1, 1, d), lambda bi, r: (bi, 0, 0)),
            pl.BlockSpec((1, 1, d), lambda bi, r: (bi, 0, 0)),
            pl.BlockSpec((1, d), lambda bi, r: (0, 0)),
            pl.BlockSpec((1, d), lambda bi, r: (0, 0)),
        ],
        out_specs=pl.BlockSpec((1, tm, d), lambda bi, r: (bi, r, 0)),
        out_shape=jax.ShapeDtypeStruct((b, lc + l, d), BF16),
        compiler_params=_cparams(2),
    )(x, ctx, gain.reshape(1, d), sh, sc, csh.reshape(1, d), csc.reshape(1, d))


def _nm_body(x_ref, g_ref, sh_ref, sc_ref, o_ref):
    o_ref[0] = _norm_mod_math(x_ref[0], g_ref[...], sh_ref[0], sc_ref[0]).astype(BF16)


def _nm_router_body(x_ref, g_ref, sh_ref, sc_ref, r_ref, o_ref, lg_ref):
    y = _norm_mod_math(x_ref[0], g_ref[...], sh_ref[0], sc_ref[0])
    o_ref[0] = y
    lg_ref[0] = jnp.dot(y, r_ref[...], precision=HI,
                        preferred_element_type=F32)


def _norm_mod(h, gain, sh, sc, router=None, tm=512):
    b, l, d = h.shape
    assert l % tm == 0
    in_specs = [
        pl.BlockSpec((1, tm, d), lambda bi, r: (bi, r, 0)),
        pl.BlockSpec((1, d), lambda bi, r: (0, 0)),
        pl.BlockSpec((1, 1, d), lambda bi, r: (bi, 0, 0)),
        pl.BlockSpec((1, 1, d), lambda bi, r: (bi, 0, 0)),
    ]
    o_spec = pl.BlockSpec((1, tm, d), lambda bi, r: (bi, r, 0))
    o_shape = jax.ShapeDtypeStruct((b, l, d), BF16)
    if router is None:
        return pl.pallas_call(
            _nm_body, name="norm_mod", grid=(b, l // tm), in_specs=in_specs, out_specs=o_spec,
            out_shape=o_shape, compiler_params=_cparams(2),
        )(h, gain.reshape(1, d), sh, sc)
    ne = router.shape[1]
    router_pad = jnp.pad(router, ((0, 0), (0, LANES - ne)))
    return pl.pallas_call(
        _nm_router_body, name="norm_mod_router", grid=(b, l // tm),
        in_specs=in_specs + [pl.BlockSpec((d, LANES), lambda bi, r: (0, 0))],
        out_specs=[o_spec, pl.BlockSpec((1, tm, LANES), lambda bi, r: (bi, r, 0))],
        out_shape=[jax.ShapeDtypeStruct((b, l, d), F32),
                   jax.ShapeDtypeStruct((b, l, LANES), F32)],
        compiler_params=_cparams(2),
    )(h, gain.reshape(1, d), sh, sc, router_pad)


def _combine_norm_body(h_ref, ya_ref, yb_ref, gt_ref, g2_ref, gain_ref, o_ref):
    gt = gt_ref[...]
    y = ya_ref[...] * gt[:, 0:1] + yb_ref[...] * gt[:, 1:2]
    v = h_ref[...] + g2_ref[0] * y
    ms = jnp.mean(v * v, axis=-1, keepdims=True)
    o_ref[...] = v * lax.rsqrt(ms + NORM_EPS) * gain_ref[...]


def _combine_norm(h, ya, yb, gate, g2, gain, rows_per_gate, tm=256):
    m, d = h.shape
    gt = jnp.pad(gate, ((0, 0), (0, LANES - gate.shape[1])))
    tiles_per_gate = rows_per_gate // tm
    row = pl.BlockSpec((tm, d), lambda i: (i, 0))
    return pl.pallas_call(
        _combine_norm_body, name="combine_norm", grid=(m // tm,),
        in_specs=[row, row, row, pl.BlockSpec((tm, LANES), lambda i: (i, 0)),
                  pl.BlockSpec((1, 1, d), lambda i: (i // tiles_per_gate, 0, 0)),
                  pl.BlockSpec((1, d), lambda i: (0, 0))],
        out_specs=row,
        out_shape=jax.ShapeDtypeStruct((m, d), F32),
        compiler_params=_cparams(1),
    )(h, ya, yb, gt, g2, gain.reshape(1, d))


def _krope_body(z_ref, cos_ref, sin_ref, o_ref):
    z = z_ref[...]
    rot = z[:, :QK_ROPE] * cos_ref[...] + z[:, QK_ROPE:] * sin_ref[...]
    o_ref[...] = jnp.concatenate([rot, jnp.zeros_like(rot)], axis=1).astype(BF16)


def _krope(z_all, cos_t, sin_t, col_block, tm=256):
    m = z_all.shape[0]
    t = cos_t.shape[0]
    tiles_per_batch = t // tm
    return pl.pallas_call(
        _krope_body, name="krope", grid=(m // tm,),
        in_specs=[pl.BlockSpec((tm, LANES), lambda i: (i, col_block)),
                  pl.BlockSpec((tm, QK_ROPE), lambda i: (i % tiles_per_batch, 0)),
                  pl.BlockSpec((tm, QK_ROPE), lambda i: (i % tiles_per_batch, 0))],
        out_specs=pl.BlockSpec((tm, LANES), lambda i: (i, 0)),
        out_shape=jax.ShapeDtypeStruct((m, LANES), BF16),
        compiler_params=_cparams(1),
    )(z_all, cos_t, sin_t)


def _attn_body(q_ref, kn_ref, v_ref, kr_ref, cos_ref, sin_ref, o_ref, kfull_ref, *, scale):
    qi = pl.program_id(2)

    @pl.when(qi == 0)
    def _():
        kfull_ref[:, :QK_NOPE] = kn_ref[0]
        kfull_ref[:, QK_NOPE:] = kr_ref[0]

    q = q_ref[...].astype(F32)
    qn = q[:, :QK_NOPE]
    qr = (q[:, QK_NOPE:QK_NOPE + QK_ROPE] * cos_ref[...]
          + q[:, QK_NOPE + QK_ROPE:] * sin_ref[...])
    qf = (jnp.concatenate([qn, qr, jnp.zeros_like(qr)], axis=1) * scale).astype(BF16)
    s = lax.dot_general(qf, kfull_ref[...], (((1,), (1,)), ((), ())),
                        preferred_element_type=F32)
    m = jnp.max(s, axis=-1, keepdims=True)
    p = jnp.exp2(s - m)
    l = jnp.sum(p, axis=-1, keepdims=True)
    o = jnp.dot(p.astype(BF16), v_ref[0], preferred_element_type=F32)
    o_ref[...] = (o / l).astype(BF16)


def _attention(q, kv, kr, cos_q, sin_q, batch, tq=256):
    m = q.shape[0]
    l = m // batch
    t = kv.shape[1]
    nq = l // tq
    scale = float((QK_NOPE + QK_ROPE) ** -0.5 * np.log2(np.e))
    hw = QK_NOPE + 2 * QK_ROPE
    return pl.pallas_call(
        functools.partial(_attn_body, scale=scale), name="attention",
        grid=(batch, MLA_HEADS, nq),
        in_specs=[
            pl.BlockSpec((tq, hw), lambda b, h, i: (b * nq + i, h)),
            pl.BlockSpec((1, t, QK_NOPE), lambda b, h, i: (b, 0, 2 * h)),
            pl.BlockSpec((1, t, V_HEAD), lambda b, h, i: (b, 0, 2 * h + 1)),
            pl.BlockSpec((1, t, LANES), lambda b, h, i: (b, 0, 0)),
            pl.BlockSpec((tq, QK_ROPE), lambda b, h, i: (i, 0)),
            pl.BlockSpec((tq, QK_ROPE), lambda b, h, i: (i, 0)),
        ],
        out_specs=pl.BlockSpec((tq, V_HEAD), lambda b, h, i: (b * nq + i, h)),
        out_shape=jax.ShapeDtypeStruct((m, MLA_HEADS * V_HEAD), BF16),
        scratch_shapes=[pltpu.VMEM((t, QK_NOPE + LANES), BF16)],
        compiler_params=_cparams(3),
    )(q, kv, kv, kr, cos_q, sin_q)


def _sgu_body(u_ref, v_ref, lw_ref, lb_ref, ws_ref, bs_ref, o_ref, vn_ref):
    v = v_ref[...].astype(F32)
    mu = jnp.mean(v, axis=-1, keepdims=True)
    vc = v - mu
    var = jnp.mean(vc * vc, axis=-1, keepdims=True)
    vn_ref[...] = (vc * lax.rsqrt(var + 1e-5) * lw_ref[...] + lb_ref[...]).astype(BF16)
    n_chunks = v_ref.shape[0] // CHUNK
    for n in range(n_chunks):
        rows = slice(n * CHUNK, (n + 1) * CHUNK)
        for g in range(SG_GROUPS):
            cols = slice(g * LANES, (g + 1) * LANES)
            vm = jnp.dot(ws_ref[g], vn_ref[rows, cols], preferred_element_type=F32)
            vm = vm + bs_ref[:, cols]
            o_ref[rows, cols] = (u_ref[rows, cols].astype(F32) * vm).astype(BF16)


def _spatial_gate(hg, ln_w, ln_b, w_s, b_s, tm=256):
    m = hg.shape[0]
    d = hg.shape[1] // 2
    bs_full = jnp.repeat(b_s.T, d // SG_GROUPS, axis=1)
    return pl.pallas_call(
        _sgu_body, name="spatial_gate", grid=(m // tm,),
        in_specs=[
            pl.BlockSpec((tm, d), lambda i: (i, 0)),
            pl.BlockSpec((tm, d), lambda i: (i, 1)),
            pl.BlockSpec((1, d), lambda i: (0, 0)),
            pl.BlockSpec((1, d), lambda i: (0, 0)),
            pl.BlockSpec((SG_GROUPS, CHUNK, CHUNK), lambda i: (0, 0, 0)),
            pl.BlockSpec((CHUNK, d), lambda i: (0, 0)),
        ],
        out_specs=pl.BlockSpec((tm, d), lambda i: (i, 0)),
        out_shape=jax.ShapeDtypeStruct((m, d), BF16),
        scratch_shapes=[pltpu.VMEM((tm, d), BF16)],
        compiler_params=_cparams(1),
    )(hg, hg, ln_w.reshape(1, d), ln_b.reshape(1, d), w_s.astype(BF16), bs_full)


def _split_bf16(x, pieces):
    out = []
    for _ in range(pieces):
        p = x.astype(BF16)
        out.append(p)
        x = x - p.astype(F32)
    return out


def _dot_exact_rhs(x, w, pieces=2):
    return sum(jnp.dot(p, w, preferred_element_type=F32) for p in _split_bf16(x, pieces))


def _dot_split(x, w_hi, w_lo):
    x_hi, x_lo = _split_bf16(x, 2)
    return (jnp.dot(x_hi, w_hi, preferred_element_type=F32)
            + (jnp.dot(x_hi, w_lo, preferred_element_type=F32)
               + jnp.dot(x_lo, w_hi, preferred_element_type=F32)))


def _hi_lo(w):
    hi = w.astype(BF16)
    return hi, (w - hi.astype(F32)).astype(BF16)


def _prep_body(z_ref, zp_ref, zn_ref, mup_ref, mun_ref, kk_ref, ka_ref, w0_ref, a0_ref,
               wup_ref, aup_ref, gup_ref, rk_ref, e_ref, et_ref,
               lw0_o, lw1_o, k0_o, k1_o, b0_o, b1_o, v_o, kkn_o, r_o, g_o, bonus_o,
               *, tiles_ctx, tiles_total):
    tm = z_ref.shape[0]
    rt = pl.program_id(0) % tiles_total
    in_ctx = rt < tiles_ctx
    has_prev = jnp.where(in_ctx, rt > 0, rt > tiles_ctx)
    has_next = jnp.where(in_ctx, rt < tiles_ctx - 1, rt < tiles_total - 1)

    z = z_ref[...]
    rows = lax.broadcasted_iota(jnp.int32, z.shape, 0)
    prev_row = jnp.where(has_prev, zp_ref[7:8, :], 0.0)
    next_row = jnp.where(has_next, zn_ref[0:1, :], 0.0)
    z_prev = jnp.where(rows == 0, prev_row, pltpu.roll(z, 1, 0))
    z_next = jnp.where(rows == tm - 1, next_row, pltpu.roll(z, tm - 1, 0))
    zs = z + mup_ref[...] * (z_prev - z) + mun_ref[...] * (z_next - z)

    d = RWKV_DIM
    r = zs[:, 0:d]
    k = zs[:, d:2 * d]
    v = zs[:, 2 * d:3 * d]
    wd = zs[:, 3 * d:3 * d + LANES]
    ad = zs[:, 3 * d + LANES:3 * d + 2 * LANES]
    gd = zs[:, 3 * d + 2 * LANES:3 * d + 4 * LANES]

    def head_sum(x):
        return _dot_exact_rhs(_dot_exact_rhs(x, e_ref[...]), et_ref[...])

    kk = k * kk_ref[...]
    nrm = jnp.sqrt(head_sum(kk * kk))
    kkn = kk / jnp.maximum(nrm, 1e-12)

    wx = w0_ref[...] + _dot_split(jnp.tanh(wd), wup_ref[0], wup_ref[1])
    lw = -float(np.exp(-0.5)) * jax.nn.sigmoid(wx)
    ax = a0_ref[...] + _dot_split(ad, aup_ref[0], aup_ref[1])
    iclr = jax.nn.sigmoid(ax)
    g = _dot_split(jax.nn.sigmoid(gd), gup_ref[0], gup_ref[1])

    ka = ka_ref[...]
    k0 = k * (1.0 + (iclr[:, :d] - 1.0) * ka)
    k1 = k * (1.0 + (iclr[:, d:] - 1.0) * ka)
    lw0_o[...] = lw[:, :d]
    lw1_o[...] = lw[:, d:]
    k0_o[...] = k0
    k1_o[...] = k1
    b0_o[...] = iclr[:, :d] * kkn
    b1_o[...] = iclr[:, d:] * kkn
    v_o[...] = v
    kkn_o[...] = kkn
    r_o[...] = r
    g_o[...] = g
    bonus_o[...] = head_sum(r * (k0 + k1) * rk_ref[...]) * v


def _head_indicator():
    head_of = jnp.arange(RWKV_DIM) // RWKV_HEAD
    return (head_of[:, None] == jnp.arange(LANES)[None, :]).astype(F32)


def _rwkv_prep(z_rw, p, rows_ctx, rows_total):
    m, w = z_rw.shape
    tm = PREP_TM
    d = RWKV_DIM
    assert rows_ctx % tm == 0 and rows_total % tm == 0 and m % rows_total == 0
    pad = w - RWKV_IN
    mup = jnp.pad(p['mu_prev'], (0, pad)).reshape(1, w)
    mun = jnp.pad(p['mu_next'], (0, pad)).reshape(1, w)
    zero = jnp.zeros((DECAY_LORA, d), F32)
    wup = jnp.concatenate([jnp.concatenate([p['w_up'][0], zero], axis=1),
                           jnp.concatenate([zero, p['w_up'][1]], axis=1)], axis=0)
    aup = jnp.concatenate([jnp.concatenate([p['a_up'][0], zero], axis=1),
                           jnp.concatenate([zero, p['a_up'][1]], axis=1)], axis=0)
    gup = jnp.pad(p['g_up'], ((0, 2 * LANES - GATE_LORA), (0, 0)))
    wup, aup, gup = (jnp.stack(_hi_lo(w)) for w in (wup, aup, gup))
    e = _head_indicator().astype(BF16)

    def full(shape):
        return pl.BlockSpec(shape, lambda i: (0,) * len(shape))

    n8 = m // 8
    out_spec = pl.BlockSpec((tm, d), lambda i: (i, 0))
    out_sds = jax.ShapeDtypeStruct((m, d), F32)
    return pl.pallas_call(
        functools.partial(_prep_body, tiles_ctx=rows_ctx // tm, tiles_total=rows_total // tm),
        name="rwkv_prep",
        grid=(m // tm,),
        in_specs=[
            pl.BlockSpec((tm, w), lambda i: (i, 0)),
            pl.BlockSpec((8, w), lambda i: (jnp.maximum(i * (tm // 8) - 1, 0), 0)),
            pl.BlockSpec((8, w), lambda i: (jnp.minimum((i + 1) * (tm // 8), n8 - 1), 0)),
            full((1, w)), full((1, w)), full((1, d)), full((1, d)), full((1, 2 * d)),
            full((1, 2 * d)), full((2, LANES, 2 * d)), full((2, LANES, 2 * d)),
            full((2, 2 * LANES, d)), full((1, d)), full((d, LANES)), full((LANES, d)),
        ],
        out_specs=[out_spec] * 11,
        out_shape=[out_sds] * 11,
        compiler_params=_cparams(1),
    )(z_rw, z_rw, z_rw, mup, mun, p['k_k'].reshape(1, d), p['k_a'].reshape(1, d),
      p['w0'].reshape(1, 2 * d), p['a0'].reshape(1, 2 * d), wup, aup, gup,
      p['r_k'].reshape(1, d), e, e.T)


def _bdot(x, y):
    return jnp.dot(x.astype(BF16), y.astype(BF16), preferred_element_type=F32)


def _chunk_units(units):
    c = RW_CHUNK
    row = lax.broadcasted_iota(jnp.int32, (c, LANES), 0)
    lane = lax.broadcasted_iota(jnp.int32, (c, LANES), 1)
    pos = lane % c
    lo = lane < c
    tr = lax.broadcasted_iota(jnp.int32, (c, c), 0)
    tc = lax.broadcasted_iota(jnp.int32, (c, c), 1)
    tri = {False: (tc <= tr).astype(BF16), True: (tc >= tr).astype(BF16)}
    strict = {False: pos < row, True: pos > row}
    incl = {False: pos <= row, True: pos >= row}
    last = {False: c - 1, True: 0}
    rev = [u[6] for u in units]
    nu = range(len(units))

    def sb(x):
        return jnp.concatenate([jnp.where(lo, x, 0.0), jnp.where(lo, 0.0, x)], axis=0)

    def nt(x, y):
        return lax.dot_general(x.astype(BF16), y.astype(BF16), (((1,), (1,)), ((), ())),
                               preferred_element_type=F32)

    def tn(x, y):
        return lax.dot_general(x.astype(BF16), y.astype(BF16), (((0,), (0,)), ((), ())),
                               preferred_element_type=F32)

    def fold(x):
        return jnp.where(lo, x[:c], 0.0) + jnp.where(lo, 0.0, x[c:])

    lw3 = [_split_bf16(units[i][0], 3) for i in nu]
    lc = [sum(jnp.dot(tri[rev[i]], piece, preferred_element_type=F32) for piece in lw3[i])
          for i in nu]
    ltot = [lc[i][last[rev[i]]:last[rev[i]] + 1, :] for i in nu]
    e_neg = [jnp.exp(-lc[i]) for i in nu]
    e_h = [jnp.exp(ltot[i] - lc[i]) for i in nu]
    at = [-units[i][3] * jnp.exp(lc[i] - units[i][0]) for i in nu]
    rt = [units[i][5] * jnp.exp(lc[i]) for i in nu]
    bt = [units[i][2] * e_neg[i] for i in nu]
    kt = [units[i][1] * e_neg[i] for i in nu]
    bh = [units[i][2] * e_h[i] for i in nu]
    kh = [units[i][1] * e_h[i] for i in nu]

    ar = [jnp.concatenate([at[i], rt[i]], axis=0) for i in nu]
    m_b = [nt(ar[i], sb(bt[i])) for i in nu]
    m_k = [nt(ar[i], sb(kt[i])) for i in nu]
    n = [jnp.where(strict[rev[i]], m_b[i][:c], 0.0) for i in nu]
    m_ak = [jnp.where(strict[rev[i]], m_k[i][:c], 0.0) for i in nu]
    m_rb = [jnp.where(incl[rev[i]], m_b[i][c:], 0.0) for i in nu]
    m_rk = [jnp.where(incl[rev[i]], m_k[i][c:], 0.0) for i in nu]
    mv = [_bdot(jnp.concatenate([m_ak[i], m_rk[i]], axis=0), sb(units[i][4])) for i in nu]
    xa = at
    xu = [mv[i][:c] for i in nu]
    n_steps = c.bit_length() - 1
    for step in range(n_steps):
        upd = [_bdot(n[i], jnp.concatenate([sb(xa[i]), sb(xu[i])], axis=1)) for i in nu]
        xa = [xa[i] + upd[i][:, :LANES] for i in nu]
        xu = [xu[i] + upd[i][:, LANES:] for i in nu]
        if step < n_steps - 1:
            n = [_bdot(n[i], sb(n[i])) for i in nu]
    rx = [_bdot(m_rb[i], jnp.concatenate([sb(xa[i]), sb(xu[i])], axis=1)) for i in nu]
    gh = [tn(bh[i], jnp.concatenate([xa[i], xu[i]], axis=1)) for i in nu]
    hk = [tn(kh[i], units[i][4]) for i in nu]
    out = []
    for i in nu:
        rp = rt[i] + rx[i][:, :LANES]
        y0 = rx[i][:, LANES:] + mv[i][c:]
        g = fold(gh[i][:, :LANES]) + jnp.where(pos == row, jnp.exp(ltot[i]), 0.0)
        h = fold(gh[i][:, LANES:] + hk[i])
        out.append((g, h, rp, y0))
    return out


def _chunk_body(lw0_ref, lw1_ref, k0_ref, k1_ref, b0_ref, b1_ref, v_ref, kkn_ref, r_ref,
                g0_o, h0_o, rp0_o, y00_o, g1_o, h1_o, rp1_o, y01_o):
    n_chunks = v_ref.shape[0] // RW_CHUNK
    units = []
    for ci in range(n_chunks):
        rows = slice(ci * RW_CHUNK, (ci + 1) * RW_CHUNK)
        v = v_ref[rows, :]
        kkn = kkn_ref[rows, :]
        r = r_ref[rows, :]
        units.append((lw0_ref[rows, :], k0_ref[rows, :], b0_ref[rows, :], kkn, v, r, False))
        units.append((lw1_ref[rows, :], k1_ref[rows, :], b1_ref[rows, :], kkn, v, r, True))
    res = _chunk_units(units)
    for ci in range(n_chunks):
        rows = slice(ci * RW_CHUNK, (ci + 1) * RW_CHUNK)
        for dr, outs in enumerate(((g0_o, h0_o, rp0_o, y00_o), (g1_o, h1_o, rp1_o, y01_o))):
            for o_ref, val in zip(outs, res[2 * ci + dr]):
                o_ref[rows, :] = val


def _rwkv_chunk(prep, tm=512):
    lw0, lw1, k0, k1, b0, b1, v, kkn, r = prep
    m, d = v.shape
    spec = pl.BlockSpec((tm, LANES), lambda i, pr: (i, pr))
    sds = jax.ShapeDtypeStruct((m, d), F32)
    return pl.pallas_call(
        _chunk_body, name="rwkv_chunk", grid=(m // tm, d // LANES),
        in_specs=[spec] * 9, out_specs=[spec] * 8, out_shape=[sds] * 8,
        compiler_params=_cparams(2),
    )(lw0, lw1, k0, k1, b0, b1, v, kkn, r)


def _scan_body(gf_ref, hf_ref, rpf_ref, y0f_ref, gb_ref, hb_ref, rpb_ref, y0b_ref,
               yf_ref, yb_ref, s_ref):
    c = RW_CHUNK
    step = pl.program_id(1)

    @pl.when(step == 0)
    def _():
        s_ref[...] = jnp.zeros(s_ref.shape, F32)

    lane = lax.broadcasted_iota(jnp.int32, (c, LANES), 1)
    lo = lane < c

    def sb(x):
        return jnp.concatenate([jnp.where(lo, x, 0.0), jnp.where(lo, 0.0, x)], axis=0)

    dirs = ((gf_ref, hf_ref, rpf_ref, y0f_ref, yf_ref), (gb_ref, hb_ref, rpb_ref, y0b_ref, yb_ref))
    cols = [slice(pr * LANES, (pr + 1) * LANES) for pr in range(N_PAIRS)]
    out = []
    for dr, (g_ref, _, rp_ref, _, _) in enumerate(dirs):
        lhs = [jnp.concatenate([rp_ref[:, cl], sb(g_ref[:, cl])], axis=0) for cl in cols]
        out.append([_bdot(lhs[pr], s_ref[dr, pr]) for pr in range(N_PAIRS)])
    for dr, (_, h_ref, _, y0_ref, y_ref) in enumerate(dirs):
        for pr, cl in enumerate(cols):
            y_ref[:, cl] = out[dr][pr][:c] + y0_ref[:, cl]
            s_ref[dr, pr] = out[dr][pr][c:] + sb(h_ref[:, cl])


def _rwkv_scan(mats, batch, chunks_ctx):
    m, d = mats[0].shape
    c = RW_CHUNK
    n_chunks = m // c // batch

    def chunk_of(b, s, reverse):
        if not reverse:
            return b * n_chunks + s
        rev = jnp.where(s < chunks_ctx, chunks_ctx - 1 - s, n_chunks - 1 - (s - chunks_ctx))
        return b * n_chunks + rev

    fwd = pl.BlockSpec((c, d), lambda b, s: (chunk_of(b, s, False), 0))
    bwd = pl.BlockSpec((c, d), lambda b, s: (chunk_of(b, s, True), 0))
    sds = jax.ShapeDtypeStruct((m, d), F32)
    return pl.pallas_call(
        _scan_body, name="rwkv_scan", grid=(batch, n_chunks),
        in_specs=[fwd] * 4 + [bwd] * 4, out_specs=[fwd, bwd], out_shape=[sds, sds],
        scratch_shapes=[pltpu.VMEM((2, N_PAIRS, LANES, LANES), F32)],
        compiler_params=_cparams(2),
    )(*mats)


def _readout_body(y0_ref, y1_ref, bonus_ref, g_ref, att_ref, lw_ref, lb_ref, e_ref, et_ref, o_ref):
    def head_mean(x):
        return _dot_exact_rhs(_dot_exact_rhs(x, e_ref[...]), et_ref[...]) * (1.0 / RWKV_HEAD)

    y = y0_ref[...] + y1_ref[...]
    mu = head_mean(y)
    yc = y - mu
    var = head_mean(yc * yc)
    yn = yc * lax.rsqrt(var + LNX_EPS) * lw_ref[...] + lb_ref[...]
    d_att = att_ref.shape[1]
    o_ref[:, :d_att] = att_ref[...]
    o_ref[:, d_att:] = ((yn + bonus_ref[...]) * g_ref[...]).astype(BF16)


def _rwkv_readout(y0, y1, bonus, g, att, p, rows_ctx, rows_total, tm=256):
    d = RWKV_DIM
    n_lat, d_att = att.shape
    assert rows_ctx % tm == 0 and rows_total % tm == 0
    per_b = (rows_total - rows_ctx) // tm
    lat0 = rows_ctx // tm
    tot = rows_total // tm
    e = _head_indicator().astype(BF16)
    rw_spec = pl.BlockSpec((tm, d), lambda i: ((i // per_b) * tot + lat0 + i % per_b, 0))

    def const(shape):
        return pl.BlockSpec(shape, lambda i: (0, 0))

    return pl.pallas_call(
        _readout_body, name="rwkv_readout", grid=(n_lat // tm,),
        in_specs=[rw_spec, rw_spec, rw_spec, rw_spec,
                  pl.BlockSpec((tm, d_att), lambda i: (i, 0)),
                  const((1, d)), const((1, d)), const((d, LANES)), const((LANES, d))],
        out_specs=pl.BlockSpec((tm, d_att + d), lambda i: (i, 0)),
        out_shape=jax.ShapeDtypeStruct((n_lat, d_att + d), BF16),
        compiler_params=_cparams(1),
    )(y0, y1, bonus, g, att, p['lnx_w'].reshape(1, d), p['lnx_b'].reshape(1, d), e, e.T)


def _rwkv_mixer(z_rw, att, p, batch, rows_ctx, rows_total):
    outs = _rwkv_prep(z_rw, p, rows_ctx, rows_total)
    g, bonus = outs[9], outs[10]
    mats = _rwkv_chunk(outs[:9])
    y0, y1 = _rwkv_scan(mats, batch, rows_ctx // RW_CHUNK)
    return _rwkv_readout(y0, y1, bonus, g, att, p, rows_ctx, rows_total)


def _rope_swap_cols(w):
    f = ROPE_FREQS
    parts = []
    for a in range(2):
        x1 = w[..., (2 * a) * f:(2 * a + 1) * f]
        x2 = w[..., (2 * a + 1) * f:(2 * a + 2) * f]
        parts += [-x2, x1]
    return jnp.concatenate(parts, axis=-1)


def _rope_tables(l):
    pos = np.arange(l)
    inv = ROPE_THETA ** (-np.arange(ROPE_FREQS, dtype=np.float64) / ROPE_FREQS)
    ar = (pos // GRID_W)[:, None] * inv
    ac = (pos % GRID_W)[:, None] * inv
    cos = np.concatenate([np.cos(ar), np.cos(ar), np.cos(ac), np.cos(ac)], axis=1)
    sin = np.concatenate([np.sin(ar), np.sin(ar), np.sin(ac), np.sin(ac)], axis=1)
    return cos.astype(np.float32), sin.astype(np.float32)


GATHER_UNROLL = 8


def _gather_body(tok_ref, tv_ref, x_hbm, o_ref, buf_ref, sem):
    i = pl.program_id(0)
    n_tiles = pl.num_programs(0)
    tm = o_ref.shape[0]

    def row_copy(tile, r):
        slot = tile % 2
        return pltpu.make_async_copy(x_hbm.at[pl.ds(tok_ref[tile * tm + r], 1), :],
                                     buf_ref.at[slot, pl.ds(r, 1), :], sem.at[slot])

    def start_tile(tile):
        def body(g, carry):
            for k in range(GATHER_UNROLL):
                row_copy(tile, g * GATHER_UNROLL + k).start(priority=k % 2)
            return carry
        lax.fori_loop(0, tm // GATHER_UNROLL, body, 0)

    def wait_tile(tile):
        def body(g, carry):
            for k in range(GATHER_UNROLL):
                row_copy(tile, g * GATHER_UNROLL + k).wait()
            return carry
        lax.fori_loop(0, tm // GATHER_UNROLL, body, 0)

    @pl.when(tv_ref[i] == 1)
    def _():
        @pl.when(i == 0)
        def _():
            start_tile(i)

        nxt = jnp.minimum(i + 1, n_tiles - 1)

        @pl.when(jnp.logical_and(i + 1 < n_tiles, tv_ref[nxt] == 1))
        def _():
            start_tile(i + 1)

        wait_tile(i)
        o_ref[...] = buf_ref[i % 2].astype(BF16)

    @pl.when(tv_ref[i] == 0)
    def _():
        o_ref[...] = jnp.zeros(o_ref.shape, o_ref.dtype)


def _gather_rows_bf16(x, row_tok, tile_valid, tm):
    d = x.shape[1]
    n_rows = row_tok.shape[0]
    gs = pltpu.PrefetchScalarGridSpec(
        num_scalar_prefetch=2,
        grid=(n_rows // tm,),
        in_specs=[pl.BlockSpec(memory_space=pl.ANY)],
        out_specs=pl.BlockSpec((tm, d), lambda i, tok, tv: (i, 0)),
        scratch_shapes=[pltpu.VMEM((2, tm, d), F32), pltpu.SemaphoreType.DMA((2,))])
    return pl.pallas_call(
        _gather_body, grid_spec=gs, name="moe_gather",
        out_shape=jax.ShapeDtypeStruct((n_rows, d), BF16),
        compiler_params=_cparams(1),
    )(row_tok, tile_valid, x)


def _moe(xn, logits, w1, w3, w2):
    n_tok = xn.shape[0]
    tm = MOE_TM
    top_val, top_idx = lax.top_k(logits, TOP_K)
    gate = jax.nn.softmax(top_val, axis=-1)
    n_assign = n_tok * TOP_K
    e_flat = top_idx.reshape(-1).astype(jnp.int32)
    experts = jnp.arange(N_EXPERTS, dtype=jnp.int32)
    order = jnp.argsort(e_flat).astype(jnp.int32)
    rank = jnp.argsort(order).astype(jnp.int32)
    counts = jnp.sum((e_flat[:, None] == experts[None, :]).astype(jnp.int32), axis=0)
    padded = (counts + tm - 1) // tm * tm
    pad_end = jnp.cumsum(padded)
    pad_start = pad_end - padded
    grp_start = jnp.cumsum(counts) - counts
    pos = (pad_start[e_flat] + rank - grp_start[e_flat]).reshape(n_tok, TOP_K)
    n_tiles = n_assign // tm + N_EXPERTS
    n_rows = n_tiles * tm

    def expert_of(row):
        return jnp.minimum(jnp.sum((row[:, None] >= pad_end[None, :]).astype(jnp.int32), axis=1),
                           N_EXPERTS - 1)

    rows = jnp.arange(n_rows, dtype=jnp.int32)
    e_row = expert_of(rows)
    off = rows - pad_start[e_row]
    src = jnp.clip(grp_start[e_row] + off, 0, n_assign - 1)
    row_tok = jnp.where(off < counts[e_row], order[src] // TOP_K, 0)
    tile_start = jnp.arange(n_tiles, dtype=jnp.int32) * tm
    n_valid = pad_end[-1] // tm
    tile_src = jnp.minimum(jnp.arange(n_tiles, dtype=jnp.int32), n_valid - 1).astype(jnp.int32)
    tile_exp = expert_of(tile_start)[tile_src]
    tile_valid = (tile_start < pad_end[-1]).astype(jnp.int32)
    group = (tile_exp, tile_valid, tile_src)

    def rows_of(a, idx):
        return a.at[idx].get(mode="promise_in_bounds")

    xs = _gather_rows_bf16(xn, row_tok.astype(jnp.int32), tile_valid, tm)
    hmid = _matmul(xs, [w1, w3], name="moe_up", tm=tm, tn=1024, out_dtype=BF16,
                   epilogue="swiglu", group=group)
    ys = _matmul(hmid, [w2], name="moe_down", tm=tm, tn=512, out_dtype=F32, group=group)
    return rows_of(ys, pos[:, 0]), rows_of(ys, pos[:, 1]), gate


def kernel(x, c, ctx, c_ctx, l0_ada_w, l0_ada_b, l0_norm1, l0_norm2, l0_w_in, l0_q_norm, l0_w_uq, l0_kv_norm, l0_w_ukv, l0_mu_prev, l0_mu_next, l0_w0, l0_w_up, l0_a0, l0_a_up, l0_g_up, l0_k_k, l0_k_a, l0_r_k, l0_lnx_w, l0_lnx_b, l0_w_o, l0_ffn_w1, l0_ffn_w3, l0_ffn_w2, l1_ada_w, l1_ada_b, l1_norm1, l1_norm2, l1_w_in, l1_v_ln_w, l1_v_ln_b, l1_w_s, l1_b_s, l1_w_o, l1_router, l1_moe_w1, l1_moe_w3, l1_moe_w2, final_norm):
    b, l, d = x.shape
    lc = ctx.shape[1]
    t = lc + l
    n_tok = b * l

    cond = jnp.zeros((8, d), F32).at[:b].set(c).at[b].set(c_ctx)
    mod0 = _matmul(cond, [l0_ada_w], name="ada_mod", tm=8, tn=1536, out_dtype=F32, prologue="silu",
                   epilogue="bias", bias=l0_ada_b)
    mod1 = _matmul(cond, [l1_ada_w], name="ada_mod", tm=8, tn=1536, out_dtype=F32, prologue="silu",
                   epilogue="bias", bias=l1_ada_b)

    def mods(mod, row0, nrows):
        return [mod[row0:row0 + nrows, i * d:(i + 1) * d].reshape(nrows, 1, d) for i in range(N_MOD)]

    sh1, sc1, g1, sh2, sc2, g2 = mods(mod0, 0, b)
    csh1, csc1 = mods(mod0, b, 1)[:2]

    xall = _norm_mod_merge(x, ctx, l0_norm1, sh1, sc1, csh1, csc1)
    mla_in = Q_LORA + KV_LORA + QK_ROPE
    w_kr = l0_w_in[:, Q_LORA + KV_LORA:mla_in]
    w_mla = jnp.concatenate([l0_w_in[:, :mla_in], _rope_swap_cols(w_kr)], axis=1)
    w_rw = jnp.pad(l0_w_in[:, mla_in:], ((0, 0), (0, RWKV_PAD - RWKV_IN)))
    xall2 = xall.reshape(b * t, d)
    z_all = _matmul(xall2, [w_mla], name="in_mla", tm=1088, tn=384, out_dtype=F32)
    z_rw = _matmul(xall2, [w_rw], name="in_rwkv", tm=1088, tn=512, out_dtype=F32)

    wq = l0_w_uq.reshape(Q_LORA, MLA_HEADS, QK_NOPE + QK_ROPE)
    wq_ext = jnp.concatenate([wq, _rope_swap_cols(wq[..., QK_NOPE:])], axis=-1)
    wq_ext = wq_ext.reshape(Q_LORA, MLA_HEADS * (QK_NOPE + 2 * QK_ROPE))
    tq_rows = 256
    per_b = l // tq_rows
    lat0 = lc // tq_rows

    def lat_rows(i):
        return (i // per_b) * (t // tq_rows) + lat0 + i % per_b

    q = _matmul(z_all, [wq_ext], name="q_up", tm=tq_rows, tn=2048, out_dtype=BF16, k=Q_LORA, x_col_block=0,
                x_row_map=lat_rows, m_out=n_tok, prologue="rms", gain=l0_q_norm)
    kv = _matmul(z_all, [l0_w_ukv], name="kv_up", tm=1088, tn=1024, out_dtype=BF16, k=KV_LORA, x_col_block=1,
                 prologue="rms", gain=l0_kv_norm)
    cos_l, sin_l = _rope_tables(l)
    cos_t = np.concatenate([np.ones((lc, QK_ROPE), np.float32), cos_l], axis=0)
    sin_t = np.concatenate([np.zeros((lc, QK_ROPE), np.float32), sin_l], axis=0)
    kr = _krope(z_all, cos_t, sin_t, col_block=(Q_LORA + KV_LORA) // LANES)
    att = _attention(q, kv.reshape(b, t, -1), kr.reshape(b, t, LANES), cos_l, sin_l, b)

    p0 = dict(mu_prev=l0_mu_prev, mu_next=l0_mu_next, w0=l0_w0, w_up=l0_w_up, a0=l0_a0,
              a_up=l0_a_up, g_up=l0_g_up, k_k=l0_k_k, k_a=l0_k_a, r_k=l0_r_k,
              lnx_w=l0_lnx_w, lnx_b=l0_lnx_b)
    mix = _rwkv_mixer(z_rw, att, p0, b, lc, t)
    h = _matmul(mix, [l0_w_o], name="mix_out", tm=1024, tn=512, out_dtype=F32, epilogue="resid",
                resid=x.reshape(n_tok, d), gate=g1, rows_per_gate=l)

    xn = _norm_mod(h.reshape(b, l, d), l0_norm2, sh2, sc2).reshape(n_tok, d)
    hmid = _matmul(xn, [l0_ffn_w1, l0_ffn_w3], name="ffn_up", tm=1024, tn=512, out_dtype=BF16,
                   epilogue="swiglu")
    h = _matmul(hmid, [l0_ffn_w2], name="ffn_down", tm=512, tn=512, out_dtype=F32, epilogue="resid",
                resid=h, gate=g2, rows_per_gate=l)

    sh1, sc1, g1, sh2, sc2, g2 = mods(mod1, 0, b)
    xn = _norm_mod(h.reshape(b, l, d), l1_norm1, sh1, sc1).reshape(n_tok, d)
    hg = _matmul(xn, [l1_w_in], name="gmlp_in", tm=1024, tn=512, out_dtype=BF16, epilogue="gelu")
    gated = _spatial_gate(hg, l1_v_ln_w, l1_v_ln_b, l1_w_s, l1_b_s)
    h = _matmul(gated, [l1_w_o], name="gmlp_out", tm=1024, tn=512, out_dtype=F32, epilogue="resid",
                resid=h, gate=g1, rows_per_gate=l)

    xn, logits = _norm_mod(h.reshape(b, l, d), l1_norm2, sh2, sc2, router=l1_router)
    ya, yb, gate = _moe(xn.reshape(n_tok, d), logits.reshape(n_tok, LANES)[:, :N_EXPERTS],
                        l1_moe_w1, l1_moe_w3, l1_moe_w2)
    return _combine_norm(h, ya, yb, gate, g2, final_norm, l).reshape(b, l, d)
```

```python
import functools

import jax
import jax.numpy as jnp
import numpy as np
from jax import lax
from jax.experimental import pallas as pl
from jax.experimental.pallas import tpu as pltpu

F32 = jnp.float32
BF16 = jnp.bfloat16

D_MODEL = 2048
GRID_W = 64
N_MOD = 6
NORM_EPS = 1e-6
V_HEAD = 128
MLA_HEADS = 8
Q_LORA = 512
KV_LORA = 512
QK_NOPE = 128
QK_ROPE = 64
ROPE_FREQS = 16
ROPE_THETA = 10000.0
RWKV_HEAD = 64
RWKV_HEADS = 16
RWKV_DIM = 1024
DECAY_LORA = 64
ICLR_LORA = 64
GATE_LORA = 160
LNX_EPS = 64e-5
RWKV_IN = 3 * RWKV_DIM + 2 * DECAY_LORA + 2 * ICLR_LORA + GATE_LORA
RWKV_PAD = 3584
RW_CHUNK = 64
PREP_TM = 128
N_PAIRS = RWKV_HEADS // 2
HI = lax.Precision.HIGHEST
CHUNK = 128
SG_GROUPS = 16
D_FF = 7168
N_EXPERTS = 8
TOP_K = 2

VMEM_LIMIT_BYTES = 56 * 1024 * 1024
LANES = 128

MOE_TM = 512


def _cparams(n_axes):
    return pltpu.CompilerParams(
        dimension_semantics=("arbitrary",) * n_axes,
        vmem_limit_bytes=VMEM_LIMIT_BYTES)


def _mm_body(*refs, n_w, prologue, epilogue, grouped):
    refs = list(refs)
    if grouped:
        te_ref, tv_ref, _ = refs[:3]
        refs = refs[3:]
    x_ref = refs.pop(0)
    w_refs = [refs.pop(0) for _ in range(n_w)]
    gain_ref = refs.pop(0) if prologue == "rms" else None
    bias_ref = refs.pop(0) if epilogue == "bias" else None
    if epilogue == "resid":
        resid_ref = refs.pop(0)
        gate_ref = refs.pop(0)
    o_ref, wbf_ref = refs

    i = pl.program_id(1)
    if grouped:
        new_w = jnp.logical_or(i == 0, te_ref[i] != te_ref[jnp.maximum(i - 1, 0)])
    else:
        new_w = i == 0

    @pl.when(new_w)
    def _():
        for n in range(n_w):
            wbf_ref[n] = w_refs[n][...].astype(BF16)

    def compute():
        x = x_ref[...]
        if prologue == "rms":
            xf = x.astype(F32)
            ms = jnp.mean(xf * xf, axis=-1, keepdims=True)
            x = xf * lax.rsqrt(ms + NORM_EPS) * gain_ref[...]
        elif prologue == "silu":
            xf = x.astype(F32)
            x = xf * jax.nn.sigmoid(xf)
        x = x.astype(BF16)
        acc = [jnp.dot(x, wbf_ref[n], preferred_element_type=F32) for n in range(n_w)]
        if epilogue == "swiglu":
            a = acc[0]
            out = a * jax.nn.sigmoid(a) * acc[1]
        elif epilogue == "gelu":
            a = acc[0]
            out = 0.5 * a * (1.0 + lax.erf(a * (2.0 ** -0.5)))
        elif epilogue == "bias":
            out = acc[0] + bias_ref[...]
        elif epilogue == "resid":
            out = resid_ref[...] + gate_ref[0] * acc[0]
        else:
            out = acc[0]
        o_ref[...] = out.astype(o_ref.dtype)

    if grouped:
        @pl.when(tv_ref[i] == 1)
        def _():
            compute()

        @pl.when(tv_ref[i] == 0)
        def _():
            o_ref[...] = jnp.zeros(o_ref.shape, o_ref.dtype)
    else:
        compute()


def _matmul(x, ws, *, name, tm, tn, out_dtype, k=None, x_col_block=0, x_row_map=None, m_out=None,
            prologue=None, gain=None, epilogue=None, bias=None, resid=None, gate=None,
            rows_per_gate=None, group=None):
    grouped = group is not None
    kdim = k if k is not None else x.shape[1]
    n = ws[0].shape[-1]
    m = m_out if m_out is not None else x.shape[0]
    assert m % tm == 0 and n % tn == 0, (m, tm, n, tn)
    n_w = len(ws)
    grid = (n // tn, m // tm)

    if grouped:
        def xmap(j, i, te, tv, ts):
            return (ts[i], x_col_block)

        def wmap(j, i, te, tv, ts):
            return (te[i], 0, j)

        def omap(j, i, te, tv, ts):
            return (i, j)
        w_spec = pl.BlockSpec((None, kdim, tn), wmap)
    else:
        def xmap(j, i):
            return ((x_row_map(i) if x_row_map is not None else i), x_col_block)

        def wmap(j, i):
            return (0, j)

        def omap(j, i):
            return (i, j)
        w_spec = pl.BlockSpec((kdim, tn), wmap)

    in_specs = [pl.BlockSpec((tm, kdim), xmap)] + [w_spec] * n_w
    args = [x] + list(ws)
    if prologue == "rms":
        in_specs.append(pl.BlockSpec((1, kdim), lambda j, i, *_: (0, 0)))
        args.append(gain.reshape(1, kdim))
    if epilogue == "bias":
        in_specs.append(pl.BlockSpec((1, tn), lambda j, i, *_: (0, j)))
        args.append(bias.reshape(1, n))
    if epilogue == "resid":
        in_specs.append(pl.BlockSpec((tm, tn), omap))
        args.append(resid)
        assert rows_per_gate % tm == 0
        tiles_per_gate = rows_per_gate // tm
        in_specs.append(pl.BlockSpec((1, 1, tn), lambda j, i, *_: (i // tiles_per_gate, 0, j)))
        args.append(gate)

    body = functools.partial(_mm_body, n_w=n_w, prologue=prologue, epilogue=epilogue,
                             grouped=grouped)
    gs = pltpu.PrefetchScalarGridSpec(
        num_scalar_prefetch=3 if grouped else 0,
        grid=grid,
        in_specs=in_specs,
        out_specs=pl.BlockSpec((tm, tn), omap),
        scratch_shapes=[pltpu.VMEM((n_w, kdim, tn), BF16)])
    call = pl.pallas_call(
        body, grid_spec=gs, name=name,
        out_shape=jax.ShapeDtypeStruct((m, n), out_dtype),
        compiler_params=_cparams(2))
    if grouped:
        return call(*group, *args)
    return call(*args)


def _norm_mod_math(v, g, sh, sc):
    ms = jnp.mean(v * v, axis=-1, keepdims=True)
    y = v * lax.rsqrt(ms + NORM_EPS) * g
    return y * (1.0 + sc) + sh


def _nm_merge_body(x_ref, c_ref, g_ref, sh_ref, sc_ref, csh_ref, csc_ref, o_ref):
    r = pl.program_id(1)

    @pl.when(r == 0)
    def _():
        o_ref[0] = _norm_mod_math(c_ref[0], g_ref[...], csh_ref[...], csc_ref[...]).astype(BF16)

    @pl.when(r > 0)
    def _():
        o_ref[0] = _norm_mod_math(x_ref[0], g_ref[...], sh_ref[0], sc_ref[0]).astype(BF16)


def _norm_mod_merge(x, ctx, gain, sh, sc, csh, csc):
    b, l, d = x.shape
    lc = ctx.shape[1]
    tm = lc
    assert l % tm == 0
    nt = l // tm + 1
    return pl.pallas_call(
        _nm_merge_body, name="norm_mod_merge",
        grid=(b, nt),
        in_specs=[
            pl.BlockSpec((1, tm, d), lambda bi, r: (bi, jnp.maximum(r - 1, 0), 0)),
            pl.BlockSpec((1, lc, d), lambda bi, r: (bi, 0, 0)),
            pl.BlockSpec((1, d), lambda bi, r: (0, 0)),
            pl.BlockSpec((1, 1, d), lambda bi, r: (bi, 0, 0)),
            pl.BlockSpec((1, 1, d), lambda bi, r: (bi, 0, 0)),
            pl.BlockSpec((1, d), lambda bi, r: (0, 0)),
            pl.BlockSpec((1, d), lambda bi, r: (0, 0)),
        ],
        out_specs=pl.BlockSpec((1, tm, d), lambda bi, r: (bi, r, 0)),
        out_shape=jax.ShapeDtypeStruct((b, lc + l, d), BF16),
        compiler_params=_cparams(2),
    )(x, ctx, gain.reshape(1, d), sh, sc, csh.reshape(1, d), csc.reshape(1, d))


def _nm_body(x_ref, g_ref, sh_ref, sc_ref, o_ref):
    o_ref[0] = _norm_mod_math(x_ref[0], g_ref[...], sh_ref[0], sc_ref[0]).astype(BF16)


def _nm_router_body(x_ref, g_ref, sh_ref, sc_ref, r_ref, o_ref, lg_ref):
    y = _norm_mod_math(x_ref[0], g_ref[...], sh_ref[0], sc_ref[0])
    o_ref[0] = y
    lg_ref[0] = jnp.dot(y, r_ref[...], precision=HI,
                        preferred_element_type=F32)


def _norm_mod(h, gain, sh, sc, router=None, tm=512):
    b, l, d = h.shape
    assert l % tm == 0
    in_specs = [
        pl.BlockSpec((1, tm, d), lambda bi, r: (bi, r, 0)),
        pl.BlockSpec((1, d), lambda bi, r: (0, 0)),
        pl.BlockSpec((1, 1, d), lambda bi, r: (bi, 0, 0)),
        pl.BlockSpec((1, 1, d), lambda bi, r: (bi, 0, 0)),
    ]
    o_spec = pl.BlockSpec((1, tm, d), lambda bi, r: (bi, r, 0))
    o_shape = jax.ShapeDtypeStruct((b, l, d), BF16)
    if router is None:
        return pl.pallas_call(
            _nm_body, name="norm_mod", grid=(b, l // tm), in_specs=in_specs, out_specs=o_spec,
            out_shape=o_shape, compiler_params=_cparams(2),
        )(h, gain.reshape(1, d), sh, sc)
    ne = router.shape[1]
    router_pad = jnp.pad(router, ((0, 0), (0, LANES - ne)))
    return pl.pallas_call(
        _nm_router_body, name="norm_mod_router", grid=(b, l // tm),
        in_specs=in_specs + [pl.BlockSpec((d, LANES), lambda bi, r: (0, 0))],
        out_specs=[o_spec, pl.BlockSpec((1, tm, LANES), lambda bi, r: (bi, r, 0))],
        out_shape=[jax.ShapeDtypeStruct((b, l, d), F32),
                   jax.ShapeDtypeStruct((b, l, LANES), F32)],
        compiler_params=_cparams(2),
    )(h, gain.reshape(1, d), sh, sc, router_pad)


def _combine_norm_body(h_ref, ya_ref, yb_ref, gt_ref, g2_ref, gain_ref, o_ref):
    gt = gt_ref[...]
    y = ya_ref[...] * gt[:, 0:1] + yb_ref[...] * gt[:, 1:2]
    v = h_ref[...] + g2_ref[0] * y
    ms = jnp.mean(v * v, axis=-1, keepdims=True)
    o_ref[...] = v * lax.rsqrt(ms + NORM_EPS) * gain_ref[...]


def _combine_norm(h, ya, yb, gate, g2, gain, rows_per_gate, tm=256):
    m, d = h.shape
    gt = jnp.pad(gate, ((0, 0), (0, LANES - gate.shape[1])))
    tiles_per_gate = rows_per_gate // tm
    row = pl.BlockSpec((tm, d), lambda i: (i, 0))
    return pl.pallas_call(
        _combine_norm_body, name="combine_norm", grid=(m // tm,),
        in_specs=[row, row, row, pl.BlockSpec((tm, LANES), lambda i: (i, 0)),
                  pl.BlockSpec((1, 1, d), lambda i: (i // tiles_per_gate, 0, 0)),
                  pl.BlockSpec((1, d), lambda i: (0, 0))],
        out_specs=row,
        out_shape=jax.ShapeDtypeStruct((m, d), F32),
        compiler_params=_cparams(1),
    )(h, ya, yb, gt, g2, gain.reshape(1, d))


def _krope_body(z_ref, cos_ref, sin_ref, o_ref):
    z = z_ref[...]
    rot = z[:, :QK_ROPE] * cos_ref[...] + z[:, QK_ROPE:] * sin_ref[...]
    o_ref[...] = jnp.concatenate([rot, jnp.zeros_like(rot)], axis=1).astype(BF16)


def _krope(z_all, cos_t, sin_t, col_block, tm=256):
    m = z_all.shape[0]
    t = cos_t.shape[0]
    tiles_per_batch = t // tm
    return pl.pallas_call(
        _krope_body, name="krope", grid=(m // tm,),
        in_specs=[pl.BlockSpec((tm, LANES), lambda i: (i, col_block)),
                  pl.BlockSpec((tm, QK_ROPE), lambda i: (i % tiles_per_batch, 0)),
                  pl.BlockSpec((tm, QK_ROPE), lambda i: (i % tiles_per_batch, 0))],
        out_specs=pl.BlockSpec((tm, LANES), lambda i: (i, 0)),
        out_shape=jax.ShapeDtypeStruct((m, LANES), BF16),
        compiler_params=_cparams(1),
    )(z_all, cos_t, sin_t)


ATTN_HEADS_PER_STEP = 2


def _attn_body(q_ref, kv_ref, kr_ref, cos_ref, sin_ref, o_ref, kfull_ref, *, scale):
    qi = pl.program_id(2)
    hw = QK_NOPE + 2 * QK_ROPE
    kvw = QK_NOPE + V_HEAD
    heads = range(ATTN_HEADS_PER_STEP)

    @pl.when(qi == 0)
    def _():
        for hh in heads:
            kfull_ref[hh, :, :QK_NOPE] = kv_ref[0, :, hh * kvw:hh * kvw + QK_NOPE]
            kfull_ref[hh, :, QK_NOPE:] = kr_ref[0]

    def rotated_query(hh):
        q = q_ref[:, hh * hw:(hh + 1) * hw].astype(F32)
        qn = q[:, :QK_NOPE]
        qr = (q[:, QK_NOPE:QK_NOPE + QK_ROPE] * cos_ref[...]
              + q[:, QK_NOPE + QK_ROPE:] * sin_ref[...])
        return (jnp.concatenate([qn, qr, jnp.zeros_like(qr)], axis=1) * scale).astype(BF16)

    qf = [rotated_query(hh) for hh in heads]
    s = [lax.dot_general(qf[hh], kfull_ref[hh], (((1,), (1,)), ((), ())),
                         preferred_element_type=F32) for hh in heads]
    m = [jnp.max(s[hh], axis=-1, keepdims=True) for hh in heads]
    p = [jnp.exp2(s[hh] - m[hh]) for hh in heads]
    l = [jnp.sum(p[hh], axis=-1, keepdims=True) for hh in heads]
    o = [jnp.dot(p[hh].astype(BF16), kv_ref[0, :, hh * kvw + QK_NOPE:(hh + 1) * kvw],
                 preferred_element_type=F32) for hh in heads]
    for hh in heads:
        o_ref[:, hh * V_HEAD:(hh + 1) * V_HEAD] = (o[hh] / l[hh]).astype(BF16)


def _attention(q, kv, kr, cos_q, sin_q, batch, tq=256):
    m = q.shape[0]
    l = m // batch
    t = kv.shape[1]
    nq = l // tq
    nh = ATTN_HEADS_PER_STEP
    scale = float((QK_NOPE + QK_ROPE) ** -0.5 * np.log2(np.e))
    hw = QK_NOPE + 2 * QK_ROPE
    return pl.pallas_call(
        functools.partial(_attn_body, scale=scale), name="attention",
        grid=(batch, MLA_HEADS // nh, nq),
        in_specs=[
            pl.BlockSpec((tq, nh * hw), lambda b, h, i: (b * nq + i, h)),
            pl.BlockSpec((1, t, nh * (QK_NOPE + V_HEAD)), lambda b, h, i: (b, 0, h)),
            pl.BlockSpec((1, t, LANES), lambda b, h, i: (b, 0, 0)),
            pl.BlockSpec((tq, QK_ROPE), lambda b, h, i: (i, 0)),
            pl.BlockSpec((tq, QK_ROPE), lambda b, h, i: (i, 0)),
        ],
        out_specs=pl.BlockSpec((tq, nh * V_HEAD), lambda b, h, i: (b * nq + i, h)),
        out_shape=jax.ShapeDtypeStruct((m, MLA_HEADS * V_HEAD), BF16),
        scratch_shapes=[pltpu.VMEM((nh, t, QK_NOPE + LANES), BF16)],
        compiler_params=_cparams(3),
    )(q, kv, kr, cos_q, sin_q)


def _sgu_body(u_ref, v_ref, lw_ref, lb_ref, ws_ref, bs_ref, o_ref, vn_ref):
    v = v_ref[...].astype(F32)
    mu = jnp.mean(v, axis=-1, keepdims=True)
    vc = v - mu
    var = jnp.mean(vc * vc, axis=-1, keepdims=True)
    vn_ref[...] = (vc * lax.rsqrt(var + 1e-5) * lw_ref[...] + lb_ref[...]).astype(BF16)
    n_chunks = v_ref.shape[0] // CHUNK
    for n in range(n_chunks):
        rows = slice(n * CHUNK, (n + 1) * CHUNK)
        for g in range(SG_GROUPS):
            cols = slice(g * LANES, (g + 1) * LANES)
            vm = jnp.dot(ws_ref[g], vn_ref[rows, cols], preferred_element_type=F32)
            vm = vm + bs_ref[:, cols]
            o_ref[rows, cols] = (u_ref[rows, cols].astype(F32) * vm).astype(BF16)


def _spatial_gate(hg, ln_w, ln_b, w_s, b_s, tm=256):
    m = hg.shape[0]
    d = hg.shape[1] // 2
    bs_full = jnp.repeat(b_s.T, d // SG_GROUPS, axis=1)
    return pl.pallas_call(
        _sgu_body, name="spatial_gate", grid=(m // tm,),
        in_specs=[
            pl.BlockSpec((tm, d), lambda i: (i, 0)),
            pl.BlockSpec((tm, d), lambda i: (i, 1)),
            pl.BlockSpec((1, d), lambda i: (0, 0)),
            pl.BlockSpec((1, d), lambda i: (0, 0)),
            pl.BlockSpec((SG_GROUPS, CHUNK, CHUNK), lambda i: (0, 0, 0)),
            pl.BlockSpec((CHUNK, d), lambda i: (0, 0)),
        ],
        out_specs=pl.BlockSpec((tm, d), lambda i: (i, 0)),
        out_shape=jax.ShapeDtypeStruct((m, d), BF16),
        scratch_shapes=[pltpu.VMEM((tm, d), BF16)],
        compiler_params=_cparams(1),
    )(hg, hg, ln_w.reshape(1, d), ln_b.reshape(1, d), w_s.astype(BF16), bs_full)


def _split_bf16(x, pieces):
    out = []
    for _ in range(pieces):
        p = x.astype(BF16)
        out.append(p)
        x = x - p.astype(F32)
    return out


def _dot_exact_rhs(x, w, pieces=2):
    return sum(jnp.dot(p, w, preferred_element_type=F32) for p in _split_bf16(x, pieces))


def _dot_split(x, w_hi, w_lo):
    x_hi, x_lo = _split_bf16(x, 2)
    return (jnp.dot(x_hi, w_hi, preferred_element_type=F32)
            + (jnp.dot(x_hi, w_lo, preferred_element_type=F32)
               + jnp.dot(x_lo, w_hi, preferred_element_type=F32)))


def _hi_lo(w):
    hi = w.astype(BF16)
    return hi, (w - hi.astype(F32)).astype(BF16)


def _prep_body(z_ref, zp_ref, zn_ref, mup_ref, mun_ref, kk_ref, ka_ref, w0_ref, a0_ref,
               wup_ref, aup_ref, gup_ref, rk_ref, e_ref, et_ref,
               lw0_o, lw1_o, k0_o, k1_o, b0_o, b1_o, v_o, kkn_o, r_o, g_o, bonus_o,
               *, tiles_ctx, tiles_total):
    tm = z_ref.shape[0]
    rt = pl.program_id(0) % tiles_total
    in_ctx = rt < tiles_ctx
    has_prev = jnp.where(in_ctx, rt > 0, rt > tiles_ctx)
    has_next = jnp.where(in_ctx, rt < tiles_ctx - 1, rt < tiles_total - 1)

    z = z_ref[...]
    rows = lax.broadcasted_iota(jnp.int32, z.shape, 0)
    prev_row = jnp.where(has_prev, zp_ref[7:8, :], 0.0)
    next_row = jnp.where(has_next, zn_ref[0:1, :], 0.0)
    z_prev = jnp.where(rows == 0, prev_row, pltpu.roll(z, 1, 0))
    z_next = jnp.where(rows == tm - 1, next_row, pltpu.roll(z, tm - 1, 0))
    zs = z + mup_ref[...] * (z_prev - z) + mun_ref[...] * (z_next - z)

    d = RWKV_DIM
    r = zs[:, 0:d]
    k = zs[:, d:2 * d]
    v = zs[:, 2 * d:3 * d]
    wd = zs[:, 3 * d:3 * d + LANES]
    ad = zs[:, 3 * d + LANES:3 * d + 2 * LANES]
    gd = zs[:, 3 * d + 2 * LANES:3 * d + 4 * LANES]

    def head_sum(x):
        return _dot_exact_rhs(_dot_exact_rhs(x, e_ref[...]), et_ref[...])

    kk = k * kk_ref[...]
    nrm = jnp.sqrt(head_sum(kk * kk))
    kkn = kk / jnp.maximum(nrm, 1e-12)

    wx = w0_ref[...] + _dot_split(jnp.tanh(wd), wup_ref[0], wup_ref[1])
    lw = -float(np.exp(-0.5)) * jax.nn.sigmoid(wx)
    ax = a0_ref[...] + _dot_split(ad, aup_ref[0], aup_ref[1])
    iclr = jax.nn.sigmoid(ax)
    g = _dot_split(jax.nn.sigmoid(gd), gup_ref[0], gup_ref[1])

    ka = ka_ref[...]
    k0 = k * (1.0 + (iclr[:, :d] - 1.0) * ka)
    k1 = k * (1.0 + (iclr[:, d:] - 1.0) * ka)
    lw0_o[...] = lw[:, :d]
    lw1_o[...] = lw[:, d:]
    k0_o[...] = k0
    k1_o[...] = k1
    b0_o[...] = iclr[:, :d] * kkn
    b1_o[...] = iclr[:, d:] * kkn
    v_o[...] = v
    kkn_o[...] = kkn
    r_o[...] = r
    g_o[...] = g
    bonus_o[...] = head_sum(r * (k0 + k1) * rk_ref[...]) * v


def _head_indicator():
    head_of = jnp.arange(RWKV_DIM) // RWKV_HEAD
    return (head_of[:, None] == jnp.arange(LANES)[None, :]).astype(F32)


def _rwkv_prep(z_rw, p, rows_ctx, rows_total):
    m, w = z_rw.shape
    tm = PREP_TM
    d = RWKV_DIM
    assert rows_ctx % tm == 0 and rows_total % tm == 0 and m % rows_total == 0
    pad = w - RWKV_IN
    mup = jnp.pad(p['mu_prev'], (0, pad)).reshape(1, w)
    mun = jnp.pad(p['mu_next'], (0, pad)).reshape(1, w)
    zero = jnp.zeros((DECAY_LORA, d), F32)
    wup = jnp.concatenate([jnp.concatenate([p['w_up'][0], zero], axis=1),
                           jnp.concatenate([zero, p['w_up'][1]], axis=1)], axis=0)
    aup = jnp.concatenate([jnp.concatenate([p['a_up'][0], zero], axis=1),
                           jnp.concatenate([zero, p['a_up'][1]], axis=1)], axis=0)
    gup = jnp.pad(p['g_up'], ((0, 2 * LANES - GATE_LORA), (0, 0)))
    wup, aup, gup = (jnp.stack(_hi_lo(w)) for w in (wup, aup, gup))
    e = _head_indicator().astype(BF16)

    def full(shape):
        return pl.BlockSpec(shape, lambda i: (0,) * len(shape))

    n8 = m // 8
    out_spec = pl.BlockSpec((tm, d), lambda i: (i, 0))
    out_sds = jax.ShapeDtypeStruct((m, d), F32)
    return pl.pallas_call(
        functools.partial(_prep_body, tiles_ctx=rows_ctx // tm, tiles_total=rows_total // tm),
        name="rwkv_prep",
        grid=(m // tm,),
        in_specs=[
            pl.BlockSpec((tm, w), lambda i: (i, 0)),
            pl.BlockSpec((8, w), lambda i: (jnp.maximum(i * (tm // 8) - 1, 0), 0)),
            pl.BlockSpec((8, w), lambda i: (jnp.minimum((i + 1) * (tm // 8), n8 - 1), 0)),
            full((1, w)), full((1, w)), full((1, d)), full((1, d)), full((1, 2 * d)),
            full((1, 2 * d)), full((2, LANES, 2 * d)), full((2, LANES, 2 * d)),
            full((2, 2 * LANES, d)), full((1, d)), full((d, LANES)), full((LANES, d)),
        ],
        out_specs=[out_spec] * 11,
        out_shape=[out_sds] * 11,
        compiler_params=_cparams(1),
    )(z_rw, z_rw, z_rw, mup, mun, p['k_k'].reshape(1, d), p['k_a'].reshape(1, d),
      p['w0'].reshape(1, 2 * d), p['a0'].reshape(1, 2 * d), wup, aup, gup,
      p['r_k'].reshape(1, d), e, e.T)


def _bdot(x, y):
    return jnp.dot(x.astype(BF16), y.astype(BF16), preferred_element_type=F32)


def _chunk_units(units):
    c = RW_CHUNK
    row = lax.broadcasted_iota(jnp.int32, (c, LANES), 0)
    lane = lax.broadcasted_iota(jnp.int32, (c, LANES), 1)
    pos = lane % c
    lo = lane < c
    tr = lax.broadcasted_iota(jnp.int32, (c, c), 0)
    tc = lax.broadcasted_iota(jnp.int32, (c, c), 1)
    tri = {False: (tc <= tr).astype(BF16), True: (tc >= tr).astype(BF16)}
    strict = {False: pos < row, True: pos > row}
    incl = {False: pos <= row, True: pos >= row}
    last = {False: c - 1, True: 0}
    rev = [u[6] for u in units]
    nu = range(len(units))

    def sb(x):
        return jnp.concatenate([jnp.where(lo, x, 0.0), jnp.where(lo, 0.0, x)], axis=0)

    def nt(x, y):
        return lax.dot_general(x.astype(BF16), y.astype(BF16), (((1,), (1,)), ((), ())),
                               preferred_element_type=F32)

    def tn(x, y):
        return lax.dot_general(x.astype(BF16), y.astype(BF16), (((0,), (0,)), ((), ())),
                               preferred_element_type=F32)

    def fold(x):
        return jnp.where(lo, x[:c], 0.0) + jnp.where(lo, 0.0, x[c:])

    lw3 = [_split_bf16(units[i][0], 3) for i in nu]
    lc = [sum(jnp.dot(tri[rev[i]], piece, preferred_element_type=F32) for piece in lw3[i])
          for i in nu]
    ltot = [lc[i][last[rev[i]]:last[rev[i]] + 1, :] for i in nu]
    e_neg = [jnp.exp(-lc[i]) for i in nu]
    e_h = [jnp.exp(ltot[i] - lc[i]) for i in nu]
    at = [-units[i][3] * jnp.exp(lc[i] - units[i][0]) for i in nu]
    rt = [units[i][5] * jnp.exp(lc[i]) for i in nu]
    bt = [units[i][2] * e_neg[i] for i in nu]
    kt = [units[i][1] * e_neg[i] for i in nu]
    bh = [units[i][2] * e_h[i] for i in nu]
    kh = [units[i][1] * e_h[i] for i in nu]

    ar = [jnp.concatenate([at[i], rt[i]], axis=0) for i in nu]
    m_b = [nt(ar[i], sb(bt[i])) for i in nu]
    m_k = [nt(ar[i], sb(kt[i])) for i in nu]
    n = [jnp.where(strict[rev[i]], m_b[i][:c], 0.0) for i in nu]
    m_ak = [jnp.where(strict[rev[i]], m_k[i][:c], 0.0) for i in nu]
    m_rb = [jnp.where(incl[rev[i]], m_b[i][c:], 0.0) for i in nu]
    m_rk = [jnp.where(incl[rev[i]], m_k[i][c:], 0.0) for i in nu]
    mv = [_bdot(jnp.concatenate([m_ak[i], m_rk[i]], axis=0), sb(units[i][4])) for i in nu]
    xa = at
    xu = [mv[i][:c] for i in nu]
    n_steps = c.bit_length() - 1
    for step in range(n_steps):
        upd = [_bdot(n[i], jnp.concatenate([sb(xa[i]), sb(xu[i])], axis=1)) for i in nu]
        xa = [xa[i] + upd[i][:, :LANES] for i in nu]
        xu = [xu[i] + upd[i][:, LANES:] for i in nu]
        if step < n_steps - 1:
            n = [_bdot(n[i], sb(n[i])) for i in nu]
    rx = [_bdot(m_rb[i], jnp.concatenate([sb(xa[i]), sb(xu[i])], axis=1)) for i in nu]
    gh = [tn(bh[i], jnp.concatenate([xa[i], xu[i]], axis=1)) for i in nu]
    hk = [tn(kh[i], units[i][4]) for i in nu]
    out = []
    for i in nu:
        rp = rt[i] + rx[i][:, :LANES]
        y0 = rx[i][:, LANES:] + mv[i][c:]
        g = fold(gh[i][:, :LANES]) + jnp.where(pos == row, jnp.exp(ltot[i]), 0.0)
        h = fold(gh[i][:, LANES:] + hk[i])
        out.append((g, h, rp, y0))
    return out


def _chunk_body(lw0_ref, lw1_ref, k0_ref, k1_ref, b0_ref, b1_ref, v_ref, kkn_ref, r_ref,
                g0_o, h0_o, rp0_o, y00_o, g1_o, h1_o, rp1_o, y01_o):
    n_chunks = v_ref.shape[0] // RW_CHUNK
    units = []
    for ci in range(n_chunks):
        rows = slice(ci * RW_CHUNK, (ci + 1) * RW_CHUNK)
        v = v_ref[rows, :]
        kkn = kkn_ref[rows, :]
        r = r_ref[rows, :]
        units.append((lw0_ref[rows, :], k0_ref[rows, :], b0_ref[rows, :], kkn, v, r, False))
        units.append((lw1_ref[rows, :], k1_ref[rows, :], b1_ref[rows, :], kkn, v, r, True))
    res = _chunk_units(units)
    for ci in range(n_chunks):
        rows = slice(ci * RW_CHUNK, (ci + 1) * RW_CHUNK)
        for dr, outs in enumerate(((g0_o, h0_o, rp0_o, y00_o), (g1_o, h1_o, rp1_o, y01_o))):
            for o_ref, val in zip(outs, res[2 * ci + dr]):
                o_ref[rows, :] = val


def _rwkv_chunk(prep, tm=512):
    lw0, lw1, k0, k1, b0, b1, v, kkn, r = prep
    m, d = v.shape
    spec = pl.BlockSpec((tm, LANES), lambda i, pr: (i, pr))
    sds = jax.ShapeDtypeStruct((m, d), F32)
    return pl.pallas_call(
        _chunk_body, name="rwkv_chunk", grid=(m // tm, d // LANES),
        in_specs=[spec] * 9, out_specs=[spec] * 8, out_shape=[sds] * 8,
        compiler_params=_cparams(2),
    )(lw0, lw1, k0, k1, b0, b1, v, kkn, r)


def _scan_body(gf_ref, hf_ref, rpf_ref, y0f_ref, gb_ref, hb_ref, rpb_ref, y0b_ref,
               yf_ref, yb_ref, s_ref):
    c = RW_CHUNK
    step = pl.program_id(1)

    @pl.when(step == 0)
    def _():
        s_ref[...] = jnp.zeros(s_ref.shape, F32)

    lane = lax.broadcasted_iota(jnp.int32, (c, LANES), 1)
    lo = lane < c

    def sb(x):
        return jnp.concatenate([jnp.where(lo, x, 0.0), jnp.where(lo, 0.0, x)], axis=0)

    dirs = ((gf_ref, hf_ref, rpf_ref, y0f_ref, yf_ref), (gb_ref, hb_ref, rpb_ref, y0b_ref, yb_ref))
    cols = [slice(pr * LANES, (pr + 1) * LANES) for pr in range(N_PAIRS)]
    out = []
    for dr, (g_ref, _, rp_ref, _, _) in enumerate(dirs):
        lhs = [jnp.concatenate([rp_ref[:, cl], sb(g_ref[:, cl])], axis=0) for cl in cols]
        out.append([_bdot(lhs[pr], s_ref[dr, pr]) for pr in range(N_PAIRS)])
    for dr, (_, h_ref, _, y0_ref, y_ref) in enumerate(dirs):
        for pr, cl in enumerate(cols):
            y_ref[:, cl] = out[dr][pr][:c] + y0_ref[:, cl]
            s_ref[dr, pr] = out[dr][pr][c:] + sb(h_ref[:, cl])


def _rwkv_scan(mats, batch, chunks_ctx):
    m, d = mats[0].shape
    c = RW_CHUNK
    n_chunks = m // c // batch

    def chunk_of(b, s, reverse):
        if not reverse:
            return b * n_chunks + s
        rev = jnp.where(s < chunks_ctx, chunks_ctx - 1 - s, n_chunks - 1 - (s - chunks_ctx))
        return b * n_chunks + rev

    fwd = pl.BlockSpec((c, d), lambda b, s: (chunk_of(b, s, False), 0))
    bwd = pl.BlockSpec((c, d), lambda b, s: (chunk_of(b, s, True), 0))
    sds = jax.ShapeDtypeStruct((m, d), F32)
    return pl.pallas_call(
        _scan_body, name="rwkv_scan", grid=(batch, n_chunks),
        in_specs=[fwd] * 4 + [bwd] * 4, out_specs=[fwd, bwd], out_shape=[sds, sds],
        scratch_shapes=[pltpu.VMEM((2, N_PAIRS, LANES, LANES), F32)],
        compiler_params=_cparams(2),
    )(*mats)


def _readout_body(y0_ref, y1_ref, bonus_ref, g_ref, att_ref, lw_ref, lb_ref, e_ref, et_ref, o_ref):
    def head_mean(x):
        return _dot_exact_rhs(_dot_exact_rhs(x, e_ref[...]), et_ref[...]) * (1.0 / RWKV_HEAD)

    y = y0_ref[...] + y1_ref[...]
    mu = head_mean(y)
    yc = y - mu
    var = head_mean(yc * yc)
    yn = yc * lax.rsqrt(var + LNX_EPS) * lw_ref[...] + lb_ref[...]
    d_att = att_ref.shape[1]
    o_ref[:, :d_att] = att_ref[...]
    o_ref[:, d_att:] = ((yn + bonus_ref[...]) * g_ref[...]).astype(BF16)


def _rwkv_readout(y0, y1, bonus, g, att, p, rows_ctx, rows_total, tm=256):
    d = RWKV_DIM
    n_lat, d_att = att.shape
    assert rows_ctx % tm == 0 and rows_total % tm == 0
    per_b = (rows_total - rows_ctx) // tm
    lat0 = rows_ctx // tm
    tot = rows_total // tm
    e = _head_indicator().astype(BF16)
    rw_spec = pl.BlockSpec((tm, d), lambda i: ((i // per_b) * tot + lat0 + i % per_b, 0))

    def const(shape):
        return pl.BlockSpec(shape, lambda i: (0, 0))

    return pl.pallas_call(
        _readout_body, name="rwkv_readout", grid=(n_lat // tm,),
        in_specs=[rw_spec, rw_spec, rw_spec, rw_spec,
                  pl.BlockSpec((tm, d_att), lambda i: (i, 0)),
                  const((1, d)), const((1, d)), const((d, LANES)), const((LANES, d))],
        out_specs=pl.BlockSpec((tm, d_att + d), lambda i: (i, 0)),
        out_shape=jax.ShapeDtypeStruct((n_lat, d_att + d), BF16),
        compiler_params=_cparams(1),
    )(y0, y1, bonus, g, att, p['lnx_w'].reshape(1, d), p['lnx_b'].reshape(1, d), e, e.T)


def _rwkv_mixer(z_rw, att, p, batch, rows_ctx, rows_total):
    outs = _rwkv_prep(z_rw, p, rows_ctx, rows_total)
    g, bonus = outs[9], outs[10]
    mats = _rwkv_chunk(outs[:9])
    y0, y1 = _rwkv_scan(mats, batch, rows_ctx // RW_CHUNK)
    return _rwkv_readout(y0, y1, bonus, g, att, p, rows_ctx, rows_total)


def _rope_swap_cols(w):
    f = ROPE_FREQS
    parts = []
    for a in range(2):
        x1 = w[..., (2 * a) * f:(2 * a + 1) * f]
        x2 = w[..., (2 * a + 1) * f:(2 * a + 2) * f]
        parts += [-x2, x1]
    return jnp.concatenate(parts, axis=-1)


def _rope_tables(l):
    pos = np.arange(l)
    inv = ROPE_THETA ** (-np.arange(ROPE_FREQS, dtype=np.float64) / ROPE_FREQS)
    ar = (pos // GRID_W)[:, None] * inv
    ac = (pos % GRID_W)[:, None] * inv
    cos = np.concatenate([np.cos(ar), np.cos(ar), np.cos(ac), np.cos(ac)], axis=1)
    sin = np.concatenate([np.sin(ar), np.sin(ar), np.sin(ac), np.sin(ac)], axis=1)
    return cos.astype(np.float32), sin.astype(np.float32)


GATHER_UNROLL = 8


def _gather_body(tok_ref, tv_ref, x_hbm, o_ref, buf_ref, sem):
    i = pl.program_id(0)
    n_tiles = pl.num_programs(0)
    tm = o_ref.shape[0]

    def row_copy(tile, r):
        slot = tile % 2
        return pltpu.make_async_copy(x_hbm.at[pl.ds(tok_ref[tile * tm + r], 1), :],
                                     buf_ref.at[slot, pl.ds(r, 1), :], sem.at[slot])

    def start_tile(tile):
        def body(g, carry):
            for k in range(GATHER_UNROLL):
                row_copy(tile, g * GATHER_UNROLL + k).start(priority=k % 2)
            return carry
        lax.fori_loop(0, tm // GATHER_UNROLL, body, 0)

    def wait_tile(tile):
        def body(g, carry):
            for k in range(GATHER_UNROLL):
                row_copy(tile, g * GATHER_UNROLL + k).wait()
            return carry
        lax.fori_loop(0, tm // GATHER_UNROLL, body, 0)

    @pl.when(tv_ref[i] == 1)
    def _():
        @pl.when(i == 0)
        def _():
            start_tile(i)

        nxt = jnp.minimum(i + 1, n_tiles - 1)

        @pl.when(jnp.logical_and(i + 1 < n_tiles, tv_ref[nxt] == 1))
        def _():
            start_tile(i + 1)

        wait_tile(i)
        o_ref[...] = buf_ref[i % 2].astype(BF16)

    @pl.when(tv_ref[i] == 0)
    def _():
        o_ref[...] = jnp.zeros(o_ref.shape, o_ref.dtype)


def _gather_rows_bf16(x, row_tok, tile_valid, tm):
    d = x.shape[1]
    n_rows = row_tok.shape[0]
    gs = pltpu.PrefetchScalarGridSpec(
        num_scalar_prefetch=2,
        grid=(n_rows // tm,),
        in_specs=[pl.BlockSpec(memory_space=pl.ANY)],
        out_specs=pl.BlockSpec((tm, d), lambda i, tok, tv: (i, 0)),
        scratch_shapes=[pltpu.VMEM((2, tm, d), F32), pltpu.SemaphoreType.DMA((2,))])
    return pl.pallas_call(
        _gather_body, grid_spec=gs, name="moe_gather",
        out_shape=jax.ShapeDtypeStruct((n_rows, d), BF16),
        compiler_params=_cparams(1),
    )(row_tok, tile_valid, x)


def _moe(xn, logits, w1, w3, w2):
    n_tok = xn.shape[0]
    tm = MOE_TM
    top_val, top_idx = lax.top_k(logits, TOP_K)
    gate = jax.nn.softmax(top_val, axis=-1)
    n_assign = n_tok * TOP_K
    e_flat = top_idx.reshape(-1).astype(jnp.int32)
    experts = jnp.arange(N_EXPERTS, dtype=jnp.int32)
    order = jnp.argsort(e_flat).astype(jnp.int32)
    rank = jnp.argsort(order).astype(jnp.int32)
    counts = jnp.sum((e_flat[:, None] == experts[None, :]).astype(jnp.int32), axis=0)
    padded = (counts + tm - 1) // tm * tm
    pad_end = jnp.cumsum(padded)
    pad_start = pad_end - padded
    grp_start = jnp.cumsum(counts) - counts
    pos = (pad_start[e_flat] + rank - grp_start[e_flat]).reshape(n_tok, TOP_K)
    n_tiles = n_assign // tm + N_EXPERTS
    n_rows = n_tiles * tm

    def expert_of(row):
        return jnp.minimum(jnp.sum((row[:, None] >= pad_end[None, :]).astype(jnp.int32), axis=1),
                           N_EXPERTS - 1)

    rows = jnp.arange(n_rows, dtype=jnp.int32)
    e_row = expert_of(rows)
    off = rows - pad_start[e_row]
    src = jnp.clip(grp_start[e_row] + off, 0, n_assign - 1)
    row_tok = jnp.where(off < counts[e_row], order[src] // TOP_K, 0)
    tile_start = jnp.arange(n_tiles, dtype=jnp.int32) * tm
    n_valid = pad_end[-1] // tm
    tile_src = jnp.minimum(jnp.arange(n_tiles, dtype=jnp.int32), n_valid - 1).astype(jnp.int32)
    tile_exp = expert_of(tile_start)[tile_src]
    tile_valid = (tile_start < pad_end[-1]).astype(jnp.int32)
    group = (tile_exp, tile_valid, tile_src)

    def rows_of(a, idx):
        return a.at[idx].get(mode="promise_in_bounds")

    xs = _gather_rows_bf16(xn, row_tok.astype(jnp.int32), tile_valid, tm)
    hmid = _matmul(xs, [w1, w3], name="moe_up", tm=tm, tn=1024, out_dtype=BF16,
                   epilogue="swiglu", group=group)
    ys = _matmul(hmid, [w2], name="moe_down", tm=tm, tn=512, out_dtype=F32, group=group)
    return rows_of(ys, pos[:, 0]), rows_of(ys, pos[:, 1]), gate


def kernel(x, c, ctx, c_ctx, l0_ada_w, l0_ada_b, l0_norm1, l0_norm2, l0_w_in, l0_q_norm, l0_w_uq, l0_kv_norm, l0_w_ukv, l0_mu_prev, l0_mu_next, l0_w0, l0_w_up, l0_a0, l0_a_up, l0_g_up, l0_k_k, l0_k_a, l0_r_k, l0_lnx_w, l0_lnx_b, l0_w_o, l0_ffn_w1, l0_ffn_w3, l0_ffn_w2, l1_ada_w, l1_ada_b, l1_norm1, l1_norm2, l1_w_in, l1_v_ln_w, l1_v_ln_b, l1_w_s, l1_b_s, l1_w_o, l1_router, l1_moe_w1, l1_moe_w3, l1_moe_w2, final_norm):
    b, l, d = x.shape
    lc = ctx.shape[1]
    t = lc + l
    n_tok = b * l

    cond = jnp.zeros((8, d), F32).at[:b].set(c).at[b].set(c_ctx)
    mod0 = _matmul(cond, [l0_ada_w], name="ada_mod", tm=8, tn=1536, out_dtype=F32, prologue="silu",
                   epilogue="bias", bias=l0_ada_b)
    mod1 = _matmul(cond, [l1_ada_w], name="ada_mod", tm=8, tn=1536, out_dtype=F32, prologue="silu",
                   epilogue="bias", bias=l1_ada_b)

    def mods(mod, row0, nrows):
        return [mod[row0:row0 + nrows, i * d:(i + 1) * d].reshape(nrows, 1, d) for i in range(N_MOD)]

    sh1, sc1, g1, sh2, sc2, g2 = mods(mod0, 0, b)
    csh1, csc1 = mods(mod0, b, 1)[:2]

    xall = _norm_mod_merge(x, ctx, l0_norm1, sh1, sc1, csh1, csc1)
    mla_in = Q_LORA + KV_LORA + QK_ROPE
    w_kr = l0_w_in[:, Q_LORA + KV_LORA:mla_in]
    w_mla = jnp.concatenate([l0_w_in[:, :mla_in], _rope_swap_cols(w_kr)], axis=1)
    w_rw = jnp.pad(l0_w_in[:, mla_in:], ((0, 0), (0, RWKV_PAD - RWKV_IN)))
    xall2 = xall.reshape(b * t, d)
    z_all = _matmul(xall2, [w_mla], name="in_mla", tm=1088, tn=384, out_dtype=F32)
    z_rw = _matmul(xall2, [w_rw], name="in_rwkv", tm=1088, tn=512, out_dtype=F32)

    wq = l0_w_uq.reshape(Q_LORA, MLA_HEADS, QK_NOPE + QK_ROPE)
    wq_ext = jnp.concatenate([wq, _rope_swap_cols(wq[..., QK_NOPE:])], axis=-1)
    wq_ext = wq_ext.reshape(Q_LORA, MLA_HEADS * (QK_NOPE + 2 * QK_ROPE))
    tq_rows = 256
    per_b = l // tq_rows
    lat0 = lc // tq_rows

    def lat_rows(i):
        return (i // per_b) * (t // tq_rows) + lat0 + i % per_b

    q = _matmul(z_all, [wq_ext], name="q_up", tm=tq_rows, tn=2048, out_dtype=BF16, k=Q_LORA, x_col_block=0,
                x_row_map=lat_rows, m_out=n_tok, prologue="rms", gain=l0_q_norm)
    kv = _matmul(z_all, [l0_w_ukv], name="kv_up", tm=1088, tn=1024, out_dtype=BF16, k=KV_LORA, x_col_block=1,
                 prologue="rms", gain=l0_kv_norm)
    cos_l, sin_l = _rope_tables(l)
    cos_t = np.concatenate([np.ones((lc, QK_ROPE), np.float32), cos_l], axis=0)
    sin_t = np.concatenate([np.zeros((lc, QK_ROPE), np.float32), sin_l], axis=0)
    kr = _krope(z_all, cos_t, sin_t, col_block=(Q_LORA + KV_LORA) // LANES)
    att = _attention(q, kv.reshape(b, t, -1), kr.reshape(b, t, LANES), cos_l, sin_l, b)

    p0 = dict(mu_prev=l0_mu_prev, mu_next=l0_mu_next, w0=l0_w0, w_up=l0_w_up, a0=l0_a0,
              a_up=l0_a_up, g_up=l0_g_up, k_k=l0_k_k, k_a=l0_k_a, r_k=l0_r_k,
              lnx_w=l0_lnx_w, lnx_b=l0_lnx_b)
    mix = _rwkv_mixer(z_rw, att, p0, b, lc, t)
    h = _matmul(mix, [l0_w_o], name="mix_out", tm=1024, tn=512, out_dtype=F32, epilogue="resid",
                resid=x.reshape(n_tok, d), gate=g1, rows_per_gate=l)

    xn = _norm_mod(h.reshape(b, l, d), l0_norm2, sh2, sc2).reshape(n_tok, d)
    hmid = _matmul(xn, [l0_ffn_w1, l0_ffn_w3], name="ffn_up", tm=1024, tn=512, out_dtype=BF16,
                   epilogue="swiglu")
    h = _matmul(hmid, [l0_ffn_w2], name="ffn_down", tm=512, tn=512, out_dtype=F32, epilogue="resid",
                resid=h, gate=g2, rows_per_gate=l)

    sh1, sc1, g1, sh2, sc2, g2 = mods(mod1, 0, b)
    xn = _norm_mod(h.reshape(b, l, d), l1_norm1, sh1, sc1).reshape(n_tok, d)
    hg = _matmul(xn, [l1_w_in], name="gmlp_in", tm=1024, tn=512, out_dtype=BF16, epilogue="gelu")
    gated = _spatial_gate(hg, l1_v_ln_w, l1_v_ln_b, l1_w_s, l1_b_s)
    h = _matmul(gated, [l1_w_o], name="gmlp_out", tm=1024, tn=512, out_dtype=F32, epilogue="resid",
                resid=h, gate=g1, rows_per_gate=l)

    xn, logits = _norm_mod(h.reshape(b, l, d), l1_norm2, sh2, sc2, router=l1_router)
    ya, yb, gate = _moe(xn.reshape(n_tok, d), logits.reshape(n_tok, LANES)[:, :N_EXPERTS],
                        l1_moe_w1, l1_moe_w3, l1_moe_w2)
    return _combine_norm(h, ya, yb, gate, g2, final_norm, l).reshape(b, l, d)
```
